```python
import jax, jax.numpy as jnp
from jax import lax
import numpy as np

D_MODEL = 1024
BATCH = 4
SEQ = 4096
DEPTH = 4

N_A_LAYERS = DEPTH // 2
N_B_LAYERS = DEPTH - N_A_LAYERS

GLA_HEADS = 4
GLA_DK = D_MODEL // 2 // GLA_HEADS
GLA_DV = D_MODEL // GLA_HEADS
GLA_GATE_RANK = 16
GLA_GATE_NORM = 16.0
GLA_CHUNK = 64

FOX_HEADS = 16
FOX_DH = D_MODEL // FOX_HEADS
FOX_BLOCK = 128

N_EXPERTS = 16
N_GROUPS = 4
EXPERTS_PER_GROUP = N_EXPERTS // N_GROUPS
TOP_K = 2
D_EXPERT = D_MODEL // 2

ALPHA = float((2 * DEPTH) ** 0.25)
BETA = float((8 * DEPTH) ** -0.25)
LN_EPS = 1e-5
RMS_EPS = 1e-6

kernel_name = "yoco_gla_fox_grouped_moe_deepnorm"


def layer_norm(x, g, b):
    xf = x.astype(jnp.float32)
    mu = jnp.mean(xf, axis=-1, keepdims=True)
    var = jnp.mean(jnp.square(xf - mu), axis=-1, keepdims=True)
    return ((xf - mu) * lax.rsqrt(var + LN_EPS) * g.astype(jnp.float32) + b.astype(jnp.float32)).astype(x.dtype)


def gla_mixer(x, w_in, w_gk, b_gk, norm_g, w_out):
    B, S, _ = x.shape
    HK = GLA_HEADS * GLA_DK
    HV = GLA_HEADS * GLA_DV
    C = GLA_CHUNK
    N = S // C
    proj = x @ w_in
    q, k, v, g, gr = jnp.split(proj, [HK, 2 * HK, 2 * HK + HV, 2 * HK + 2 * HV], axis=-1)
    gk = jax.nn.log_sigmoid((gr @ w_gk + b_gk).astype(jnp.float32)) / GLA_GATE_NORM

    def to_chunks(t, d):
        return t.reshape(B, N, C, GLA_HEADS, d).transpose(0, 1, 3, 2, 4)

    qc = to_chunks(q.astype(jnp.float32), GLA_DK) * (GLA_DK ** -0.5)
    kc = to_chunks(k.astype(jnp.float32), GLA_DK)
    vc = to_chunks(v.astype(jnp.float32), GLA_DV)
    bc = jnp.cumsum(to_chunks(gk, GLA_DK), axis=3)
    b_last = bc[:, :, :, -1:, :]
    q_dec = qc * jnp.exp(bc)
    k_inv = kc * jnp.exp(-bc)
    k_end = kc * jnp.exp(b_last - bc)

    causal = jnp.tril(jnp.ones((C, C), dtype=bool))
    attn = jnp.where(causal, jnp.einsum('bnhcd,bnhsd->bnhcs', q_dec, k_inv), 0.0)
    o_intra = jnp.einsum('bnhcs,bnhsv->bnhcv', attn, vc)

    kv_chunk = jnp.einsum('bnhcd,bnhcv->nbhdv', k_end, vc)
    chunk_decay = jnp.exp(b_last[:, :, :, 0, :]).transpose(1, 0, 2, 3)

    def step(state, inp):
        kv_n, dec_n = inp
        return state * dec_n[..., None] + kv_n, state

    init = jnp.zeros((B, GLA_HEADS, GLA_DK, GLA_DV), jnp.float32)
    _, states = lax.scan(step, init, (kv_chunk, chunk_decay))
    o_inter = jnp.einsum('bnhcd,nbhdv->bnhcv', q_dec, states)

    o = (o_intra + o_inter).transpose(0, 1, 3, 2, 4).reshape(B, S, GLA_HEADS, GLA_DV)
    o = o * lax.rsqrt(jnp.mean(jnp.square(o), axis=-1, keepdims=True) + RMS_EPS) * norm_g.astype(jnp.float32)
    o = o * jax.nn.silu(g.astype(jnp.float32).reshape(B, S, GLA_HEADS, GLA_DV))
    return (o.reshape(B, S, HV).astype(x.dtype) @ w_out).astype(x.dtype)


def fox_shared_kv(h, kv_w, forget_bias):
    B, S, _ = h.shape
    FD = FOX_HEADS * FOX_DH
    proj = h @ kv_w
    k, v, fl = jnp.split(proj, [FD, 2 * FD], axis=-1)
    k = k.reshape(B, S, FOX_HEADS, FOX_DH).astype(jnp.float32)
    v = v.reshape(B, S, FOX_HEADS, FOX_DH).astype(jnp.float32)
    log_f = jax.nn.log_sigmoid(fl.astype(jnp.float32) + forget_bias.astype(jnp.float32))
    cum = jnp.cumsum(log_f, axis=1)
    return k, v, cum


def fox_mixer(x, k, v, cum, w_qg, w_out):
    B, S, _ = x.shape
    FD = FOX_HEADS * FOX_DH
    nb = S // FOX_BLOCK
    proj = x @ w_qg
    q, g = jnp.split(proj, [FD], axis=-1)
    q = q.reshape(B, S, FOX_HEADS, FOX_DH).astype(jnp.float32) * (FOX_DH ** -0.5)
    qb = q.reshape(B, nb, FOX_BLOCK, FOX_HEADS, FOX_DH).transpose(1, 0, 2, 3, 4)
    cum_bhs = cum.transpose(0, 2, 1)
    cqb = cum_bhs.reshape(B, FOX_HEADS, nb, FOX_BLOCK).transpose(2, 0, 1, 3)
    key_pos = jnp.arange(S)

    def block(args):
        q_blk, c_blk, i = args
        s = jnp.einsum('bqhd,bkhd->bhqk', q_blk, k)
        decay = c_blk[..., :, None] - cum_bhs[..., None, :]
        q_pos = i * FOX_BLOCK + jnp.arange(FOX_BLOCK)
        mask = key_pos[None, :] <= q_pos[:, None]
        p = jax.nn.softmax(jnp.where(mask, s + decay, -jnp.inf), axis=-1)
        return jnp.einsum('bhqk,bkhd->bqhd', p, v)

    o = lax.map(block, (qb, cqb, jnp.arange(nb)))
    o = o.transpose(1, 0, 2, 3, 4).reshape(B, S, FD)
    o = o * jax.nn.sigmoid(g.astype(jnp.float32))
    return (o.astype(x.dtype) @ w_out).astype(x.dtype)


def grouped_moe(x, router_w, router_bias, w_gate, w_up, w_down):
    B, S, D = x.shape
    xt = x.reshape(-1, D)
    T = xt.shape[0]
    scores = jax.nn.softmax((xt @ router_w).astype(jnp.float32), axis=-1)
    sel = scores + router_bias.astype(jnp.float32)
    grouped = sel.reshape(T, N_GROUPS, EXPERTS_PER_GROUP)
    group_score = jnp.sum(lax.top_k(grouped, TOP_K)[0], axis=-1)
    best_group = jnp.argmax(group_score, axis=-1)
    in_group = jnp.take_along_axis(grouped, best_group[:, None, None], axis=1)[:, 0]
    _, local_idx = lax.top_k(in_group, TOP_K)
    expert_idx = best_group[:, None] * EXPERTS_PER_GROUP + local_idx
    w = jnp.take_along_axis(scores, expert_idx, axis=-1)
    w = w / jnp.sum(w, axis=-1, keepdims=True)
    gates = jnp.sum(jax.nn.one_hot(expert_idx, N_EXPERTS, dtype=jnp.float32) * w[..., None], axis=1)
    h = jax.nn.silu(jnp.einsum('td,edf->tef', xt, w_gate)) * jnp.einsum('td,edf->tef', xt, w_up)
    h = h * gates[..., None].astype(h.dtype)
    y = jnp.einsum('tef,efd->td', h, w_down)
    return y.reshape(B, S, D).astype(x.dtype)


def setup_inputs(seed: int = 0) -> dict:
    key = jax.random.key(seed)
    ks = jax.random.split(key, 24)

    def nrm(k, shape, scale):
        return jax.random.normal(k, shape, jnp.float32) * scale

    HK = GLA_HEADS * GLA_DK
    HV = GLA_HEADS * GLA_DV
    FD = FOX_HEADS * FOX_DH
    s_in = D_MODEL ** -0.5
    x = nrm(ks[0], (BATCH, SEQ, D_MODEL), 1.0)
    gla_w_in = jnp.concatenate([
        nrm(ks[1], (N_A_LAYERS, D_MODEL, 2 * HK), s_in),
        nrm(ks[2], (N_A_LAYERS, D_MODEL, HV), s_in * BETA),
        nrm(ks[3], (N_A_LAYERS, D_MODEL, HV + GLA_GATE_RANK), s_in),
    ], axis=-1)
    gla_w_gk = nrm(ks[4], (N_A_LAYERS, GLA_GATE_RANK, HK), GLA_GATE_RANK ** -0.5)
    gla_b_gk = nrm(ks[5], (N_A_LAYERS, HK), 0.1)
    gla_norm_g = 1.0 + nrm(ks[6], (N_A_LAYERS, GLA_DV), 0.02)
    gla_w_out = nrm(ks[7], (N_A_LAYERS, HV, D_MODEL), (HV ** -0.5) * BETA)
    kv_w = jnp.concatenate([
        nrm(ks[8], (D_MODEL, FD), s_in),
        nrm(ks[9], (D_MODEL, FD), s_in * BETA),
        nrm(ks[10], (D_MODEL, FOX_HEADS), s_in),
    ], axis=-1)
    forget_bias = jax.random.uniform(ks[11], (FOX_HEADS,), jnp.float32, minval=1.0, maxval=4.0)
    fox_w_qg = nrm(ks[12], (N_B_LAYERS, D_MODEL, 2 * FD), s_in)
    fox_w_out = nrm(ks[13], (N_B_LAYERS, FD, D_MODEL), (FD ** -0.5) * BETA)
    router_w = nrm(ks[14], (D_MODEL, N_EXPERTS), s_in)
    router_bias = nrm(ks[15], (N_EXPERTS,), 0.01)
    moe_w_gate = nrm(ks[16], (DEPTH, N_EXPERTS, D_MODEL, D_EXPERT), s_in)
    moe_w_up = nrm(ks[17], (DEPTH, N_EXPERTS, D_MODEL, D_EXPERT), s_in)
    moe_w_down = nrm(ks[18], (DEPTH, N_EXPERTS, D_EXPERT, D_MODEL), (D_EXPERT ** -0.5) * BETA)
    ln_g = 1.0 + nrm(ks[19], (DEPTH, 2, D_MODEL), 0.02)
    ln_b = nrm(ks[20], (DEPTH, 2, D_MODEL), 0.02)
    return {"x": x, "gla_w_in": gla_w_in, "gla_w_gk": gla_w_gk, "gla_b_gk": gla_b_gk,
            "gla_norm_g": gla_norm_g, "gla_w_out": gla_w_out, "kv_w": kv_w,
            "forget_bias": forget_bias, "fox_w_qg": fox_w_qg, "fox_w_out": fox_w_out,
            "router_w": router_w, "router_bias": router_bias, "moe_w_gate": moe_w_gate,
            "moe_w_up": moe_w_up, "moe_w_down": moe_w_down, "ln_g": ln_g, "ln_b": ln_b}


def reference(x, gla_w_in, gla_w_gk, gla_b_gk, gla_norm_g, gla_w_out, kv_w, forget_bias,
              fox_w_qg, fox_w_out, router_w, router_bias, moe_w_gate, moe_w_up, moe_w_down,
              ln_g, ln_b):
    h = x
    k_sh = v_sh = cum_sh = None
    for layer in range(DEPTH):
        if layer < N_A_LAYERS:
            mix = gla_mixer(h, gla_w_in[layer], gla_w_gk[layer], gla_b_gk[layer],
                            gla_norm_g[layer], gla_w_out[layer])
        else:
            j = layer - N_A_LAYERS
            mix = fox_mixer(h, k_sh, v_sh, cum_sh, fox_w_qg[j], fox_w_out[j])
        h = layer_norm(ALPHA * h + mix, ln_g[layer, 0], ln_b[layer, 0])
        ffn = grouped_moe(h, router_w, router_bias, moe_w_gate[layer], moe_w_up[layer], moe_w_down[layer])
        h = layer_norm(ALPHA * h + ffn, ln_g[layer, 1], ln_b[layer, 1])
        if layer == N_A_LAYERS - 1:
            k_sh, v_sh, cum_sh = fox_shared_kv(h, kv_w, forget_bias)
    return h
```

```python
import functools
import math

import jax
import jax.numpy as jnp
from jax import lax
from jax.experimental import pallas as pl
from jax.experimental.pallas import tpu as pltpu

F32 = jnp.float32
BF16 = jnp.bfloat16
I32 = jnp.int32
HIGHEST = lax.Precision.HIGHEST

D_MODEL = 1024
DEPTH = 4
N_A_LAYERS = DEPTH // 2

GLA_HEADS = 4
GLA_DK = 128
GLA_DV = 256
GLA_RANK = 16
GLA_GATE_NORM = 16.0
GLA_CHUNK = 64
GLA_HK = GLA_HEADS * GLA_DK
GLA_HV = GLA_HEADS * GLA_DV

FOX_HEADS = 16
FOX_DH = 64
FOX_FD = FOX_HEADS * FOX_DH

N_EXPERTS = 16
N_GROUPS = 4
EPG = 4
D_EXPERT = 512
N_PAIRS = 6
N_CLASSES = N_GROUPS * N_PAIRS
CLASS_PAD = 32

ALPHA = float((2 * DEPTH) ** 0.25)
LN_EPS = 1e-5
RMS_EPS = 1e-6

LANES = 128
VMEM_LIMIT = 48 * 1024 * 1024

TM_PROJ = 512
TN_PROJ = 512
GLA_ROWS = 256
TM_POST = 256
TM_RANK = 512
TM_ROWS = 512
TM_MOE = 256
FOX_TQ = 512


def _cparams(n_axes):
    return pltpu.CompilerParams(dimension_semantics=("arbitrary",) * n_axes,
                                vmem_limit_bytes=VMEM_LIMIT)


def _log_sigmoid(x):
    return jnp.minimum(x, 0.0) - jnp.log(1.0 + jnp.exp(-jnp.abs(x)))


def _sigmoid(x):
    return 1.0 / (1.0 + jnp.exp(-x))


def _layer_norm(z, g, b):
    mu = jnp.mean(z, axis=-1, keepdims=True)
    zc = z - mu
    var = jnp.mean(zc * zc, axis=-1, keepdims=True)
    return zc * lax.rsqrt(var + LN_EPS) * g + b


def _proj_kernel(x_ref, *refs, n_w):
    xb = x_ref[...].astype(BF16)
    for w_ref, o_ref in zip(refs[:n_w], refs[n_w:]):
        n = o_ref.shape[1]
        tn = min(TN_PROJ, n)
        for c in range(n // tn):
            o_ref[:, c * tn:(c + 1) * tn] = jnp.dot(
                xb, w_ref[:, c * tn:(c + 1) * tn], preferred_element_type=F32).astype(o_ref.dtype)


def _proj(x, ws, out_dtypes):
    t, k = x.shape
    tm = TM_PROJ
    return pl.pallas_call(
        functools.partial(_proj_kernel, n_w=len(ws)),
        grid=(t // tm,),
        in_specs=[pl.BlockSpec((tm, k), lambda i: (i, 0))]
        + [pl.BlockSpec(w.shape, lambda i: (0, 0)) for w in ws],
        out_specs=[pl.BlockSpec((tm, w.shape[1]), lambda i: (i, 0)) for w in ws],
        out_shape=[jax.ShapeDtypeStruct((t, w.shape[1]), dt) for w, dt in zip(ws, out_dtypes)],
        compiler_params=_cparams(1),
        name="proj",
    )(x, *ws)


def _gla_kernel(q_ref, k_ref, v_ref, g_ref, gr_ref, wgk_ref, bgk_ref, ng_ref, o_ref, state_ref):
    c = GLA_CHUNK

    @pl.when(pl.program_id(1) == 0)
    def _():
        state_ref[...] = jnp.zeros_like(state_ref)

    gkz = jnp.dot(gr_ref[...], wgk_ref[...], precision=HIGHEST,
                  preferred_element_type=F32) + bgk_ref[...]
    gk = _log_sigmoid(gkz) * (1.0 / GLA_GATE_NORM)
    row = lax.broadcasted_iota(I32, (c, c), 0)
    col = lax.broadcasted_iota(I32, (c, c), 1)
    causal = col <= row
    tril = causal.astype(F32)
    scale = GLA_DK ** -0.5
    ng = ng_ref[...]

    for ci in range(GLA_ROWS // c):
        rs = slice(ci * c, (ci + 1) * c)
        bc = jnp.dot(tril, gk[rs], precision=HIGHEST, preferred_element_type=F32)
        b_last = bc[c - 1:c, :]
        qf = q_ref[rs, :].astype(F32)
        kf = k_ref[rs, :].astype(F32)
        q_dec = (qf * scale * jnp.exp(bc)).astype(BF16)
        k_inv = (kf * jnp.exp(-bc)).astype(BF16)
        k_end = (kf * jnp.exp(b_last - bc)).astype(BF16)
        dec = jnp.exp(b_last)
        for h in range(GLA_HEADS):
            ks = slice(h * GLA_DK, (h + 1) * GLA_DK)
            vs = slice(h * GLA_DV, (h + 1) * GLA_DV)
            v_h = v_ref[rs, vs]
            attn = lax.dot_general(q_dec[:, ks], k_inv[:, ks], (((1,), (1,)), ((), ())),
                                   preferred_element_type=F32)
            attn = jnp.where(causal, attn, 0.0).astype(BF16)
            st = state_ref[h]
            o = jnp.dot(attn, v_h, preferred_element_type=F32)
            o = o + lax.dot_general(q_dec[:, ks], st.astype(BF16), (((1,), (1,)), ((), ())),
                                    preferred_element_type=F32)
            kv_t = lax.dot_general(v_h, k_end[:, ks], (((0,), (0,)), ((), ())),
                                   preferred_element_type=F32)
            state_ref[h] = st * dec[:, ks] + kv_t
            o = o * lax.rsqrt(jnp.mean(o * o, axis=-1, keepdims=True) + RMS_EPS) * ng
            gate = g_ref[rs, vs].astype(F32)
            o = o * (gate * _sigmoid(gate))
            o_ref[rs, vs] = o.astype(o_ref.dtype)


def _gla_core(qkvg, gr, wgk_pad, bgk, ng, batch, seq):
    t = batch * seq
    r = GLA_ROWS
    nblk = seq // r
    rowmap = lambda b, i: b * nblk + i
    return pl.pallas_call(
        _gla_kernel,
        grid=(batch, nblk),
        in_specs=[
            pl.BlockSpec((r, GLA_HK), lambda b, i: (rowmap(b, i), 0)),
            pl.BlockSpec((r, GLA_HK), lambda b, i: (rowmap(b, i), 1)),
            pl.BlockSpec((r, GLA_HV), lambda b, i: (rowmap(b, i), 1)),
            pl.BlockSpec((r, GLA_HV), lambda b, i: (rowmap(b, i), 2)),
            pl.BlockSpec((r, LANES), lambda b, i: (rowmap(b, i), 0)),
            pl.BlockSpec((LANES, GLA_HK), lambda b, i: (0, 0)),
            pl.BlockSpec((1, GLA_HK), lambda b, i: (0, 0)),
            pl.BlockSpec((1, GLA_DV), lambda b, i: (0, 0)),
        ],
        out_specs=pl.BlockSpec((r, GLA_HV), lambda b, i: (rowmap(b, i), 0)),
        out_shape=jax.ShapeDtypeStruct((t, GLA_HV), BF16),
        scratch_shapes=[pltpu.VMEM((GLA_HEADS, GLA_DV, GLA_DK), F32)],
        compiler_params=_cparams(2),
        name="gla_core",
    )(qkvg, qkvg, qkvg, qkvg, gr, wgk_pad, bgk, ng)


def _route_class(logits_t, bias_col):
    mx = jnp.max(logits_t, axis=0, keepdims=True)
    e = jnp.exp(logits_t - mx)
    scores = e / jnp.sum(e, axis=0, keepdims=True)
    sel = scores + bias_col
    rows = [sel[j:j + 1, :] for j in range(N_EXPERTS)]
    best_g = None
    best_s = None
    for g in range(N_GROUPS):
        m = rows[g * EPG:(g + 1) * EPG]
        gs = None
        for a in range(EPG):
            for b in range(a + 1, EPG):
                s = m[a] + m[b]
                gs = s if gs is None else jnp.maximum(gs, s)
        if g == 0:
            best_g = jnp.zeros(gs.shape, I32)
            best_s = gs
        else:
            better = gs > best_s
            best_g = jnp.where(better, g, best_g)
            best_s = jnp.where(better, gs, best_s)
    mem = []
    for j in range(EPG):
        vj = rows[j]
        for g in range(1, N_GROUPS):
            vj = jnp.where(best_g == g, rows[g * EPG + j], vj)
        mem.append(vj)
    i1 = jnp.zeros(best_g.shape, I32)
    b1 = mem[0]
    for j in range(1, EPG):
        better = mem[j] > b1
        i1 = jnp.where(better, j, i1)
        b1 = jnp.where(better, mem[j], b1)
    i2 = jnp.where(i1 == 0, 1, 0).astype(I32)
    b2 = jnp.where(i1 == 0, mem[1], mem[0])
    for j in range(1, EPG):
        better = (mem[j] > b2) & (i1 != j) & (i2 != j)
        i2 = jnp.where(better, j, i2)
        b2 = jnp.where(better, mem[j], b2)
    lo = jnp.minimum(i1, i2)
    hi = jnp.maximum(i1, i2)
    pair = jnp.where(lo == 0, hi - 1, jnp.where(lo == 1, hi + 1, 5))
    return best_g * N_PAIRS + pair


def _post_kernel(o_ref, w_ref, h_ref, g_ref, b_ref, rwt_ref, rb_ref, h1_ref, cls_ref):
    mix = jnp.dot(o_ref[...], w_ref[...], preferred_element_type=F32)
    h1 = _layer_norm(ALPHA * h_ref[...] + mix, g_ref[...], b_ref[...])
    h1_ref[...] = h1
    logits_t = lax.dot_general(rwt_ref[...], h1, (((1,), (1,)), ((), ())),
                               precision=HIGHEST, preferred_element_type=F32)
    cls_ref[...] = _route_class(logits_t, rb_ref[...])


def _post_mixer(o, w_out, h, ln_g, ln_b, rwt, rb_col):
    t, d = h.shape
    tm = TM_POST
    kdim = o.shape[1]
    return pl.pallas_call(
        _post_kernel,
        grid=(t // tm,),
        in_specs=[
            pl.BlockSpec((tm, kdim), lambda i: (i, 0)),
            pl.BlockSpec((kdim, d), lambda i: (0, 0)),
            pl.BlockSpec((tm, d), lambda i: (i, 0)),
            pl.BlockSpec((1, d), lambda i: (0, 0)),
            pl.BlockSpec((1, d), lambda i: (0, 0)),
            pl.BlockSpec((N_EXPERTS, d), lambda i: (0, 0)),
            pl.BlockSpec((N_EXPERTS, 1), lambda i: (0, 0)),
        ],
        out_specs=[pl.BlockSpec((tm, d), lambda i: (i, 0)),
                   pl.BlockSpec((1, tm), lambda i: (0, i))],
        out_shape=[jax.ShapeDtypeStruct((t, d), F32), jax.ShapeDtypeStruct((1, t), I32)],
        compiler_params=_cparams(1),
        name="post_mixer",
    )(o, w_out, h, ln_g, ln_b, rwt, rb_col)


def _rank_kernel(cls_ref, pos_ref, toff_ref, cnt_ref, carry_ref, off_ref, tri_ref):
    phase = pl.program_id(0)
    i = pl.program_id(1)
    tm = cls_ref.shape[1]
    onehot = (lax.broadcasted_iota(I32, (CLASS_PAD, tm), 0) == cls_ref[...]).astype(F32)
    tile_count = jnp.sum(onehot, axis=1, keepdims=True)

    @pl.when((phase == 0) & (i == 0))
    def _():
        cnt_ref[...] = jnp.zeros_like(cnt_ref)
        r = lax.broadcasted_iota(I32, (tm, tm), 0)
        cc = lax.broadcasted_iota(I32, (tm, tm), 1)
        tri_ref[...] = (r <= cc).astype(BF16)

    @pl.when(phase == 0)
    def _():
        cnt_ref[...] += jnp.broadcast_to(tile_count, cnt_ref.shape)

    @pl.when((phase == 1) & (i == 0))
    def _():
        ntile = jnp.floor((cnt_ref[...] + (TM_MOE - 1)) * (1.0 / TM_MOE))
        r = lax.broadcasted_iota(I32, (CLASS_PAD, CLASS_PAD), 0)
        cc = lax.broadcasted_iota(I32, (CLASS_PAD, CLASS_PAD), 1)
        strict = (cc < r).astype(BF16)
        first_tile = jnp.dot(strict, ntile.astype(BF16), preferred_element_type=F32)
        toff_ref[...] = first_tile.astype(I32)
        off_ref[...] = first_tile * float(TM_MOE)
        carry_ref[...] = jnp.zeros_like(carry_ref)

    @pl.when(phase == 1)
    def _():
        prefix = jnp.dot(onehot.astype(BF16), tri_ref[...], preferred_element_type=F32)
        base = off_ref[:, 0:1] + carry_ref[:, 0:1] - 1.0
        posf = jnp.sum(onehot * (prefix + base), axis=0, keepdims=True)
        pos_ref[...] = posf.astype(I32)
        carry_ref[...] += jnp.broadcast_to(tile_count, carry_ref.shape)


def _rank(cls):
    t = cls.shape[1]
    tm = TM_RANK
    return pl.pallas_call(
        _rank_kernel,
        grid=(2, t // tm),
        in_specs=[pl.BlockSpec((1, tm), lambda p, i: (0, i))],
        out_specs=[pl.BlockSpec((1, tm), lambda p, i: (0, i * p)),
                   pl.BlockSpec((CLASS_PAD, LANES), lambda p, i: (0, 0))],
        out_shape=[jax.ShapeDtypeStruct((1, t), I32),
                   jax.ShapeDtypeStruct((CLASS_PAD, LANES), I32)],
        scratch_shapes=[pltpu.VMEM((CLASS_PAD, LANES), F32),
                        pltpu.VMEM((CLASS_PAD, LANES), F32),
                        pltpu.VMEM((CLASS_PAD, LANES), F32),
                        pltpu.VMEM((tm, tm), BF16)],
        compiler_params=_cparams(2),
        name="rank",
    )(cls)


def _dispatch_kernel(pos_ref, h_ref, xs_in_ref, xs_ref, sem):
    del xs_in_ref
    tm = h_ref.shape[0]
    base = pl.program_id(0) * tm

    def issue(r, carry):
        p = pos_ref[base + r]
        pltpu.make_async_copy(h_ref.at[pl.ds(r, 1), :], xs_ref.at[pl.ds(p, 1), :], sem).start()
        return carry

    lax.fori_loop(0, tm, issue, 0)
    pltpu.make_async_copy(h_ref, xs_ref.at[pl.ds(0, tm), :], sem).wait()


def _dispatch(pos, h1, n_sorted):
    t, d = h1.shape
    tm = TM_ROWS
    xs0 = jnp.zeros((n_sorted, d), F32)
    return pl.pallas_call(
        _dispatch_kernel,
        grid_spec=pltpu.PrefetchScalarGridSpec(
            num_scalar_prefetch=1,
            grid=(t // tm,),
            in_specs=[pl.BlockSpec((tm, d), lambda i, pos: (i, 0)),
                      pl.BlockSpec(memory_space=pl.ANY)],
            out_specs=pl.BlockSpec(memory_space=pl.ANY),
            scratch_shapes=[pltpu.SemaphoreType.DMA(())],
        ),
        out_shape=jax.ShapeDtypeStruct((n_sorted, d), F32),
        input_output_aliases={2: 0},
        compiler_params=_cparams(1),
        name="dispatch",
    )(pos, h1, xs0)


def _combine_kernel(pos_ref, h_ref, ys_ref, g_ref, b_ref, o_ref, buf_ref, sem):
    tm = h_ref.shape[0]
    base = pl.program_id(0) * tm

    def issue(r, carry):
        p = pos_ref[base + r]
        pltpu.make_async_copy(ys_ref.at[pl.ds(p, 1), :], buf_ref.at[pl.ds(r, 1), :], sem).start()
        return carry

    lax.fori_loop(0, tm, issue, 0)
    pltpu.make_async_copy(ys_ref.at[pl.ds(0, tm), :], buf_ref, sem).wait()
    o_ref[...] = _layer_norm(ALPHA * h_ref[...] + buf_ref[...], g_ref[...], b_ref[...])


def _combine(pos, h1, ys, ln_g, ln_b):
    t, d = h1.shape
    tm = TM_ROWS
    return pl.pallas_call(
        _combine_kernel,
        grid_spec=pltpu.PrefetchScalarGridSpec(
            num_scalar_prefetch=1,
            grid=(t // tm,),
            in_specs=[pl.BlockSpec((tm, d), lambda i, pos: (i, 0)),
                      pl.BlockSpec(memory_space=pl.ANY),
                      pl.BlockSpec((1, d), lambda i, pos: (0, 0)),
                      pl.BlockSpec((1, d), lambda i, pos: (0, 0))],
            out_specs=pl.BlockSpec((tm, d), lambda i, pos: (i, 0)),
            scratch_shapes=[pltpu.VMEM((tm, d), F32), pltpu.SemaphoreType.DMA(())],
        ),
        out_shape=jax.ShapeDtypeStruct((t, d), F32),
        compiler_params=_cparams(1),
        name="combine",
    )(pos, h1, ys, ln_g, ln_b)


def _moe_kernel(ea_ref, eb_ref, valid_ref, xs_ref, rw_ref, wga_ref, wua_ref, wda_ref,
                wgb_ref, wub_ref, wdb_ref, ys_ref):
    i = pl.program_id(0)

    @pl.when(valid_ref[i] == 0)
    def _():
        ys_ref[...] = jnp.zeros_like(ys_ref)

    @pl.when(valid_ref[i] != 0)
    def _():
        x = xs_ref[...]
        logits = jnp.dot(x, rw_ref[...], precision=HIGHEST, preferred_element_type=F32)
        e = jnp.exp(logits - jnp.max(logits, axis=1, keepdims=True))
        scores = e / jnp.sum(e, axis=1, keepdims=True)
        lane = lax.broadcasted_iota(I32, scores.shape, 1)
        sa = jnp.sum(jnp.where(lane == ea_ref[i], scores, 0.0), axis=1, keepdims=True)
        sb = jnp.sum(jnp.where(lane == eb_ref[i], scores, 0.0), axis=1, keepdims=True)
        tot = sa + sb
        xb = x.astype(BF16)

        def expert(wg_ref, wu_ref, wd_ref, gate):
            a = jnp.dot(xb, wg_ref[...], preferred_element_type=F32)
            u = jnp.dot(xb, wu_ref[...], preferred_element_type=F32)
            hid = (a * _sigmoid(a)) * u * gate
            return jnp.dot(hid.astype(BF16), wd_ref[...], preferred_element_type=F32)

        y = expert(wga_ref, wua_ref, wda_ref, sa / tot)
        y = y + expert(wgb_ref, wub_ref, wdb_ref, sb / tot)
        ys_ref[...] = y


def _moe(ea, eb, valid, xs, rw, wg, wu, wd):
    n_sorted, d = xs.shape
    tm = TM_MOE
    f = wg.shape[2]
    up_a = pl.BlockSpec((None, d, f), lambda i, ea, eb, va: (ea[i], 0, 0))
    up_b = pl.BlockSpec((None, d, f), lambda i, ea, eb, va: (eb[i], 0, 0))
    dn_a = pl.BlockSpec((None, f, d), lambda i, ea, eb, va: (ea[i], 0, 0))
    dn_b = pl.BlockSpec((None, f, d), lambda i, ea, eb, va: (eb[i], 0, 0))
    return pl.pallas_call(
        _moe_kernel,
        grid_spec=pltpu.PrefetchScalarGridSpec(
            num_scalar_prefetch=3,
            grid=(n_sorted // tm,),
            in_specs=[pl.BlockSpec((tm, d), lambda i, ea, eb, va: (i, 0)),
                      pl.BlockSpec((d, N_EXPERTS), lambda i, ea, eb, va: (0, 0)),
                      up_a, up_a, dn_a, up_b, up_b, dn_b],
            out_specs=pl.BlockSpec((tm, d), lambda i, ea, eb, va: (i, 0)),
        ),
        out_shape=jax.ShapeDtypeStruct((n_sorted, d), F32),
        compiler_params=_cparams(1),
        name="moe",
    )(ea, eb, valid, xs, rw, wg, wu, wd, wg, wu, wd)


_PAIR_LO = (0, 0, 0, 1, 1, 2)
_PAIR_HI = (1, 2, 3, 2, 3, 3)


def _tile_plan(toff, n_tiles):
    first = toff[:N_CLASSES, 0]
    tiles = jnp.arange(n_tiles, dtype=I32)
    cls = jnp.sum((first[None, :] <= tiles[:, None]).astype(I32), axis=1) - 1
    group = cls // N_PAIRS
    pair = cls % N_PAIRS
    lo = jnp.asarray(_PAIR_LO, I32)[pair]
    hi = jnp.asarray(_PAIR_HI, I32)[pair]
    total = toff[N_CLASSES, 0]
    valid = (tiles < total).astype(I32)
    return group * EPG + lo, group * EPG + hi, valid


def _moe_block(h1, cls, rw, wg, wu, wd, ln_g, ln_b):
    t, _ = h1.shape
    n_tiles = t // TM_MOE + N_CLASSES
    pos2d, toff = _rank(cls)
    pos = pos2d.reshape(t)
    ea, eb, valid = _tile_plan(toff, n_tiles)
    xs = _dispatch(pos, h1, n_tiles * TM_MOE)
    ys = _moe(ea, eb, valid, xs, rw, wg, wu, wd)
    return _combine(pos, h1, ys, ln_g, ln_b)


def _kv_kernel(x_ref, wk_ref, wv_ref, wf_ref, fb_ref, k_ref, v_ref, cum_ref, cumt_ref,
               carry_ref, *, tiles_per_seq):
    i = pl.program_id(0)
    tm = x_ref.shape[0]
    xb = x_ref[...].astype(BF16)
    for w_ref, o_ref in ((wk_ref, k_ref), (wv_ref, v_ref)):
        n = o_ref.shape[1]
        for c in range(n // TN_PROJ):
            cs = slice(c * TN_PROJ, (c + 1) * TN_PROJ)
            o_ref[:, cs] = jnp.dot(xb, w_ref[:, cs], preferred_element_type=F32).astype(o_ref.dtype)
    fl = jnp.dot(xb, wf_ref[...], preferred_element_type=F32) + fb_ref[...]
    log_f = _log_sigmoid(fl)

    @pl.when(i % tiles_per_seq == 0)
    def _():
        carry_ref[...] = jnp.zeros_like(carry_ref)

    r = lax.broadcasted_iota(I32, (tm, tm), 0)
    cc = lax.broadcasted_iota(I32, (tm, tm), 1)
    tril = (cc <= r).astype(F32)
    cum = jnp.dot(tril, log_f, precision=HIGHEST, preferred_element_type=F32) + carry_ref[...]
    cum_ref[...] = cum
    cumt_ref[...] = cum.T
    carry_ref[...] = cum[tm - 1:tm, :]


def _fox_kv(h, wk, wv, wf_pad, fb_pad, seq):
    t, d = h.shape
    tm = TM_PROJ
    return pl.pallas_call(
        functools.partial(_kv_kernel, tiles_per_seq=seq // tm),
        grid=(t // tm,),
        in_specs=[pl.BlockSpec((tm, d), lambda i: (i, 0)),
                  pl.BlockSpec(wk.shape, lambda i: (0, 0)),
                  pl.BlockSpec(wv.shape, lambda i: (0, 0)),
                  pl.BlockSpec(wf_pad.shape, lambda i: (0, 0)),
                  pl.BlockSpec((1, LANES), lambda i: (0, 0))],
        out_specs=[pl.BlockSpec((tm, FOX_FD), lambda i: (i, 0)),
                   pl.BlockSpec((tm, FOX_FD), lambda i: (i, 0)),
                   pl.BlockSpec((tm, LANES), lambda i: (i, 0)),
                   pl.BlockSpec((LANES, tm), lambda i: (0, i))],
        out_shape=[jax.ShapeDtypeStruct((t, FOX_FD), BF16),
                   jax.ShapeDtypeStruct((t, FOX_FD), BF16),
                   jax.ShapeDtypeStruct((t, LANES), F32),
                   jax.ShapeDtypeStruct((LANES, t), F32)],
        scratch_shapes=[pltpu.VMEM((1, LANES), F32)],
        compiler_params=_cparams(1),
        name="fox_kv",
    )(h, wk, wv, wf_pad, fb_pad)


def _fox_kernel(q_ref, g_ref, k_ref, v_ref, cum_ref, cumt_ref, o_ref):
    pair = pl.program_id(1)
    qi = pl.program_id(2)
    tq = q_ref.shape[0]
    tk = tq
    lane = lax.broadcasted_iota(I32, (tq, LANES), 1)
    q = q_ref[...]
    cum_q = cum_ref[...]
    outs = []
    for hh in range(2):
        head = pair * 2 + hh
        in_head = (lane >= hh * FOX_DH) & (lane < (hh + 1) * FOX_DH)
        q_h = jnp.where(in_head, q, jnp.zeros_like(q))
        c_q = jnp.sum(jnp.where(lane == head, cum_q, 0.0), axis=1, keepdims=True)

        def scores(j, q_h=q_h, c_q=c_q, head=head):
            ks = pl.ds(pl.multiple_of(j * tk, tk), tk)
            s = lax.dot_general(q_h, k_ref[ks, :], (((1,), (1,)), ((), ())),
                                preferred_element_type=F32)
            c_k = cumt_ref[pl.ds(head, 1), ks]
            return s + (c_q - c_k), ks

        def update(carry, s, ks):
            m, l, acc = carry
            m_new = jnp.maximum(m, jnp.max(s, axis=1, keepdims=True))
            alpha = jnp.exp(m - m_new)
            p = jnp.exp(s - m_new)
            l = alpha * l + jnp.sum(p, axis=1, keepdims=True)
            acc = alpha * acc + jnp.dot(p.astype(BF16), v_ref[ks, :], preferred_element_type=F32)
            return m_new, l, acc

        def body(j, carry):
            s, ks = scores(j)
            return update(carry, s, ks)

        init = (jnp.full((tq, 1), -jnp.inf, F32), jnp.zeros((tq, 1), F32),
                jnp.zeros((tq, LANES), F32))
        carry = lax.fori_loop(0, qi, body, init)
        s, ks = scores(qi)
        r = lax.broadcasted_iota(I32, (tq, tk), 0)
        cc = lax.broadcasted_iota(I32, (tq, tk), 1)
        s = jnp.where(cc <= r, s, -jnp.inf)
        _, l, acc = update(carry, s, ks)
        outs.append(acc / l)
    o = jnp.where(lane < FOX_DH, outs[0], outs[1])
    o_ref[...] = (o * _sigmoid(g_ref[...].astype(F32))).astype(o_ref.dtype)


def _fox_attn(qg, k, v, cum, cumt, batch, seq):
    t = batch * seq
    tq = FOX_TQ
    nq = seq // tq
    n_pairs = FOX_FD // LANES
    return pl.pallas_call(
        _fox_kernel,
        grid=(batch, n_pairs, nq),
        in_specs=[
            pl.BlockSpec((tq, LANES), lambda b, p, i: (b * nq + i, p)),
            pl.BlockSpec((tq, LANES), lambda b, p, i: (b * nq + i, n_pairs + p)),
            pl.BlockSpec((seq, LANES), lambda b, p, i: (b, p)),
            pl.BlockSpec((seq, LANES), lambda b, p, i: (b, p)),
            pl.BlockSpec((tq, LANES), lambda b, p, i: (b * nq + i, 0)),
            pl.BlockSpec((LANES, seq), lambda b, p, i: (0, b)),
        ],
        out_specs=pl.BlockSpec((tq, LANES), lambda b, p, i: (b * nq + i, p)),
        out_shape=jax.ShapeDtypeStruct((t, FOX_FD), BF16),
        compiler_params=_cparams(3),
        name="fox_attn",
    )(qg, qg, k, v, cum, cumt)


def kernel(x, gla_w_in, gla_w_gk, gla_b_gk, gla_norm_g, gla_w_out, kv_w, forget_bias, fox_w_qg,
           fox_w_out, router_w, router_bias, moe_w_gate, moe_w_up, moe_w_down, ln_g, ln_b):
    batch, seq, d = x.shape
    t = batch * seq
    h = x.reshape(t, d)
    rwt = router_w.T
    rb_col = router_bias.reshape(N_EXPERTS, 1)
    n_main = 2 * GLA_HK + 2 * GLA_HV
    k_sh = v_sh = cum = cumt = None
    for layer in range(DEPTH):
        if layer < N_A_LAYERS:
            w_in = gla_w_in[layer]
            w_main = w_in[:, :n_main].astype(BF16)
            w_gr = jnp.pad(w_in[:, n_main:], ((0, 0), (0, LANES - GLA_RANK))).astype(BF16)
            qkvg, gr = _proj(h, [w_main, w_gr], [BF16, F32])
            wgk_pad = jnp.pad(gla_w_gk[layer], ((0, LANES - GLA_RANK), (0, 0)))
            o = _gla_core(qkvg, gr, wgk_pad, gla_b_gk[layer].reshape(1, GLA_HK),
                          gla_norm_g[layer].reshape(1, GLA_DV), batch, seq)
            w_out = gla_w_out[layer].astype(BF16)
        else:
            j = layer - N_A_LAYERS
            w_qg = fox_w_qg[j]
            w_qg = jnp.concatenate([w_qg[:, :FOX_FD] * (FOX_DH ** -0.5), w_qg[:, FOX_FD:]], axis=1)
            (qg,) = _proj(h, [w_qg.astype(BF16)], [BF16])
            o = _fox_attn(qg, k_sh, v_sh, cum, cumt, batch, seq)
            w_out = fox_w_out[j].astype(BF16)
        h1, cls = _post_mixer(o, w_out, h, ln_g[layer, 0].reshape(1, d), ln_b[layer, 0].reshape(1, d),
                              rwt, rb_col)
        h = _moe_block(h1, cls, router_w, moe_w_gate[layer].astype(BF16), moe_w_up[layer].astype(BF16),
                       moe_w_down[layer].astype(BF16), ln_g[layer, 1].reshape(1, d),
                       ln_b[layer, 1].reshape(1, d))
        if layer == N_A_LAYERS - 1:
            wf_pad = jnp.pad(kv_w[:, 2 * FOX_FD:], ((0, 0), (0, LANES - FOX_HEADS))).astype(BF16)
            fb_pad = jnp.pad(forget_bias, (0, LANES - FOX_HEADS)).reshape(1, LANES)
            k_sh, v_sh, cum, cumt = _fox_kv(h, kv_w[:, :FOX_FD].astype(BF16),
                                            kv_w[:, FOX_FD:2 * FOX_FD].astype(BF16), wf_pad, fb_pad, seq)
    return h.reshape(batch, seq, d)
```

```python
import functools
import math

import jax
import jax.numpy as jnp
import numpy as np
from jax import lax
from jax.experimental import pallas as pl
from jax.experimental.pallas import tpu as pltpu

F32 = jnp.float32
BF16 = jnp.bfloat16
I32 = jnp.int32
HIGHEST = lax.Precision.HIGHEST

D_MODEL = 1024
DEPTH = 4
N_A_LAYERS = DEPTH // 2

GLA_HEADS = 4
GLA_DK = 128
GLA_DV = 256
GLA_RANK = 16
GLA_GATE_NORM = 16.0
GLA_CHUNK = 64
GLA_HK = GLA_HEADS * GLA_DK
GLA_HV = GLA_HEADS * GLA_DV

FOX_HEADS = 16
FOX_DH = 64
FOX_FD = FOX_HEADS * FOX_DH

N_EXPERTS = 16
N_GROUPS = 4
EPG = 4
D_EXPERT = 512
N_PAIRS = 6
N_CLASSES = N_GROUPS * N_PAIRS
CLASS_PAD = 32

ALPHA = float((2 * DEPTH) ** 0.25)
LN_EPS = 1e-5
RMS_EPS = 1e-6
LOG2E = 1.4426950408889634

LANES = 128
VMEM_LIMIT = 48 * 1024 * 1024

TM_PROJ = 512
TN_PROJ = 512
GLA_ROWS = 256
TM_POST = 256
TM_RANK = 512
TM_ROWS = 512
TM_MOE = 256
FOX_TQ = 512


def _cparams(n_axes):
    return pltpu.CompilerParams(dimension_semantics=("arbitrary",) * n_axes,
                                vmem_limit_bytes=VMEM_LIMIT)


def _log_sigmoid(x):
    return jnp.minimum(x, 0.0) - jnp.log(1.0 + jnp.exp(-jnp.abs(x)))


def _sigmoid(x):
    return 1.0 / (1.0 + jnp.exp(-x))


def _layer_norm(z, g, b):
    mu = jnp.mean(z, axis=-1, keepdims=True)
    zc = z - mu
    var = jnp.mean(zc * zc, axis=-1, keepdims=True)
    return zc * lax.rsqrt(var + LN_EPS) * g + b


def _proj_kernel(x_ref, *refs, n_w):
    xb = x_ref[...].astype(BF16)
    for w_ref, o_ref in zip(refs[:n_w], refs[n_w:]):
        n = o_ref.shape[1]
        tn = min(TN_PROJ, n)
        for c in range(n // tn):
            o_ref[:, c * tn:(c + 1) * tn] = jnp.dot(
                xb, w_ref[:, c * tn:(c + 1) * tn], preferred_element_type=F32).astype(o_ref.dtype)


def _proj(x, ws, out_dtypes):
    t, k = x.shape
    tm = TM_PROJ
    return pl.pallas_call(
        functools.partial(_proj_kernel, n_w=len(ws)),
        grid=(t // tm,),
        in_specs=[pl.BlockSpec((tm, k), lambda i: (i, 0))]
        + [pl.BlockSpec(w.shape, lambda i: (0, 0)) for w in ws],
        out_specs=[pl.BlockSpec((tm, w.shape[1]), lambda i: (i, 0)) for w in ws],
        out_shape=[jax.ShapeDtypeStruct((t, w.shape[1]), dt) for w, dt in zip(ws, out_dtypes)],
        compiler_params=_cparams(1),
        name="proj",
    )(x, *ws)


def _gla_kernel(q_ref, k_ref, v_ref, g_ref, gr_ref, wgk_ref, bgk_ref, ng_ref, o_ref, state_ref):
    c = GLA_CHUNK

    @pl.when(pl.program_id(1) == 0)
    def _():
        state_ref[...] = jnp.zeros_like(state_ref)

    gkz = jnp.dot(gr_ref[...], wgk_ref[...], precision=HIGHEST,
                  preferred_element_type=F32) + bgk_ref[...]
    gk = _log_sigmoid(gkz) * (1.0 / GLA_GATE_NORM)
    row = lax.broadcasted_iota(I32, (c, c), 0)
    col = lax.broadcasted_iota(I32, (c, c), 1)
    causal = col <= row
    tril = causal.astype(F32)
    scale = GLA_DK ** -0.5
    ng = ng_ref[...]

    for ci in range(GLA_ROWS // c):
        rs = slice(ci * c, (ci + 1) * c)
        bc = jnp.dot(tril, gk[rs], precision=HIGHEST, preferred_element_type=F32)
        b_last = bc[c - 1:c, :]
        qf = q_ref[rs, :].astype(F32)
        kf = k_ref[rs, :].astype(F32)
        q_dec = (qf * scale * jnp.exp(bc)).astype(BF16)
        k_inv = (kf * jnp.exp(-bc)).astype(BF16)
        k_end = (kf * jnp.exp(b_last - bc)).astype(BF16)
        dec = jnp.exp(b_last)
        for h in range(GLA_HEADS):
            ks = slice(h * GLA_DK, (h + 1) * GLA_DK)
            vs = slice(h * GLA_DV, (h + 1) * GLA_DV)
            v_h = v_ref[rs, vs]
            attn = lax.dot_general(q_dec[:, ks], k_inv[:, ks], (((1,), (1,)), ((), ())),
                                   preferred_element_type=F32)
            attn = jnp.where(causal, attn, 0.0).astype(BF16)
            st = state_ref[h]
            o = jnp.dot(attn, v_h, preferred_element_type=F32)
            o = o + lax.dot_general(q_dec[:, ks], st.astype(BF16), (((1,), (1,)), ((), ())),
                                    preferred_element_type=F32)
            kv_t = lax.dot_general(v_h, k_end[:, ks], (((0,), (0,)), ((), ())),
                                   preferred_element_type=F32)
            state_ref[h] = st * dec[:, ks] + kv_t
            o = o * lax.rsqrt(jnp.mean(o * o, axis=-1, keepdims=True) + RMS_EPS) * ng
            gate = g_ref[rs, vs].astype(F32)
            o = o * (gate * _sigmoid(gate))
            o_ref[rs, vs] = o.astype(o_ref.dtype)


def _gla_core(qkvg, gr, wgk_pad, bgk, ng, batch, seq):
    t = batch * seq
    r = GLA_ROWS
    nblk = seq // r
    rowmap = lambda b, i: b * nblk + i
    return pl.pallas_call(
        _gla_kernel,
        grid=(batch, nblk),
        in_specs=[
            pl.BlockSpec((r, GLA_HK), lambda b, i: (rowmap(b, i), 0)),
            pl.BlockSpec((r, GLA_HK), lambda b, i: (rowmap(b, i), 1)),
            pl.BlockSpec((r, GLA_HV), lambda b, i: (rowmap(b, i), 1)),
            pl.BlockSpec((r, GLA_HV), lambda b, i: (rowmap(b, i), 2)),
            pl.BlockSpec((r, LANES), lambda b, i: (rowmap(b, i), 0)),
            pl.BlockSpec((LANES, GLA_HK), lambda b, i: (0, 0)),
            pl.BlockSpec((1, GLA_HK), lambda b, i: (0, 0)),
            pl.BlockSpec((1, GLA_DV), lambda b, i: (0, 0)),
        ],
        out_specs=pl.BlockSpec((r, GLA_HV), lambda b, i: (rowmap(b, i), 0)),
        out_shape=jax.ShapeDtypeStruct((t, GLA_HV), BF16),
        scratch_shapes=[pltpu.VMEM((GLA_HEADS, GLA_DV, GLA_DK), F32)],
        compiler_params=_cparams(2),
        name="gla_core",
    )(qkvg, qkvg, qkvg, qkvg, gr, wgk_pad, bgk, ng)


def _route_class(logits_t, bias_col):
    mx = jnp.max(logits_t, axis=0, keepdims=True)
    e = jnp.exp(logits_t - mx)
    scores = e / jnp.sum(e, axis=0, keepdims=True)
    sel = scores + bias_col
    rows = [sel[j:j + 1, :] for j in range(N_EXPERTS)]
    best_g = None
    best_s = None
    for g in range(N_GROUPS):
        m = rows[g * EPG:(g + 1) * EPG]
        gs = None
        for a in range(EPG):
            for b in range(a + 1, EPG):
                s = m[a] + m[b]
                gs = s if gs is None else jnp.maximum(gs, s)
        if g == 0:
            best_g = jnp.zeros(gs.shape, I32)
            best_s = gs
        else:
            better = gs > best_s
            best_g = jnp.where(better, g, best_g)
            best_s = jnp.where(better, gs, best_s)
    mem = []
    for j in range(EPG):
        vj = rows[j]
        for g in range(1, N_GROUPS):
            vj = jnp.where(best_g == g, rows[g * EPG + j], vj)
        mem.append(vj)
    i1 = jnp.zeros(best_g.shape, I32)
    b1 = mem[0]
    for j in range(1, EPG):
        better = mem[j] > b1
        i1 = jnp.where(better, j, i1)
        b1 = jnp.where(better, mem[j], b1)
    i2 = jnp.where(i1 == 0, 1, 0).astype(I32)
    b2 = jnp.where(i1 == 0, mem[1], mem[0])
    for j in range(1, EPG):
        better = (mem[j] > b2) & (i1 != j) & (i2 != j)
        i2 = jnp.where(better, j, i2)
        b2 = jnp.where(better, mem[j], b2)
    lo = jnp.minimum(i1, i2)
    hi = jnp.maximum(i1, i2)
    pair = jnp.where(lo == 0, hi - 1, jnp.where(lo == 1, hi + 1, 5))
    return best_g * N_PAIRS + pair


def _post_kernel(o_ref, w_ref, h_ref, g_ref, b_ref, rwt_ref, rb_ref, h1_ref, cls_ref):
    mix = jnp.dot(o_ref[...], w_ref[...], preferred_element_type=F32)
    h1 = _layer_norm(ALPHA * h_ref[...] + mix, g_ref[...], b_ref[...])
    h1_ref[...] = h1
    logits_t = lax.dot_general(rwt_ref[...], h1, (((1,), (1,)), ((), ())),
                               precision=HIGHEST, preferred_element_type=F32)
    cls_ref[...] = _route_class(logits_t, rb_ref[...])


def _post_mixer(o, w_out, h, ln_g, ln_b, rwt, rb_col):
    t, d = h.shape
    tm = TM_POST
    kdim = o.shape[1]
    return pl.pallas_call(
        _post_kernel,
        grid=(t // tm,),
        in_specs=[
            pl.BlockSpec((tm, kdim), lambda i: (i, 0)),
            pl.BlockSpec((kdim, d), lambda i: (0, 0)),
            pl.BlockSpec((tm, d), lambda i: (i, 0)),
            pl.BlockSpec((1, d), lambda i: (0, 0)),
            pl.BlockSpec((1, d), lambda i: (0, 0)),
            pl.BlockSpec((N_EXPERTS, d), lambda i: (0, 0)),
            pl.BlockSpec((N_EXPERTS, 1), lambda i: (0, 0)),
        ],
        out_specs=[pl.BlockSpec((tm, d), lambda i: (i, 0)),
                   pl.BlockSpec((1, tm), lambda i: (0, i))],
        out_shape=[jax.ShapeDtypeStruct((t, d), F32), jax.ShapeDtypeStruct((1, t), I32)],
        compiler_params=_cparams(1),
        name="post_mixer",
    )(o, w_out, h, ln_g, ln_b, rwt, rb_col)


def _rank_kernel(cls_ref, pos_ref, toff_ref, cnt_ref, carry_ref, off_ref, tri_ref):
    phase = pl.program_id(0)
    i = pl.program_id(1)
    tm = cls_ref.shape[1]
    onehot = (lax.broadcasted_iota(I32, (CLASS_PAD, tm), 0) == cls_ref[...]).astype(F32)
    tile_count = jnp.sum(onehot, axis=1, keepdims=True)

    @pl.when((phase == 0) & (i == 0))
    def _():
        cnt_ref[...] = jnp.zeros_like(cnt_ref)
        r = lax.broadcasted_iota(I32, (tm, tm), 0)
        cc = lax.broadcasted_iota(I32, (tm, tm), 1)
        tri_ref[...] = (r <= cc).astype(BF16)

    @pl.when(phase == 0)
    def _():
        cnt_ref[...] += jnp.broadcast_to(tile_count, cnt_ref.shape)

    @pl.when((phase == 1) & (i == 0))
    def _():
        ntile = jnp.floor((cnt_ref[...] + (TM_MOE - 1)) * (1.0 / TM_MOE))
        r = lax.broadcasted_iota(I32, (CLASS_PAD, CLASS_PAD), 0)
        cc = lax.broadcasted_iota(I32, (CLASS_PAD, CLASS_PAD), 1)
        strict = (cc < r).astype(BF16)
        first_tile = jnp.dot(strict, ntile.astype(BF16), preferred_element_type=F32)
        toff_ref[...] = first_tile.astype(I32)
        off_ref[...] = first_tile * float(TM_MOE)
        carry_ref[...] = jnp.zeros_like(carry_ref)

    @pl.when(phase == 1)
    def _():
        prefix = jnp.dot(onehot.astype(BF16), tri_ref[...], preferred_element_type=F32)
        base = off_ref[:, 0:1] + carry_ref[:, 0:1] - 1.0
        posf = jnp.sum(onehot * (prefix + base), axis=0, keepdims=True)
        pos_ref[...] = posf.astype(I32)
        carry_ref[...] += jnp.broadcast_to(tile_count, carry_ref.shape)


def _rank(cls):
    t = cls.shape[1]
    tm = TM_RANK
    return pl.pallas_call(
        _rank_kernel,
        grid=(2, t // tm),
        in_specs=[pl.BlockSpec((1, tm), lambda p, i: (0, i))],
        out_specs=[pl.BlockSpec((1, tm), lambda p, i: (0, i * p)),
                   pl.BlockSpec((CLASS_PAD, LANES), lambda p, i: (0, 0))],
        out_shape=[jax.ShapeDtypeStruct((1, t), I32),
                   jax.ShapeDtypeStruct((CLASS_PAD, LANES), I32)],
        scratch_shapes=[pltpu.VMEM((CLASS_PAD, LANES), F32),
                        pltpu.VMEM((CLASS_PAD, LANES), F32),
                        pltpu.VMEM((CLASS_PAD, LANES), F32),
                        pltpu.VMEM((tm, tm), BF16)],
        compiler_params=_cparams(2),
        name="rank",
    )(cls)


def _dispatch_kernel(pos_ref, h_ref, xs_in_ref, xs_ref, sem):
    del xs_in_ref
    tm = h_ref.shape[0]
    base = pl.program_id(0) * tm

    def issue(r, carry):
        p = pos_ref[base + r]
        pltpu.make_async_copy(h_ref.at[pl.ds(r, 1), :], xs_ref.at[pl.ds(p, 1), :], sem).start()
        return carry

    lax.fori_loop(0, tm, issue, 0)
    pltpu.make_async_copy(h_ref, xs_ref.at[pl.ds(0, tm), :], sem).wait()


def _dispatch(pos, h1, n_sorted):
    t, d = h1.shape
    tm = TM_ROWS
    xs0 = jnp.zeros((n_sorted, d), F32)
    return pl.pallas_call(
        _dispatch_kernel,
        grid_spec=pltpu.PrefetchScalarGridSpec(
            num_scalar_prefetch=1,
            grid=(t // tm,),
            in_specs=[pl.BlockSpec((tm, d), lambda i, pos: (i, 0)),
                      pl.BlockSpec(memory_space=pl.ANY)],
            out_specs=pl.BlockSpec(memory_space=pl.ANY),
            scratch_shapes=[pltpu.SemaphoreType.DMA(())],
        ),
        out_shape=jax.ShapeDtypeStruct((n_sorted, d), F32),
        input_output_aliases={2: 0},
        compiler_params=_cparams(1),
        name="dispatch",
    )(pos, h1, xs0)


def _combine_kernel(pos_ref, h_ref, ys_ref, g_ref, b_ref, o_ref, buf_ref, sem):
    tm = h_ref.shape[0]
    base = pl.program_id(0) * tm

    def issue(r, carry):
        p = pos_ref[base + r]
        pltpu.make_async_copy(ys_ref.at[pl.ds(p, 1), :], buf_ref.at[pl.ds(r, 1), :], sem).start()
        return carry

    lax.fori_loop(0, tm, issue, 0)
    pltpu.make_async_copy(ys_ref.at[pl.ds(0, tm), :], buf_ref, sem).wait()
    o_ref[...] = _layer_norm(ALPHA * h_ref[...] + buf_ref[...], g_ref[...], b_ref[...])


def _combine(pos, h1, ys, ln_g, ln_b):
    t, d = h1.shape
    tm = TM_ROWS
    return pl.pallas_call(
        _combine_kernel,
        grid_spec=pltpu.PrefetchScalarGridSpec(
            num_scalar_prefetch=1,
            grid=(t // tm,),
            in_specs=[pl.BlockSpec((tm, d), lambda i, pos: (i, 0)),
                      pl.BlockSpec(memory_space=pl.ANY),
                      pl.BlockSpec((1, d), lambda i, pos: (0, 0)),
                      pl.BlockSpec((1, d), lambda i, pos: (0, 0))],
            out_specs=pl.BlockSpec((tm, d), lambda i, pos: (i, 0)),
            scratch_shapes=[pltpu.VMEM((tm, d), F32), pltpu.SemaphoreType.DMA(())],
        ),
        out_shape=jax.ShapeDtypeStruct((t, d), F32),
        compiler_params=_cparams(1),
        name="combine",
    )(pos, h1, ys, ln_g, ln_b)


def _moe_kernel(ea_ref, eb_ref, valid_ref, xs_ref, rw_ref, wga_ref, wua_ref, wda_ref,
                wgb_ref, wub_ref, wdb_ref, ys_ref):
    i = pl.program_id(0)

    @pl.when(valid_ref[i] == 0)
    def _():
        ys_ref[...] = jnp.zeros_like(ys_ref)

    @pl.when(valid_ref[i] != 0)
    def _():
        x = xs_ref[...]
        logits = jnp.dot(x, rw_ref[...], precision=HIGHEST, preferred_element_type=F32)
        e = jnp.exp(logits - jnp.max(logits, axis=1, keepdims=True))
        scores = e / jnp.sum(e, axis=1, keepdims=True)
        lane = lax.broadcasted_iota(I32, scores.shape, 1)
        sa = jnp.sum(jnp.where(lane == ea_ref[i], scores, 0.0), axis=1, keepdims=True)
        sb = jnp.sum(jnp.where(lane == eb_ref[i], scores, 0.0), axis=1, keepdims=True)
        tot = sa + sb
        xb = x.astype(BF16)

        def expert(wg_ref, wu_ref, wd_ref, gate):
            a = jnp.dot(xb, wg_ref[...], preferred_element_type=F32)
            u = jnp.dot(xb, wu_ref[...], preferred_element_type=F32)
            hid = (a * _sigmoid(a)) * u * gate
            return jnp.dot(hid.astype(BF16), wd_ref[...], preferred_element_type=F32)

        y = expert(wga_ref, wua_ref, wda_ref, sa / tot)
        y = y + expert(wgb_ref, wub_ref, wdb_ref, sb / tot)
        ys_ref[...] = y


def _moe(ea, eb, valid, xs, rw, wg, wu, wd):
    n_sorted, d = xs.shape
    tm = TM_MOE
    f = wg.shape[2]
    up_a = pl.BlockSpec((None, d, f), lambda i, ea, eb, va: (ea[i], 0, 0))
    up_b = pl.BlockSpec((None, d, f), lambda i, ea, eb, va: (eb[i], 0, 0))
    dn_a = pl.BlockSpec((None, f, d), lambda i, ea, eb, va: (ea[i], 0, 0))
    dn_b = pl.BlockSpec((None, f, d), lambda i, ea, eb, va: (eb[i], 0, 0))
    return pl.pallas_call(
        _moe_kernel,
        grid_spec=pltpu.PrefetchScalarGridSpec(
            num_scalar_prefetch=3,
            grid=(n_sorted // tm,),
            in_specs=[pl.BlockSpec((tm, d), lambda i, ea, eb, va: (i, 0)),
                      pl.BlockSpec((d, N_EXPERTS), lambda i, ea, eb, va: (0, 0)),
                      up_a, up_a, dn_a, up_b, up_b, dn_b],
            out_specs=pl.BlockSpec((tm, d), lambda i, ea, eb, va: (i, 0)),
        ),
        out_shape=jax.ShapeDtypeStruct((n_sorted, d), F32),
        compiler_params=_cparams(1),
        name="moe",
    )(ea, eb, valid, xs, rw, wg, wu, wd, wg, wu, wd)


_PAIR_LO = (0, 0, 0, 1, 1, 2)
_PAIR_HI = (1, 2, 3, 2, 3, 3)


def _tile_plan(toff, n_tiles):
    first = toff[:N_CLASSES, 0]
    tiles = jnp.arange(n_tiles, dtype=I32)
    cls = jnp.sum((first[None, :] <= tiles[:, None]).astype(I32), axis=1) - 1
    group = cls // N_PAIRS
    pair = cls % N_PAIRS
    lo = jnp.asarray(_PAIR_LO, I32)[pair]
    hi = jnp.asarray(_PAIR_HI, I32)[pair]
    total = toff[N_CLASSES, 0]
    valid = (tiles < total).astype(I32)
    return group * EPG + lo, group * EPG + hi, valid


def _moe_block(h1, cls, rw, wg, wu, wd, ln_g, ln_b):
    t, _ = h1.shape
    n_tiles = t // TM_MOE + N_CLASSES
    pos2d, toff = _rank(cls)
    pos = pos2d.reshape(t)
    ea, eb, valid = _tile_plan(toff, n_tiles)
    xs = _dispatch(pos, h1, n_tiles * TM_MOE)
    ys = _moe(ea, eb, valid, xs, rw, wg, wu, wd)
    return _combine(pos, h1, ys, ln_g, ln_b)


DEC_LANES = 6


def _decay_placement():
    pq = np.zeros((LANES, FOX_FD), np.float32)
    pk = np.zeros((LANES, FOX_FD), np.float32)
    oq = np.zeros((1, FOX_FD), np.float32)
    ok = np.zeros((1, FOX_FD), np.float32)
    for h in range(FOX_HEADS):
        base = (h // 2) * LANES + (h % 2) * DEC_LANES
        for part in range(3):
            pq[part * FOX_HEADS + h, base + part] = 1.0
            pk[part * FOX_HEADS + h, base + 3 + part] = -1.0
            oq[0, base + 3 + part] = 1.0
            ok[0, base + part] = 1.0
    return pq, pk, oq, ok


def _split3(x):
    hi = x.astype(BF16).astype(F32)
    r = x - hi
    mid = r.astype(BF16).astype(F32)
    lo = (r - mid).astype(BF16).astype(F32)
    return hi, mid, lo


def _kv_kernel(x_ref, wk_ref, wv_ref, wf_ref, fb_ref, pq_ref, pk_ref, oq_ref, ok_ref,
               k_ref, v_ref, cq_ref, ck_ref, carry_ref, tril_ref, *, tiles_per_seq):
    i = pl.program_id(0)
    tm = x_ref.shape[0]

    @pl.when(i == 0)
    def _():
        r = lax.broadcasted_iota(I32, (tm, tm), 0)
        cc = lax.broadcasted_iota(I32, (tm, tm), 1)
        tril_ref[...] = (cc <= r).astype(BF16)

    @pl.when(i % tiles_per_seq == 0)
    def _():
        carry_ref[...] = jnp.zeros_like(carry_ref)

    xb = x_ref[...].astype(BF16)
    for w_ref, o_ref in ((wk_ref, k_ref), (wv_ref, v_ref)):
        n = o_ref.shape[1]
        for c in range(n // TN_PROJ):
            cs = slice(c * TN_PROJ, (c + 1) * TN_PROJ)
            o_ref[:, cs] = jnp.dot(xb, w_ref[:, cs], preferred_element_type=F32).astype(o_ref.dtype)
    fl = jnp.dot(xb, wf_ref[...], preferred_element_type=F32) + fb_ref[...]
    lane = lax.broadcasted_iota(I32, fl.shape, 1)
    log_f = jnp.where(lane < FOX_HEADS, _log_sigmoid(fl), 0.0)
    tril = tril_ref[...]
    cum = carry_ref[...]
    for part in _split3(log_f):
        cum = cum + jnp.dot(tril, part.astype(BF16), preferred_element_type=F32)
    carry_ref[...] = cum[tm - 1:tm, :]
    hi, mid, lo = _split3(cum * LOG2E)
    packed = hi + pltpu.roll(mid, FOX_HEADS, axis=1) + pltpu.roll(lo, 2 * FOX_HEADS, axis=1)
    packed = packed.astype(BF16)
    cq_ref[...] = (jnp.dot(packed, pq_ref[...], preferred_element_type=F32)
                   + oq_ref[...]).astype(cq_ref.dtype)
    ck_ref[...] = (jnp.dot(packed, pk_ref[...], preferred_element_type=F32)
                   + ok_ref[...]).astype(ck_ref.dtype)


def _fox_kv(h, wk, wv, wf_pad, fb_pad, seq):
    t, d = h.shape
    tm = TM_PROJ
    pq, pk, oq, ok = _decay_placement()
    full = lambda a: pl.BlockSpec(a.shape, lambda i: (0, 0))
    consts = [jnp.asarray(pq, BF16), jnp.asarray(pk, BF16), jnp.asarray(oq), jnp.asarray(ok)]
    row_out = pl.BlockSpec((tm, FOX_FD), lambda i: (i, 0))
    return pl.pallas_call(
        functools.partial(_kv_kernel, tiles_per_seq=seq // tm),
        grid=(t // tm,),
        in_specs=[pl.BlockSpec((tm, d), lambda i: (i, 0)), full(wk), full(wv), full(wf_pad),
                  full(fb_pad)] + [full(c) for c in consts],
        out_specs=[row_out, row_out, row_out, row_out],
        out_shape=[jax.ShapeDtypeStruct((t, FOX_FD), BF16)] * 4,
        scratch_shapes=[pltpu.VMEM((1, LANES), F32), pltpu.VMEM((tm, tm), BF16)],
        compiler_params=_cparams(1),
        name="fox_kv",
    )(h, wk, wv, wf_pad, fb_pad, *consts)


def _fox_kernel(q_ref, g_ref, cq_ref, k_ref, ck_ref, v_ref, o_ref,
                s_ref, mb_ref, m_ref, acc_ref):
    qi = pl.program_id(2)
    tq = q_ref.shape[0]
    tk = tq
    lane = lax.broadcasted_iota(I32, (tq, LANES), 1)
    klane = lax.broadcasted_iota(I32, (tk, LANES), 1)
    q = q_ref[...]
    cq = cq_ref[...]
    q_augs = []
    for hh in range(2):
        in_head = (lane >= hh * FOX_DH) & (lane < (hh + 1) * FOX_DH)
        in_dec = (lane >= hh * DEC_LANES) & (lane < (hh + 1) * DEC_LANES)
        q_augs.append(jnp.concatenate([jnp.where(in_head, q, jnp.zeros_like(q)),
                                       jnp.where(in_dec, cq, jnp.zeros_like(cq))], axis=1))
    one_lane = (FOX_DH, 0)
    v_keep = [klane < FOX_DH, klane >= FOX_DH]

    def scores(j):
        ks = pl.ds(pl.multiple_of(j * tk, tk), tk)
        k_aug = jnp.concatenate([k_ref[ks, :], ck_ref[ks, :]], axis=1)
        for hh in range(2):
            s = lax.dot_general(q_augs[hh], k_aug, (((1,), (1,)), ((), ())),
                                preferred_element_type=F32)
            s_ref[hh] = s
            mb_ref[hh] = jnp.max(s, axis=1, keepdims=True)

    def absorb(j, masked):
        ks = pl.ds(pl.multiple_of(j * tk, tk), tk)
        v = v_ref[ks, :]
        for hh in range(2):
            ones = (klane == one_lane[hh]).astype(BF16)
            v_aug = jnp.where(v_keep[hh], v, ones)
            if masked:
                r = lax.broadcasted_iota(I32, (tq, tk), 0)
                cc = lax.broadcasted_iota(I32, (tq, tk), 1)
                s = jnp.where(cc <= r, s_ref[hh], -jnp.inf)
                mb = jnp.max(s, axis=1, keepdims=True)
            else:
                s = s_ref[hh]
                mb = mb_ref[hh]
            m_old = m_ref[hh]
            m_new = jnp.maximum(m_old, mb)
            m_ref[hh] = m_new
            alpha = jnp.exp2(m_old - m_new)
            p = jnp.exp2(s - m_new).astype(BF16)
            acc_ref[hh] = alpha * acc_ref[hh] + jnp.dot(p, v_aug, preferred_element_type=F32)

    m_ref[...] = jnp.full(m_ref.shape, -jnp.inf, F32)
    acc_ref[...] = jnp.zeros(acc_ref.shape, F32)
    scores(0)

    def body(j, carry):
        absorb(j, False)
        scores(j + 1)
        return carry

    lax.fori_loop(0, qi, body, 0)
    absorb(qi, True)
    acc0 = acc_ref[0]
    acc1 = acc_ref[1]
    o = jnp.where(lane < FOX_DH, acc0 / acc0[:, FOX_DH:FOX_DH + 1], acc1 / acc1[:, 0:1])
    o_ref[...] = (o * _sigmoid(g_ref[...].astype(F32))).astype(o_ref.dtype)


def _fox_attn(qg, k, v, cq, ck, batch, seq):
    t = batch * seq
    tq = FOX_TQ
    nq = seq // tq
    n_pairs = FOX_FD // LANES
    q_blk = lambda b, p, i: (b * nq + i, p)
    kv_blk = lambda b, p, i: (b, p)
    return pl.pallas_call(
        _fox_kernel,
        grid=(batch, n_pairs, nq),
        in_specs=[
            pl.BlockSpec((tq, LANES), q_blk),
            pl.BlockSpec((tq, LANES), lambda b, p, i: (b * nq + i, n_pairs + p)),
            pl.BlockSpec((tq, LANES), q_blk),
            pl.BlockSpec((seq, LANES), kv_blk),
            pl.BlockSpec((seq, LANES), kv_blk),
            pl.BlockSpec((seq, LANES), kv_blk),
        ],
        out_specs=pl.BlockSpec((tq, LANES), q_blk),
        out_shape=jax.ShapeDtypeStruct((t, FOX_FD), BF16),
        scratch_shapes=[pltpu.VMEM((2, tq, tq), F32),
                        pltpu.VMEM((2, tq, 1), F32),
                        pltpu.VMEM((2, tq, 1), F32),
                        pltpu.VMEM((2, tq, LANES), F32)],
        compiler_params=_cparams(3),
        name="fox_attn",
    )(qg, qg, cq, k, ck, v)


def kernel(x, gla_w_in, gla_w_gk, gla_b_gk, gla_norm_g, gla_w_out, kv_w, forget_bias, fox_w_qg,
           fox_w_out, router_w, router_bias, moe_w_gate, moe_w_up, moe_w_down, ln_g, ln_b):
    batch, seq, d = x.shape
    t = batch * seq
    h = x.reshape(t, d)
    rwt = router_w.T
    rb_col = router_bias.reshape(N_EXPERTS, 1)
    n_main = 2 * GLA_HK + 2 * GLA_HV
    k_sh = v_sh = cq_sh = ck_sh = None
    for layer in range(DEPTH):
        if layer < N_A_LAYERS:
            w_in = gla_w_in[layer]
            w_main = w_in[:, :n_main].astype(BF16)
            w_gr = jnp.pad(w_in[:, n_main:], ((0, 0), (0, LANES - GLA_RANK))).astype(BF16)
            qkvg, gr = _proj(h, [w_main, w_gr], [BF16, F32])
            wgk_pad = jnp.pad(gla_w_gk[layer], ((0, LANES - GLA_RANK), (0, 0)))
            o = _gla_core(qkvg, gr, wgk_pad, gla_b_gk[layer].reshape(1, GLA_HK),
                          gla_norm_g[layer].reshape(1, GLA_DV), batch, seq)
            w_out = gla_w_out[layer].astype(BF16)
        else:
            j = layer - N_A_LAYERS
            w_qg = fox_w_qg[j]
            w_qg = jnp.concatenate([w_qg[:, :FOX_FD] * (FOX_DH ** -0.5 * LOG2E), w_qg[:, FOX_FD:]], axis=1)
            (qg,) = _proj(h, [w_qg.astype(BF16)], [BF16])
            o = _fox_attn(qg, k_sh, v_sh, cq_sh, ck_sh, batch, seq)
            w_out = fox_w_out[j].astype(BF16)
        h1, cls = _post_mixer(o, w_out, h, ln_g[layer, 0].reshape(1, d), ln_b[layer, 0].reshape(1, d),
                              rwt, rb_col)
        h = _moe_block(h1, cls, router_w, moe_w_gate[layer].astype(BF16), moe_w_up[layer].astype(BF16),
                       moe_w_down[layer].astype(BF16), ln_g[layer, 1].reshape(1, d),
                       ln_b[layer, 1].reshape(1, d))
        if layer == N_A_LAYERS - 1:
            wf_pad = jnp.pad(kv_w[:, 2 * FOX_FD:], ((0, 0), (0, LANES - FOX_HEADS))).astype(BF16)
            fb_pad = jnp.pad(forget_bias, (0, LANES - FOX_HEADS)).reshape(1, LANES)
            k_sh, v_sh, cq_sh, ck_sh = _fox_kv(h, kv_w[:, :FOX_FD].astype(BF16),
                                            kv_w[:, FOX_FD:2 * FOX_FD].astype(BF16), wf_pad, fb_pad, seq)
    return h.reshape(batch, seq, d)
```

```python
import functools
import math

import jax
import jax.numpy as jnp
import numpy as np
from jax import lax
from jax.experimental import pallas as pl
from jax.experimental.pallas import tpu as pltpu

F32 = jnp.float32
BF16 = jnp.bfloat16
I32 = jnp.int32

D_MODEL = 1024
DEPTH = 4
N_A_LAYERS = DEPTH // 2

GLA_HEADS = 4
GLA_DK = 128
GLA_DV = 256
GLA_RANK = 16
GLA_GATE_NORM = 16.0
GLA_CHUNK = 64
GLA_HK = GLA_HEADS * GLA_DK
GLA_HV = GLA_HEADS * GLA_DV

FOX_HEADS = 16
FOX_DH = 64
FOX_FD = FOX_HEADS * FOX_DH

N_EXPERTS = 16
N_GROUPS = 4
EPG = 4
D_EXPERT = 512
N_PAIRS = 6
N_CLASSES = N_GROUPS * N_PAIRS
CLASS_PAD = 32

ALPHA = float((2 * DEPTH) ** 0.25)
LN_EPS = 1e-5
RMS_EPS = 1e-6
LOG2E = 1.4426950408889634

LANES = 128
VMEM_LIMIT = 48 * 1024 * 1024

TM_PROJ = 512
TN_PROJ = 512
GLA_ROWS = 256
TM_POST = 512
TM_RANK = 512
TM_ROWS = 512
TM_MOE = 256
ROW_DMA_UNROLL = 8
FOX_TQ = 512


def _cparams(n_axes):
    return pltpu.CompilerParams(dimension_semantics=("arbitrary",) * n_axes,
                                vmem_limit_bytes=VMEM_LIMIT)


def _log_sigmoid(x):
    return jnp.minimum(x, 0.0) - jnp.log(1.0 + jnp.exp(-jnp.abs(x)))


def _sigmoid(x):
    return 1.0 / (1.0 + jnp.exp(-x))


def _layer_norm(z, g, b):
    mu = jnp.mean(z, axis=-1, keepdims=True)
    zc = z - mu
    var = jnp.mean(zc * zc, axis=-1, keepdims=True)
    return zc * lax.rsqrt(var + LN_EPS) * g + b


def _split3(x):
    hi = x.astype(BF16).astype(F32)
    r = x - hi
    mid = r.astype(BF16).astype(F32)
    lo = (r - mid).astype(BF16).astype(F32)
    return hi, mid, lo


def _proj_kernel(x_ref, *refs, n_w):
    xb = x_ref[...].astype(BF16)
    for w_ref, o_ref in zip(refs[:n_w], refs[n_w:]):
        n = o_ref.shape[1]
        tn = min(TN_PROJ, n)
        for c in range(n // tn):
            o_ref[:, c * tn:(c + 1) * tn] = jnp.dot(
                xb, w_ref[:, c * tn:(c + 1) * tn], preferred_element_type=F32).astype(o_ref.dtype)


def _proj(x, ws, out_dtypes):
    t, k = x.shape
    tm = TM_PROJ
    return pl.pallas_call(
        functools.partial(_proj_kernel, n_w=len(ws)),
        grid=(t // tm,),
        in_specs=[pl.BlockSpec((tm, k), lambda i: (i, 0))]
        + [pl.BlockSpec(w.shape, lambda i: (0, 0)) for w in ws],
        out_specs=[pl.BlockSpec((tm, w.shape[1]), lambda i: (i, 0)) for w in ws],
        out_shape=[jax.ShapeDtypeStruct((t, w.shape[1]), dt) for w, dt in zip(ws, out_dtypes)],
        compiler_params=_cparams(1),
        name="proj",
    )(x, *ws)


def _gla_kernel(q_ref, k_ref, v_ref, g_ref, gr_ref, wgk_ref, bgk_ref, ng_ref, o_ref, state_ref):
    c = GLA_CHUNK

    @pl.when(pl.program_id(1) == 0)
    def _():
        state_ref[...] = jnp.zeros_like(state_ref)

    gr = gr_ref[...]
    gr_hi = gr.astype(BF16)
    gr_lo = (gr - gr_hi.astype(F32)).astype(BF16)
    w_hi = wgk_ref[0]
    gkz = (jnp.dot(gr_hi, w_hi, preferred_element_type=F32)
           + jnp.dot(gr_lo, w_hi, preferred_element_type=F32)
           + jnp.dot(gr_hi, wgk_ref[1], preferred_element_type=F32)) + bgk_ref[...]
    gk = _log_sigmoid(gkz) * (1.0 / GLA_GATE_NORM)
    row = lax.broadcasted_iota(I32, (c, c), 0)
    col = lax.broadcasted_iota(I32, (c, c), 1)
    causal = col <= row
    tril = jnp.where(causal, 1.0, 0.0).astype(BF16)
    gk_parts = [part.astype(BF16) for part in _split3(gk)]
    scale = GLA_DK ** -0.5
    ng = ng_ref[...]

    for ci in range(GLA_ROWS // c):
        rs = slice(ci * c, (ci + 1) * c)
        bc = sum(jnp.dot(tril, part[rs], preferred_element_type=F32) for part in gk_parts)
        b_last = bc[c - 1:c, :]
        qf = q_ref[rs, :].astype(F32)
        kf = k_ref[rs, :].astype(F32)
        q_dec = (qf * scale * jnp.exp(bc)).astype(BF16)
        k_inv = (kf * jnp.exp(-bc)).astype(BF16)
        k_end = (kf * jnp.exp(b_last - bc)).astype(BF16)
        dec = jnp.exp(b_last)
        for h in range(GLA_HEADS):
            ks = slice(h * GLA_DK, (h + 1) * GLA_DK)
            vs = slice(h * GLA_DV, (h + 1) * GLA_DV)
            v_h = v_ref[rs, vs]
            attn = lax.dot_general(q_dec[:, ks], k_inv[:, ks], (((1,), (1,)), ((), ())),
                                   preferred_element_type=F32)
            attn = jnp.where(causal, attn, 0.0).astype(BF16)
            st = state_ref[h]
            o = jnp.dot(attn, v_h, preferred_element_type=F32)
            o = o + lax.dot_general(q_dec[:, ks], st.astype(BF16), (((1,), (1,)), ((), ())),
                                    preferred_element_type=F32)
            kv_t = lax.dot_general(v_h, k_end[:, ks], (((0,), (0,)), ((), ())),
                                   preferred_element_type=F32)
            state_ref[h] = st * dec[:, ks] + kv_t
            o = o * lax.rsqrt(jnp.mean(o * o, axis=-1, keepdims=True) + RMS_EPS) * ng
            gate = g_ref[rs, vs].astype(F32)
            o = o * (gate * _sigmoid(gate))
            o_ref[rs, vs] = o.astype(o_ref.dtype)


def _gla_core(qkvg, gr, wgk_pad, bgk, ng, batch, seq):
    t = batch * seq
    r = GLA_ROWS
    nblk = seq // r
    rowmap = lambda b, i: b * nblk + i
    return pl.pallas_call(
        _gla_kernel,
        grid=(batch, nblk),
        in_specs=[
            pl.BlockSpec((r, GLA_HK), lambda b, i: (rowmap(b, i), 0)),
            pl.BlockSpec((r, GLA_HK), lambda b, i: (rowmap(b, i), 1)),
            pl.BlockSpec((r, GLA_HV), lambda b, i: (rowmap(b, i), 1)),
            pl.BlockSpec((r, GLA_HV), lambda b, i: (rowmap(b, i), 2)),
            pl.BlockSpec((r, LANES), lambda b, i: (rowmap(b, i), 0)),
            pl.BlockSpec((2, LANES, GLA_HK), lambda b, i: (0, 0, 0)),
            pl.BlockSpec((1, GLA_HK), lambda b, i: (0, 0)),
            pl.BlockSpec((1, GLA_DV), lambda b, i: (0, 0)),
        ],
        out_specs=pl.BlockSpec((r, GLA_HV), lambda b, i: (rowmap(b, i), 0)),
        out_shape=jax.ShapeDtypeStruct((t, GLA_HV), BF16),
        scratch_shapes=[pltpu.VMEM((GLA_HEADS, GLA_DV, GLA_DK), F32)],
        compiler_params=_cparams(2),
        name="gla_core",
    )(qkvg, qkvg, qkvg, qkvg, gr, wgk_pad, bgk, ng)


def _route_class(logits_t, bias_col):
    mx = jnp.max(logits_t, axis=0, keepdims=True)
    e = jnp.exp(logits_t - mx)
    scores = e / jnp.sum(e, axis=0, keepdims=True)
    sel = scores + bias_col
    rows = [sel[j:j + 1, :] for j in range(N_EXPERTS)]
    best_g = None
    best_s = None
    for g in range(N_GROUPS):
        m = rows[g * EPG:(g + 1) * EPG]
        gs = None
        for a in range(EPG):
            for b in range(a + 1, EPG):
                s = m[a] + m[b]
                gs = s if gs is None else jnp.maximum(gs, s)
        if g == 0:
            best_g = jnp.zeros(gs.shape, I32)
            best_s = gs
        else:
            better = gs > best_s
            best_g = jnp.where(better, g, best_g)
            best_s = jnp.where(better, gs, best_s)
    mem = []
    for j in range(EPG):
        vj = rows[j]
        for g in range(1, N_GROUPS):
            vj = jnp.where(best_g == g, rows[g * EPG + j], vj)
        mem.append(vj)
    i1 = jnp.zeros(best_g.shape, I32)
    b1 = mem[0]
    for j in range(1, EPG):
        better = mem[j] > b1
        i1 = jnp.where(better, j, i1)
        b1 = jnp.where(better, mem[j], b1)
    i2 = jnp.where(i1 == 0, 1, 0).astype(I32)
    b2 = jnp.where(i1 == 0, mem[1], mem[0])
    for j in range(1, EPG):
        better = (mem[j] > b2) & (i1 != j) & (i2 != j)
        i2 = jnp.where(better, j, i2)
        b2 = jnp.where(better, mem[j], b2)
    lo = jnp.minimum(i1, i2)
    hi = jnp.maximum(i1, i2)
    pair = jnp.where(lo == 0, hi - 1, jnp.where(lo == 1, hi + 1, 5))
    return best_g * N_PAIRS + pair


def _router_weight(router_w):
    hi = router_w.astype(BF16)
    lo = (router_w - hi.astype(F32)).astype(BF16)
    pad = jnp.zeros((router_w.shape[0], LANES - 2 * N_EXPERTS), BF16)
    return jnp.concatenate([hi, lo, pad], axis=1)


def _router_logits(x, w2_ref):
    hi = x.astype(BF16)
    lo = (x - hi.astype(F32)).astype(BF16)
    w2 = w2_ref[...]
    a = jnp.dot(hi, w2, preferred_element_type=F32) + jnp.dot(lo, w2, preferred_element_type=F32)
    return a + pltpu.roll(a, LANES - N_EXPERTS, axis=1)


def _post_kernel(o_ref, w_ref, h_ref, g_ref, b_ref, w2_ref, rb_ref, h1_ref, cls_ref):
    mix = jnp.dot(o_ref[...], w_ref[...], preferred_element_type=F32)
    h1 = _layer_norm(ALPHA * h_ref[...] + mix, g_ref[...], b_ref[...])
    h1_ref[...] = h1
    logits_t = _router_logits(h1, w2_ref).T[:N_EXPERTS, :]
    cls_ref[...] = _route_class(logits_t, rb_ref[...])


def _post_mixer(o, w_out, h, ln_g, ln_b, w2, rb_col):
    t, d = h.shape
    tm = TM_POST
    kdim = o.shape[1]
    return pl.pallas_call(
        _post_kernel,
        grid=(t // tm,),
        in_specs=[
            pl.BlockSpec((tm, kdim), lambda i: (i, 0)),
            pl.BlockSpec((kdim, d), lambda i: (0, 0)),
            pl.BlockSpec((tm, d), lambda i: (i, 0)),
            pl.BlockSpec((1, d), lambda i: (0, 0)),
            pl.BlockSpec((1, d), lambda i: (0, 0)),
            pl.BlockSpec((d, LANES), lambda i: (0, 0)),
            pl.BlockSpec((N_EXPERTS, 1), lambda i: (0, 0)),
        ],
        out_specs=[pl.BlockSpec((tm, d), lambda i: (i, 0)),
                   pl.BlockSpec((1, tm), lambda i: (0, i))],
        out_shape=[jax.ShapeDtypeStruct((t, d), F32), jax.ShapeDtypeStruct((1, t), I32)],
        compiler_params=_cparams(1),
        name="post_mixer",
    )(o, w_out, h, ln_g, ln_b, w2, rb_col)


def _rank_kernel(cls_ref, pos_ref, toff_ref, cnt_ref, carry_ref, off_ref, tri_ref):
    phase = pl.program_id(0)
    i = pl.program_id(1)
    tm = cls_ref.shape[1]
    onehot = (lax.broadcasted_iota(I32, (CLASS_PAD, tm), 0) == cls_ref[...]).astype(F32)
    tile_count = jnp.sum(onehot, axis=1, keepdims=True)

    @pl.when((phase == 0) & (i == 0))
    def _():
        cnt_ref[...] = jnp.zeros_like(cnt_ref)
        r = lax.broadcasted_iota(I32, (tm, tm), 0)
        cc = lax.broadcasted_iota(I32, (tm, tm), 1)
        tri_ref[...] = (r <= cc).astype(BF16)

    @pl.when(phase == 0)
    def _():
        cnt_ref[...] += jnp.broadcast_to(tile_count, cnt_ref.shape)

    @pl.when((phase == 1) & (i == 0))
    def _():
        ntile = jnp.floor((cnt_ref[...] + (TM_MOE - 1)) * (1.0 / TM_MOE))
        r = lax.broadcasted_iota(I32, (CLASS_PAD, CLASS_PAD), 0)
        cc = lax.broadcasted_iota(I32, (CLASS_PAD, CLASS_PAD), 1)
        strict = (cc < r).astype(BF16)
        first_tile = jnp.dot(strict, ntile.astype(BF16), preferred_element_type=F32)
        toff_ref[...] = first_tile.astype(I32)
        off_ref[...] = first_tile * float(TM_MOE)
        carry_ref[...] = jnp.zeros_like(carry_ref)

    @pl.when(phase == 1)
    def _():
        prefix = jnp.dot(onehot.astype(BF16), tri_ref[...], preferred_element_type=F32)
        base = off_ref[:, 0:1] + carry_ref[:, 0:1] - 1.0
        posf = jnp.sum(onehot * (prefix + base), axis=0, keepdims=True)
        pos_ref[...] = posf.astype(I32)
        carry_ref[...] += jnp.broadcast_to(tile_count, carry_ref.shape)


def _rank(cls):
    t = cls.shape[1]
    tm = TM_RANK
    return pl.pallas_call(
        _rank_kernel,
        grid=(2, t // tm),
        in_specs=[pl.BlockSpec((1, tm), lambda p, i: (0, i))],
        out_specs=[pl.BlockSpec((1, tm), lambda p, i: (0, i * p)),
                   pl.BlockSpec((CLASS_PAD, LANES), lambda p, i: (0, 0))],
        out_shape=[jax.ShapeDtypeStruct((1, t), I32),
                   jax.ShapeDtypeStruct((CLASS_PAD, LANES), I32)],
        scratch_shapes=[pltpu.VMEM((CLASS_PAD, LANES), F32),
                        pltpu.VMEM((CLASS_PAD, LANES), F32),
                        pltpu.VMEM((CLASS_PAD, LANES), F32),
                        pltpu.VMEM((tm, tm), BF16)],
        compiler_params=_cparams(2),
        name="rank",
    )(cls)


def _dispatch_kernel(pos_ref, h_ref, xs_in_ref, xs_ref, sem):
    del xs_in_ref
    tm = h_ref.shape[0]
    base = pl.program_id(0) * tm

    def issue(r, carry):
        p = pos_ref[base + r]
        pltpu.make_async_copy(h_ref.at[pl.ds(r, 1), :], xs_ref.at[pl.ds(p, 1), :], sem).start()
        return carry

    lax.fori_loop(0, tm, issue, 0, unroll=ROW_DMA_UNROLL)
    pltpu.make_async_copy(h_ref, xs_ref.at[pl.ds(0, tm), :], sem).wait()


def _dispatch(pos, h1, n_sorted):
    t, d = h1.shape
    tm = TM_ROWS
    xs0 = jnp.zeros((n_sorted, d), F32)
    return pl.pallas_call(
        _dispatch_kernel,
        grid_spec=pltpu.PrefetchScalarGridSpec(
            num_scalar_prefetch=1,
            grid=(t // tm,),
            in_specs=[pl.BlockSpec((tm, d), lambda i, pos: (i, 0)),
                      pl.BlockSpec(memory_space=pl.ANY)],
            out_specs=pl.BlockSpec(memory_space=pl.ANY),
            scratch_shapes=[pltpu.SemaphoreType.DMA(())],
        ),
        out_shape=jax.ShapeDtypeStruct((n_sorted, d), F32),
        input_output_aliases={2: 0},
        compiler_params=_cparams(1),
        name="dispatch",
    )(pos, h1, xs0)


def _combine_kernel(pos_ref, h_ref, ys_ref, g_ref, b_ref, o_ref, buf_ref, sem):
    tm = h_ref.shape[0]
    base = pl.program_id(0) * tm

    def issue(r, carry):
        p = pos_ref[base + r]
        pltpu.make_async_copy(ys_ref.at[pl.ds(p, 1), :], buf_ref.at[pl.ds(r, 1), :], sem).start()
        return carry

    lax.fori_loop(0, tm, issue, 0, unroll=ROW_DMA_UNROLL)
    pltpu.make_async_copy(ys_ref.at[pl.ds(0, tm), :], buf_ref, sem).wait()
    o_ref[...] = _layer_norm(ALPHA * h_ref[...] + buf_ref[...], g_ref[...], b_ref[...])


def _combine(pos, h1, ys, ln_g, ln_b):
    t, d = h1.shape
    tm = TM_ROWS
    return pl.pallas_call(
        _combine_kernel,
        grid_spec=pltpu.PrefetchScalarGridSpec(
            num_scalar_prefetch=1,
            grid=(t // tm,),
            in_specs=[pl.BlockSpec((tm, d), lambda i, pos: (i, 0)),
                      pl.BlockSpec(memory_space=pl.ANY),
                      pl.BlockSpec((1, d), lambda i, pos: (0, 0)),
                      pl.BlockSpec((1, d), lambda i, pos: (0, 0))],
            out_specs=pl.BlockSpec((tm, d), lambda i, pos: (i, 0)),
            scratch_shapes=[pltpu.VMEM((tm, d), F32), pltpu.SemaphoreType.DMA(())],
        ),
        out_shape=jax.ShapeDtypeStruct((t, d), F32),
        compiler_params=_cparams(1),
        name="combine",
    )(pos, h1, ys, ln_g, ln_b)


def _moe_kernel(ea_ref, eb_ref, valid_ref, xs_ref, w2_ref, wga_ref, wua_ref, wda_ref,
                wgb_ref, wub_ref, wdb_ref, ys_ref, wbf_up_ref, wbf_dn_ref):
    i = pl.program_id(0)
    prev = jnp.maximum(i - 1, 0)
    fresh = (i == 0) | (ea_ref[i] != ea_ref[prev]) | (eb_ref[i] != eb_ref[prev])

    @pl.when(fresh)
    def _():
        for slot, w_ref in enumerate((wga_ref, wua_ref, wgb_ref, wub_ref)):
            wbf_up_ref[slot] = w_ref[...].astype(BF16)
        for slot, w_ref in enumerate((wda_ref, wdb_ref)):
            wbf_dn_ref[slot] = w_ref[...].astype(BF16)

    @pl.when(valid_ref[i] == 0)
    def _():
        ys_ref[...] = jnp.zeros_like(ys_ref)

    @pl.when(valid_ref[i] != 0)
    def _():
        x = xs_ref[...]
        logits = _router_logits(x, w2_ref)
        lane = lax.broadcasted_iota(I32, logits.shape, 1)
        logits = jnp.where(lane < N_EXPERTS, logits, -jnp.inf)
        e = jnp.exp(logits - jnp.max(logits, axis=1, keepdims=True))
        scores = e / jnp.sum(e, axis=1, keepdims=True)
        sa = jnp.sum(jnp.where(lane == ea_ref[i], scores, 0.0), axis=1, keepdims=True)
        sb = jnp.sum(jnp.where(lane == eb_ref[i], scores, 0.0), axis=1, keepdims=True)
        tot = sa + sb
        xb = x.astype(BF16)

        def expert(slot, gate):
            a = jnp.dot(xb, wbf_up_ref[2 * slot], preferred_element_type=F32)
            u = jnp.dot(xb, wbf_up_ref[2 * slot + 1], preferred_element_type=F32)
            hid = (a * _sigmoid(a)) * u * gate
            return jnp.dot(hid.astype(BF16), wbf_dn_ref[slot], preferred_element_type=F32)

        ys_ref[...] = expert(0, sa / tot) + expert(1, sb / tot)


def _moe(ea, eb, valid, xs, w2, wg, wu, wd):
    n_sorted, d = xs.shape
    tm = TM_MOE
    f = wg.shape[2]
    up_a = pl.BlockSpec((None, d, f), lambda i, ea, eb, va: (ea[i], 0, 0))
    up_b = pl.BlockSpec((None, d, f), lambda i, ea, eb, va: (eb[i], 0, 0))
    dn_a = pl.BlockSpec((None, f, d), lambda i, ea, eb, va: (ea[i], 0, 0))
    dn_b = pl.BlockSpec((None, f, d), lambda i, ea, eb, va: (eb[i], 0, 0))
    return pl.pallas_call(
        _moe_kernel,
        grid_spec=pltpu.PrefetchScalarGridSpec(
            num_scalar_prefetch=3,
            grid=(n_sorted // tm,),
            in_specs=[pl.BlockSpec((tm, d), lambda i, ea, eb, va: (i, 0)),
                      pl.BlockSpec((d, LANES), lambda i, ea, eb, va: (0, 0)),
                      up_a, up_a, dn_a, up_b, up_b, dn_b],
            out_specs=pl.BlockSpec((tm, d), lambda i, ea, eb, va: (i, 0)),
            scratch_shapes=[pltpu.VMEM((4, d, f), BF16), pltpu.VMEM((2, f, d), BF16)],
        ),
        out_shape=jax.ShapeDtypeStruct((n_sorted, d), F32),
        compiler_params=_cparams(1),
        name="moe",
    )(ea, eb, valid, xs, w2, wg, wu, wd, wg, wu, wd)


_PAIR_LO = (0, 0, 0, 1, 1, 2)
_PAIR_HI = (1, 2, 3, 2, 3, 3)


def _tile_plan(toff, n_tiles):
    first = toff[:N_CLASSES, 0]
    tiles = jnp.arange(n_tiles, dtype=I32)
    cls = jnp.sum((first[None, :] <= tiles[:, None]).astype(I32), axis=1) - 1
    group = cls // N_PAIRS
    pair = cls % N_PAIRS
    lo = jnp.asarray(_PAIR_LO, I32)[pair]
    hi = jnp.asarray(_PAIR_HI, I32)[pair]
    total = toff[N_CLASSES, 0]
    valid = (tiles < total).astype(I32)
    return group * EPG + lo, group * EPG + hi, valid


def _moe_block(h1, cls, w2, wg, wu, wd, ln_g, ln_b):
    t, _ = h1.shape
    n_tiles = t // TM_MOE + N_CLASSES
    pos2d, toff = _rank(cls)
    pos = pos2d.reshape(t)
    ea, eb, valid = _tile_plan(toff, n_tiles)
    xs = _dispatch(pos, h1, n_tiles * TM_MOE)
    ys = _moe(ea, eb, valid, xs, w2, wg, wu, wd)
    return _combine(pos, h1, ys, ln_g, ln_b)


DEC_LANES = 6


def _decay_placement():
    pq = np.zeros((LANES, FOX_FD), np.float32)
    pk = np.zeros((LANES, FOX_FD), np.float32)
    oq = np.zeros((1, FOX_FD), np.float32)
    ok = np.zeros((1, FOX_FD), np.float32)
    for h in range(FOX_HEADS):
        base = (h // 2) * LANES + (h % 2) * DEC_LANES
        for part in range(3):
            pq[part * FOX_HEADS + h, base + part] = 1.0
            pk[part * FOX_HEADS + h, base + 3 + part] = -1.0
            oq[0, base + 3 + part] = 1.0
            ok[0, base + part] = 1.0
    return pq, pk, oq, ok


def _kv_kernel(x_ref, wk_ref, wv_ref, wf_ref, fb_ref, pq_ref, pk_ref, oq_ref, ok_ref,
               k_ref, v_ref, cq_ref, ck_ref, carry_ref, tril_ref, *, tiles_per_seq):
    i = pl.program_id(0)
    tm = x_ref.shape[0]

    @pl.when(i == 0)
    def _():
        r = lax.broadcasted_iota(I32, (tm, tm), 0)
        cc = lax.broadcasted_iota(I32, (tm, tm), 1)
        tril_ref[...] = (cc <= r).astype(BF16)

    @pl.when(i % tiles_per_seq == 0)
    def _():
        carry_ref[...] = jnp.zeros_like(carry_ref)

    xb = x_ref[...].astype(BF16)
    for w_ref, o_ref in ((wk_ref, k_ref), (wv_ref, v_ref)):
        n = o_ref.shape[1]
        for c in range(n // TN_PROJ):
            cs = slice(c * TN_PROJ, (c + 1) * TN_PROJ)
            o_ref[:, cs] = jnp.dot(xb, w_ref[:, cs], preferred_element_type=F32).astype(o_ref.dtype)
    fl = jnp.dot(xb, wf_ref[...], preferred_element_type=F32) + fb_ref[...]
    lane = lax.broadcasted_iota(I32, fl.shape, 1)
    log_f = jnp.where(lane < FOX_HEADS, _log_sigmoid(fl), 0.0)
    tril = tril_ref[...]
    cum = carry_ref[...]
    for part in _split3(log_f):
        cum = cum + jnp.dot(tril, part.astype(BF16), preferred_element_type=F32)
    carry_ref[...] = cum[tm - 1:tm, :]
    hi, mid, lo = _split3(cum * LOG2E)
    packed = hi + pltpu.roll(mid, FOX_HEADS, axis=1) + pltpu.roll(lo, 2 * FOX_HEADS, axis=1)
    packed = packed.astype(BF16)
    cq_ref[...] = (jnp.dot(packed, pq_ref[...], preferred_element_type=F32)
                   + oq_ref[...]).astype(cq_ref.dtype)
    ck_ref[...] = (jnp.dot(packed, pk_ref[...], preferred_element_type=F32)
                   + ok_ref[...]).astype(ck_ref.dtype)


def _fox_kv(h, wk, wv, wf_pad, fb_pad, seq):
    t, d = h.shape
    tm = TM_PROJ
    pq, pk, oq, ok = _decay_placement()
    full = lambda a: pl.BlockSpec(a.shape, lambda i: (0, 0))
    consts = [jnp.asarray(pq, BF16), jnp.asarray(pk, BF16), jnp.asarray(oq), jnp.asarray(ok)]
    row_out = pl.BlockSpec((tm, FOX_FD), lambda i: (i, 0))
    return pl.pallas_call(
        functools.partial(_kv_kernel, tiles_per_seq=seq // tm),
        grid=(t // tm,),
        in_specs=[pl.BlockSpec((tm, d), lambda i: (i, 0)), full(wk), full(wv), full(wf_pad),
                  full(fb_pad)] + [full(c) for c in consts],
        out_specs=[row_out, row_out, row_out, row_out],
        out_shape=[jax.ShapeDtypeStruct((t, FOX_FD), BF16)] * 4,
        scratch_shapes=[pltpu.VMEM((1, LANES), F32), pltpu.VMEM((tm, tm), BF16)],
        compiler_params=_cparams(1),
        name="fox_kv",
    )(h, wk, wv, wf_pad, fb_pad, *consts)


def _fox_kernel(q_ref, g_ref, cq_ref, k_ref, ck_ref, v_ref, o_ref,
                s_ref, mb_ref, m_ref, acc_ref):
    qi = pl.program_id(2)
    tq = q_ref.shape[0]
    tk = tq
    lane = lax.broadcasted_iota(I32, (tq, LANES), 1)
    klane = lax.broadcasted_iota(I32, (tk, LANES), 1)
    q = q_ref[...]
    cq = cq_ref[...]
    q_augs = []
    for hh in range(2):
        in_head = (lane >= hh * FOX_DH) & (lane < (hh + 1) * FOX_DH)
        in_dec = (lane >= hh * DEC_LANES) & (lane < (hh + 1) * DEC_LANES)
        q_augs.append(jnp.concatenate([jnp.where(in_head, q, jnp.zeros_like(q)),
                                       jnp.where(in_dec, cq, jnp.zeros_like(cq))], axis=1))
    one_lane = (FOX_DH, 0)
    v_keep = [klane < FOX_DH, klane >= FOX_DH]

    def scores(j):
        ks = pl.ds(pl.multiple_of(j * tk, tk), tk)
        k_aug = jnp.concatenate([k_ref[ks, :], ck_ref[ks, :]], axis=1)
        for hh in range(2):
            s = lax.dot_general(q_augs[hh], k_aug, (((1,), (1,)), ((), ())),
                                preferred_element_type=F32)
            s_ref[hh] = s
            mb_ref[hh] = jnp.broadcast_to(jnp.max(s, axis=1, keepdims=True), (tq, LANES))

    def absorb(j, masked):
        ks = pl.ds(pl.multiple_of(j * tk, tk), tk)
        v = v_ref[ks, :]
        for hh in range(2):
            ones = (klane == one_lane[hh]).astype(BF16)
            v_aug = jnp.where(v_keep[hh], v, ones)
            if masked:
                r = lax.broadcasted_iota(I32, (tq, tk), 0)
                cc = lax.broadcasted_iota(I32, (tq, tk), 1)
                s = jnp.where(cc <= r, s_ref[hh], -jnp.inf)
                mb = jnp.broadcast_to(jnp.max(s, axis=1, keepdims=True), (tq, LANES))
            else:
                s = s_ref[hh]
                mb = mb_ref[hh]
            m_old = m_ref[hh]
            m_new = jnp.maximum(m_old, mb)
            m_ref[hh] = m_new
            alpha = jnp.exp2(m_old - m_new)
            p = jnp.exp2((s - jnp.concatenate([m_new] * (tk // LANES), axis=1)).astype(BF16))
            acc_ref[hh] = alpha * acc_ref[hh] + jnp.dot(p, v_aug, preferred_element_type=F32)

    m_ref[...] = jnp.full(m_ref.shape, -jnp.inf, F32)
    acc_ref[...] = jnp.zeros(acc_ref.shape, F32)
    scores(0)

    def body(i, carry):
        for j in (2 * i, 2 * i + 1):
            absorb(j, False)
            scores(j + 1)
        return carry

    lax.fori_loop(0, qi // 2, body, 0)

    @pl.when(qi % 2 == 1)
    def _():
        absorb(qi - 1, False)
        scores(qi)

    absorb(qi, True)
    acc0 = acc_ref[0]
    acc1 = acc_ref[1]
    o = jnp.where(lane < FOX_DH, acc0 / acc0[:, FOX_DH:FOX_DH + 1], acc1 / acc1[:, 0:1])
    o_ref[...] = (o * _sigmoid(g_ref[...].astype(F32))).astype(o_ref.dtype)


def _fox_attn(qg, k, v, cq, ck, batch, seq):
    t = batch * seq
    tq = FOX_TQ
    nq = seq // tq
    n_pairs = FOX_FD // LANES
    q_blk = lambda b, p, i: (b * nq + i, p)
    kv_blk = lambda b, p, i: (b, p)
    return pl.pallas_call(
        _fox_kernel,
        grid=(batch, n_pairs, nq),
        in_specs=[
            pl.BlockSpec((tq, LANES), q_blk),
            pl.BlockSpec((tq, LANES), lambda b, p, i: (b * nq + i, n_pairs + p)),
            pl.BlockSpec((tq, LANES), q_blk),
            pl.BlockSpec((seq, LANES), kv_blk),
            pl.BlockSpec((seq, LANES), kv_blk),
            pl.BlockSpec((seq, LANES), kv_blk),
        ],
        out_specs=pl.BlockSpec((tq, LANES), q_blk),
        out_shape=jax.ShapeDtypeStruct((t, FOX_FD), BF16),
        scratch_shapes=[pltpu.VMEM((2, tq, tq), F32),
                        pltpu.VMEM((2, tq, LANES), F32),
                        pltpu.VMEM((2, tq, LANES), F32),
                        pltpu.VMEM((2, tq, LANES), F32)],
        compiler_params=_cparams(3),
        name="fox_attn",
    )(qg, qg, cq, k, ck, v)


def kernel(x, gla_w_in, gla_w_gk, gla_b_gk, gla_norm_g, gla_w_out, kv_w, forget_bias, fox_w_qg,
           fox_w_out, router_w, router_bias, moe_w_gate, moe_w_up, moe_w_down, ln_g, ln_b):
    batch, seq, d = x.shape
    t = batch * seq
    h = x.reshape(t, d)
    w2 = _router_weight(router_w)
    rb_col = router_bias.reshape(N_EXPERTS, 1)
    n_main = 2 * GLA_HK + 2 * GLA_HV
    k_sh = v_sh = cq_sh = ck_sh = None
    for layer in range(DEPTH):
        if layer < N_A_LAYERS:
            w_in = gla_w_in[layer]
            w_main = w_in[:, :n_main].astype(BF16)
            w_gr = jnp.pad(w_in[:, n_main:], ((0, 0), (0, LANES - GLA_RANK))).astype(BF16)
            qkvg, gr = _proj(h, [w_main, w_gr], [BF16, F32])
            wgk = jnp.pad(gla_w_gk[layer], ((0, LANES - GLA_RANK), (0, 0)))
            wgk_hi = wgk.astype(BF16)
            wgk_pad = jnp.stack([wgk_hi, (wgk - wgk_hi.astype(F32)).astype(BF16)])
            o = _gla_core(qkvg, gr, wgk_pad, gla_b_gk[layer].reshape(1, GLA_HK),
                          gla_norm_g[layer].reshape(1, GLA_DV), batch, seq)
            w_out = gla_w_out[layer].astype(BF16)
        else:
            j = layer - N_A_LAYERS
            w_qg = fox_w_qg[j]
            w_qg = jnp.concatenate([w_qg[:, :FOX_FD] * (FOX_DH ** -0.5 * LOG2E), w_qg[:, FOX_FD:]], axis=1)
            (qg,) = _proj(h, [w_qg.astype(BF16)], [BF16])
            o = _fox_attn(qg, k_sh, v_sh, cq_sh, ck_sh, batch, seq)
            w_out = fox_w_out[j].astype(BF16)
        h1, cls = _post_mixer(o, w_out, h, ln_g[layer, 0].reshape(1, d), ln_b[layer, 0].reshape(1, d),
                              w2, rb_col)
        h = _moe_block(h1, cls, w2, moe_w_gate[layer], moe_w_up[layer], moe_w_down[layer],
                       ln_g[layer, 1].reshape(1, d), ln_b[layer, 1].reshape(1, d))
        if layer == N_A_LAYERS - 1:
            wf_pad = jnp.pad(kv_w[:, 2 * FOX_FD:], ((0, 0), (0, LANES - FOX_HEADS))).astype(BF16)
            fb_pad = jnp.pad(forget_bias, (0, LANES - FOX_HEADS)).reshape(1, LANES)
            k_sh, v_sh, cq_sh, ck_sh = _fox_kv(h, kv_w[:, :FOX_FD].astype(BF16),
                                            kv_w[:, FOX_FD:2 * FOX_FD].astype(BF16), wf_pad, fb_pad, seq)
    return h.reshape(batch, seq, d)
```

```python
import functools
import math

import jax
import jax.numpy as jnp
import numpy as np
from jax import lax
from jax.experimental import pallas as pl
from jax.experimental.pallas import tpu as pltpu

F32 = jnp.float32
BF16 = jnp.bfloat16
I32 = jnp.int32

D_MODEL = 1024
DEPTH = 4
N_A_LAYERS = DEPTH // 2

GLA_HEADS = 4
GLA_DK = 128
GLA_DV = 256
GLA_RANK = 16
GLA_GATE_NORM = 16.0
GLA_CHUNK = 64
GLA_HK = GLA_HEADS * GLA_DK
GLA_HV = GLA_HEADS * GLA_DV

FOX_HEADS = 16
FOX_DH = 64
FOX_FD = FOX_HEADS * FOX_DH

N_EXPERTS = 16
N_GROUPS = 4
EPG = 4
D_EXPERT = 512
N_PAIRS = 6
N_CLASSES = N_GROUPS * N_PAIRS
CLASS_PAD = 32

ALPHA = float((2 * DEPTH) ** 0.25)
LN_EPS = 1e-5
RMS_EPS = 1e-6
LOG2E = 1.4426950408889634

LANES = 128
VMEM_LIMIT = 48 * 1024 * 1024

TM_PROJ = 512
TN_PROJ = 512
GLA_ROWS = 256
TM_POST = 512
TM_RANK = 512
TM_ROWS = 512
TM_MOE = 256
FOX_TQ = 512


def _cparams(n_axes):
    return pltpu.CompilerParams(dimension_semantics=("arbitrary",) * n_axes,
                                vmem_limit_bytes=VMEM_LIMIT)


def _log_sigmoid(x):
    return jnp.minimum(x, 0.0) - jnp.log(1.0 + jnp.exp(-jnp.abs(x)))


def _sigmoid(x):
    return 1.0 / (1.0 + jnp.exp(-x))


def _layer_norm(z, g, b):
    mu = jnp.mean(z, axis=-1, keepdims=True)
    zc = z - mu
    var = jnp.mean(zc * zc, axis=-1, keepdims=True)
    return zc * lax.rsqrt(var + LN_EPS) * g + b


def _split3(x):
    hi = x.astype(BF16).astype(F32)
    r = x - hi
    mid = r.astype(BF16).astype(F32)
    lo = (r - mid).astype(BF16).astype(F32)
    return hi, mid, lo


def _proj_kernel(x_ref, *refs, n_w):
    xb = x_ref[...].astype(BF16)
    for w_ref, o_ref in zip(refs[:n_w], refs[n_w:]):
        n = o_ref.shape[1]
        tn = min(TN_PROJ, n)
        for c in range(n // tn):
            o_ref[:, c * tn:(c + 1) * tn] = jnp.dot(
                xb, w_ref[:, c * tn:(c + 1) * tn], preferred_element_type=F32).astype(o_ref.dtype)


def _proj(x, ws, out_dtypes):
    t, k = x.shape
    tm = TM_PROJ
    return pl.pallas_call(
        functools.partial(_proj_kernel, n_w=len(ws)),
        grid=(t // tm,),
        in_specs=[pl.BlockSpec((tm, k), lambda i: (i, 0))]
        + [pl.BlockSpec(w.shape, lambda i: (0, 0)) for w in ws],
        out_specs=[pl.BlockSpec((tm, w.shape[1]), lambda i: (i, 0)) for w in ws],
        out_shape=[jax.ShapeDtypeStruct((t, w.shape[1]), dt) for w, dt in zip(ws, out_dtypes)],
        compiler_params=_cparams(1),
        name="proj",
    )(x, *ws)


def _gla_kernel(q_ref, k_ref, v_ref, g_ref, gr_ref, wgk_ref, bgk_ref, ng_ref, o_ref, state_ref):
    c = GLA_CHUNK

    @pl.when(pl.program_id(1) == 0)
    def _():
        state_ref[...] = jnp.zeros_like(state_ref)

    gr = gr_ref[...]
    gr_hi = gr.astype(BF16)
    gr_lo = (gr - gr_hi.astype(F32)).astype(BF16)
    w_hi = wgk_ref[0]
    gkz = (jnp.dot(gr_hi, w_hi, preferred_element_type=F32)
           + jnp.dot(gr_lo, w_hi, preferred_element_type=F32)
           + jnp.dot(gr_hi, wgk_ref[1], preferred_element_type=F32)) + bgk_ref[...]
    gk = _log_sigmoid(gkz) * (1.0 / GLA_GATE_NORM)
    row = lax.broadcasted_iota(I32, (c, c), 0)
    col = lax.broadcasted_iota(I32, (c, c), 1)
    causal = col <= row
    tril = jnp.where(causal, 1.0, 0.0).astype(BF16)
    gk_parts = [part.astype(BF16) for part in _split3(gk)]
    scale = GLA_DK ** -0.5
    ng = ng_ref[...]

    for ci in range(GLA_ROWS // c):
        rs = slice(ci * c, (ci + 1) * c)
        bc = sum(jnp.dot(tril, part[rs], preferred_element_type=F32) for part in gk_parts)
        b_last = bc[c - 1:c, :]
        qf = q_ref[rs, :].astype(F32)
        kf = k_ref[rs, :].astype(F32)
        q_dec = (qf * scale * jnp.exp(bc)).astype(BF16)
        k_inv = (kf * jnp.exp(-bc)).astype(BF16)
        k_end = (kf * jnp.exp(b_last - bc)).astype(BF16)
        dec = jnp.exp(b_last)
        for h in range(GLA_HEADS):
            ks = slice(h * GLA_DK, (h + 1) * GLA_DK)
            vs = slice(h * GLA_DV, (h + 1) * GLA_DV)
            v_h = v_ref[rs, vs]
            attn = lax.dot_general(q_dec[:, ks], k_inv[:, ks], (((1,), (1,)), ((), ())),
                                   preferred_element_type=F32)
            attn = jnp.where(causal, attn, 0.0).astype(BF16)
            st = state_ref[h]
            o = jnp.dot(attn, v_h, preferred_element_type=F32)
            o = o + lax.dot_general(q_dec[:, ks], st.astype(BF16), (((1,), (1,)), ((), ())),
                                    preferred_element_type=F32)
            kv_t = lax.dot_general(v_h, k_end[:, ks], (((0,), (0,)), ((), ())),
                                   preferred_element_type=F32)
            state_ref[h] = st * dec[:, ks] + kv_t
            o = o * lax.rsqrt(jnp.mean(o * o, axis=-1, keepdims=True) + RMS_EPS) * ng
            gate = g_ref[rs, vs].astype(F32)
            o = o * (gate * _sigmoid(gate))
            o_ref[rs, vs] = o.astype(o_ref.dtype)


def _gla_core(qkvg, gr, wgk_pad, bgk, ng, batch, seq):
    t = batch * seq
    r = GLA_ROWS
    nblk = seq // r
    rowmap = lambda b, i: b * nblk + i
    return pl.pallas_call(
        _gla_kernel,
        grid=(batch, nblk),
        in_specs=[
            pl.BlockSpec((r, GLA_HK), lambda b, i: (rowmap(b, i), 0)),
            pl.BlockSpec((r, GLA_HK), lambda b, i: (rowmap(b, i), 1)),
            pl.BlockSpec((r, GLA_HV), lambda b, i: (rowmap(b, i), 1)),
            pl.BlockSpec((r, GLA_HV), lambda b, i: (rowmap(b, i), 2)),
            pl.BlockSpec((r, LANES), lambda b, i: (rowmap(b, i), 0)),
            pl.BlockSpec((2, LANES, GLA_HK), lambda b, i: (0, 0, 0)),
            pl.BlockSpec((1, GLA_HK), lambda b, i: (0, 0)),
            pl.BlockSpec((1, GLA_DV), lambda b, i: (0, 0)),
        ],
        out_specs=pl.BlockSpec((r, GLA_HV), lambda b, i: (rowmap(b, i), 0)),
        out_shape=jax.ShapeDtypeStruct((t, GLA_HV), BF16),
        scratch_shapes=[pltpu.VMEM((GLA_HEADS, GLA_DV, GLA_DK), F32)],
        compiler_params=_cparams(2),
        name="gla_core",
    )(qkvg, qkvg, qkvg, qkvg, gr, wgk_pad, bgk, ng)


def _route_class(logits_t, bias_col):
    mx = jnp.max(logits_t, axis=0, keepdims=True)
    e = jnp.exp(logits_t - mx)
    scores = e / jnp.sum(e, axis=0, keepdims=True)
    sel = scores + bias_col
    rows = [sel[j:j + 1, :] for j in range(N_EXPERTS)]
    best_g = None
    best_s = None
    for g in range(N_GROUPS):
        m = rows[g * EPG:(g + 1) * EPG]
        gs = None
        for a in range(EPG):
            for b in range(a + 1, EPG):
                s = m[a] + m[b]
                gs = s if gs is None else jnp.maximum(gs, s)
        if g == 0:
            best_g = jnp.zeros(gs.shape, I32)
            best_s = gs
        else:
            better = gs > best_s
            best_g = jnp.where(better, g, best_g)
            best_s = jnp.where(better, gs, best_s)
    mem = []
    for j in range(EPG):
        vj = rows[j]
        for g in range(1, N_GROUPS):
            vj = jnp.where(best_g == g, rows[g * EPG + j], vj)
        mem.append(vj)
    i1 = jnp.zeros(best_g.shape, I32)
    b1 = mem[0]
    for j in range(1, EPG):
        better = mem[j] > b1
        i1 = jnp.where(better, j, i1)
        b1 = jnp.where(better, mem[j], b1)
    i2 = jnp.where(i1 == 0, 1, 0).astype(I32)
    b2 = jnp.where(i1 == 0, mem[1], mem[0])
    for j in range(1, EPG):
        better = (mem[j] > b2) & (i1 != j) & (i2 != j)
        i2 = jnp.where(better, j, i2)
        b2 = jnp.where(better, mem[j], b2)
    lo = jnp.minimum(i1, i2)
    hi = jnp.maximum(i1, i2)
    pair = jnp.where(lo == 0, hi - 1, jnp.where(lo == 1, hi + 1, 5))
    return best_g * N_PAIRS + pair


def _router_weight(router_w):
    hi = router_w.astype(BF16)
    lo = (router_w - hi.astype(F32)).astype(BF16)
    pad = jnp.zeros((router_w.shape[0], LANES - 2 * N_EXPERTS), BF16)
    return jnp.concatenate([hi, lo, pad], axis=1)


def _router_logits(x, w2_ref):
    hi = x.astype(BF16)
    lo = (x - hi.astype(F32)).astype(BF16)
    w2 = w2_ref[...]
    a = jnp.dot(hi, w2, preferred_element_type=F32) + jnp.dot(lo, w2, preferred_element_type=F32)
    return a + pltpu.roll(a, LANES - N_EXPERTS, axis=1)


def _post_kernel(o_ref, w_ref, h_ref, g_ref, b_ref, w2_ref, rb_ref, h1_ref, cls_ref):
    mix = jnp.dot(o_ref[...], w_ref[...], preferred_element_type=F32)
    h1 = _layer_norm(ALPHA * h_ref[...] + mix, g_ref[...], b_ref[...])
    h1_ref[...] = h1
    logits_t = _router_logits(h1, w2_ref).T[:N_EXPERTS, :]
    cls_ref[...] = _route_class(logits_t, rb_ref[...])


def _post_mixer(o, w_out, h, ln_g, ln_b, w2, rb_col):
    t, d = h.shape
    tm = TM_POST
    kdim = o.shape[1]
    return pl.pallas_call(
        _post_kernel,
        grid=(t // tm,),
        in_specs=[
            pl.BlockSpec((tm, kdim), lambda i: (i, 0)),
            pl.BlockSpec((kdim, d), lambda i: (0, 0)),
            pl.BlockSpec((tm, d), lambda i: (i, 0)),
            pl.BlockSpec((1, d), lambda i: (0, 0)),
            pl.BlockSpec((1, d), lambda i: (0, 0)),
            pl.BlockSpec((d, LANES), lambda i: (0, 0)),
            pl.BlockSpec((N_EXPERTS, 1), lambda i: (0, 0)),
        ],
        out_specs=[pl.BlockSpec((tm, d), lambda i: (i, 0)),
                   pl.BlockSpec((1, tm), lambda i: (0, i))],
        out_shape=[jax.ShapeDtypeStruct((t, d), F32), jax.ShapeDtypeStruct((1, t), I32)],
        compiler_params=_cparams(1),
        name="post_mixer",
    )(o, w_out, h, ln_g, ln_b, w2, rb_col)


def _rank_kernel(cls_ref, pos_ref, toff_ref, cnt_out_ref, cnt_ref, carry_ref, off_ref, tri_ref):
    phase = pl.program_id(0)
    i = pl.program_id(1)
    tm = cls_ref.shape[1]
    onehot = (lax.broadcasted_iota(I32, (CLASS_PAD, tm), 0) == cls_ref[...]).astype(F32)
    tile_count = jnp.sum(onehot, axis=1, keepdims=True)

    @pl.when((phase == 0) & (i == 0))
    def _():
        cnt_ref[...] = jnp.zeros_like(cnt_ref)
        r = lax.broadcasted_iota(I32, (tm, tm), 0)
        cc = lax.broadcasted_iota(I32, (tm, tm), 1)
        tri_ref[...] = (r <= cc).astype(BF16)

    @pl.when(phase == 0)
    def _():
        cnt_ref[...] += jnp.broadcast_to(tile_count, cnt_ref.shape)

    @pl.when((phase == 1) & (i == 0))
    def _():
        ntile = jnp.floor((cnt_ref[...] + (TM_MOE - 1)) * (1.0 / TM_MOE))
        r = lax.broadcasted_iota(I32, (CLASS_PAD, CLASS_PAD), 0)
        cc = lax.broadcasted_iota(I32, (CLASS_PAD, CLASS_PAD), 1)
        strict = (cc < r).astype(BF16)
        first_tile = jnp.dot(strict, ntile.astype(BF16), preferred_element_type=F32)
        toff_ref[...] = first_tile.astype(I32)
        cnt_out_ref[...] = cnt_ref[...].astype(I32)
        off_ref[...] = first_tile * float(TM_MOE)
        carry_ref[...] = jnp.zeros_like(carry_ref)

    @pl.when(phase == 1)
    def _():
        prefix = jnp.dot(onehot.astype(BF16), tri_ref[...], preferred_element_type=F32)
        base = off_ref[:, 0:1] + carry_ref[:, 0:1] - 1.0
        posf = jnp.sum(onehot * (prefix + base), axis=0, keepdims=True)
        pos_ref[...] = posf.astype(I32)
        carry_ref[...] += jnp.broadcast_to(tile_count, carry_ref.shape)


def _rank(cls):
    t = cls.shape[1]
    tm = TM_RANK
    return pl.pallas_call(
        _rank_kernel,
        grid=(2, t // tm),
        in_specs=[pl.BlockSpec((1, tm), lambda p, i: (0, i))],
        out_specs=[pl.BlockSpec((1, tm), lambda p, i: (0, i * p)),
                   pl.BlockSpec((CLASS_PAD, LANES), lambda p, i: (0, 0)),
                   pl.BlockSpec((CLASS_PAD, LANES), lambda p, i: (0, 0))],
        out_shape=[jax.ShapeDtypeStruct((1, t), I32),
                   jax.ShapeDtypeStruct((CLASS_PAD, LANES), I32),
                   jax.ShapeDtypeStruct((CLASS_PAD, LANES), I32)],
        scratch_shapes=[pltpu.VMEM((CLASS_PAD, LANES), F32),
                        pltpu.VMEM((CLASS_PAD, LANES), F32),
                        pltpu.VMEM((CLASS_PAD, LANES), F32),
                        pltpu.VMEM((tm, tm), BF16)],
        compiler_params=_cparams(2),
        name="rank",
    )(cls)


def _slot_map_kernel(pos_ref, cnt_ref, first_ref, src_ref, dst_ref, *, n_tiles):
    t = pos_ref.shape[0]
    tm = TM_MOE

    def lead(r, carry):
        dst_ref[r] = t + r
        return carry

    lax.fori_loop(0, tm, lead, 0, unroll=8)

    def pad(p, carry):
        src_ref[p] = 0
        dst_ref[tm + p] = t + (p & (tm - 1))
        return carry

    for c in range(N_CLASSES):
        lax.fori_loop(first_ref[c] * tm + cnt_ref[c], first_ref[c + 1] * tm, pad, 0)
    lax.fori_loop(first_ref[N_CLASSES] * tm, n_tiles * tm, pad, 0)

    def last(r, carry):
        src_ref[n_tiles * tm + r] = 0
        return carry

    lax.fori_loop(0, tm, last, 0, unroll=8)

    def token(tok, carry):
        p = pos_ref[tok]
        src_ref[p] = tok
        dst_ref[tm + p] = tok
        return carry

    lax.fori_loop(0, t, token, 0, unroll=8)


def _slot_map(pos, cnt, first, n_tiles):
    n = (n_tiles + 1) * TM_MOE
    smem = pl.BlockSpec(memory_space=pltpu.SMEM)
    return pl.pallas_call(
        functools.partial(_slot_map_kernel, n_tiles=n_tiles),
        grid_spec=pltpu.PrefetchScalarGridSpec(
            num_scalar_prefetch=3, grid=(1,), in_specs=[], out_specs=[smem, smem]),
        out_shape=[jax.ShapeDtypeStruct((n,), I32), jax.ShapeDtypeStruct((n,), I32)],
        compiler_params=_cparams(1),
        name="slot_map",
    )(pos, cnt, first)


def _moe_kernel(src_ref, dst_ref, ea_ref, eb_ref, nt_ref, h1_ref, w2_ref, wga_ref, wua_ref, wda_ref,
                wgb_ref, wub_ref, wdb_ref, ytok_ref, gbuf, xcur, ycur, sbuf, wbf_up_ref, wbf_dn_ref,
                gsem, ssem):
    i = pl.program_id(0)
    total = nt_ref[0]
    tm = TM_MOE

    def gather_start(tile):
        base = tile * tm
        for r in range(tm):
            pltpu.make_async_copy(h1_ref.at[pl.ds(src_ref[base + r], 1), :],
                                  gbuf.at[pl.ds(r, 1), :], gsem).start()

    def gather_wait():
        pltpu.make_async_copy(h1_ref.at[pl.ds(0, tm), :], gbuf, gsem).wait()

    def scatter_start(tile):
        base = (tile + 1) * tm
        for r in range(tm):
            pltpu.make_async_copy(sbuf.at[pl.ds(r, 1), :],
                                  ytok_ref.at[pl.ds(dst_ref[base + r], 1), :], ssem).start()

    def scatter_wait():
        pltpu.make_async_copy(sbuf, ytok_ref.at[pl.ds(0, tm), :], ssem).wait()

    @pl.when(i == 0)
    def _():
        sbuf[...] = jnp.zeros(sbuf.shape, F32)
        gather_start(0)

    @pl.when(i < total)
    def _():
        prev = jnp.maximum(i - 1, 0)
        fresh = (i == 0) | (ea_ref[i] != ea_ref[prev]) | (eb_ref[i] != eb_ref[prev])

        @pl.when(fresh)
        def _():
            for slot, w_ref in enumerate((wga_ref, wua_ref, wgb_ref, wub_ref)):
                wbf_up_ref[slot] = w_ref[...].astype(BF16)
            for slot, w_ref in enumerate((wda_ref, wdb_ref)):
                wbf_dn_ref[slot] = w_ref[...].astype(BF16)

        gather_wait()
        xcur[...] = gbuf[...]

        @pl.when(i >= 1)
        def _():
            scatter_wait()
            sbuf[...] = ycur[...]

        scatter_start(i - 1)
        gather_start(i + 1)
        x = xcur[...]
        logits = _router_logits(x, w2_ref)
        lane = lax.broadcasted_iota(I32, logits.shape, 1)
        logits = jnp.where(lane < N_EXPERTS, logits, -jnp.inf)
        e = jnp.exp(logits - jnp.max(logits, axis=1, keepdims=True))
        scores = e / jnp.sum(e, axis=1, keepdims=True)
        sa = jnp.sum(jnp.where(lane == ea_ref[i], scores, 0.0), axis=1, keepdims=True)
        sb = jnp.sum(jnp.where(lane == eb_ref[i], scores, 0.0), axis=1, keepdims=True)
        tot = sa + sb
        xb = x.astype(BF16)

        def expert(slot, gate):
            a = jnp.dot(xb, wbf_up_ref[2 * slot], preferred_element_type=F32)
            u = jnp.dot(xb, wbf_up_ref[2 * slot + 1], preferred_element_type=F32)
            hid = (a * _sigmoid(a)) * u * gate
            return jnp.dot(hid.astype(BF16), wbf_dn_ref[slot], preferred_element_type=F32)

        ycur[...] = expert(0, sa / tot) + expert(1, sb / tot)

        @pl.when(i == total - 1)
        def _():
            scatter_wait()
            sbuf[...] = ycur[...]
            scatter_start(i)
            gather_wait()
            scatter_wait()


def _moe(src, dst, ea, eb, nt, h1, w2, wg, wu, wd, layer, n_tiles):
    t, d = h1.shape
    tm = TM_MOE
    f = wg.shape[3]
    idx_a = lambda i, src, dst, ea, eb, nt: (layer, ea[i], 0, 0)
    idx_b = lambda i, src, dst, ea, eb, nt: (layer, eb[i], 0, 0)
    up_a = pl.BlockSpec((None, None, d, f), idx_a)
    up_b = pl.BlockSpec((None, None, d, f), idx_b)
    dn_a = pl.BlockSpec((None, None, f, d), idx_a)
    dn_b = pl.BlockSpec((None, None, f, d), idx_b)
    return pl.pallas_call(
        _moe_kernel,
        grid_spec=pltpu.PrefetchScalarGridSpec(
            num_scalar_prefetch=5,
            grid=(n_tiles,),
            in_specs=[pl.BlockSpec(memory_space=pl.ANY),
                      pl.BlockSpec((d, LANES), lambda i, *_: (0, 0)),
                      up_a, up_a, dn_a, up_b, up_b, dn_b],
            out_specs=pl.BlockSpec(memory_space=pl.ANY),
            scratch_shapes=[pltpu.VMEM((tm, d), F32)] * 4
            + [pltpu.VMEM((4, d, f), BF16), pltpu.VMEM((2, f, d), BF16),
               pltpu.SemaphoreType.DMA(()), pltpu.SemaphoreType.DMA(())],
        ),
        out_shape=jax.ShapeDtypeStruct((t + tm, d), F32),
        compiler_params=_cparams(1),
        name="moe",
    )(src, dst, ea, eb, nt, h1, w2, wg, wu, wd, wg, wu, wd)


def _add_ln_kernel(h_ref, y_ref, g_ref, b_ref, o_ref):
    o_ref[...] = _layer_norm(ALPHA * h_ref[...] + y_ref[...], g_ref[...], b_ref[...])


def _add_ln(h1, ytok, ln_g, ln_b):
    t, d = h1.shape
    tm = TM_ROWS
    row = pl.BlockSpec((tm, d), lambda i: (i, 0))
    vec = pl.BlockSpec((1, d), lambda i: (0, 0))
    return pl.pallas_call(
        _add_ln_kernel,
        grid=(t // tm,),
        in_specs=[row, row, vec, vec],
        out_specs=row,
        out_shape=jax.ShapeDtypeStruct((t, d), F32),
        compiler_params=_cparams(1),
        name="add_ln",
    )(h1, ytok, ln_g, ln_b)


_PAIR_LO = (0, 0, 0, 1, 1, 2)
_PAIR_HI = (1, 2, 3, 2, 3, 3)


def _tile_plan(first, n_tiles):
    tiles = jnp.arange(n_tiles, dtype=I32)
    cls = jnp.sum((first[None, :N_CLASSES] <= tiles[:, None]).astype(I32), axis=1) - 1
    group = cls // N_PAIRS
    pair = cls % N_PAIRS
    lo = jnp.asarray(_PAIR_LO, I32)[pair]
    hi = jnp.asarray(_PAIR_HI, I32)[pair]
    return group * EPG + lo, group * EPG + hi


def _moe_block(h1, cls, w2, wg, wu, wd, layer, ln_g, ln_b):
    t, _ = h1.shape
    n_tiles = t // TM_MOE + N_CLASSES
    pos2d, toff, cnt = _rank(cls)
    first = toff[:, 0]
    src, dst = _slot_map(pos2d.reshape(t), cnt[:, 0], first, n_tiles)
    ea, eb = _tile_plan(first, n_tiles)
    ytok = _moe(src, dst, ea, eb, first[N_CLASSES:N_CLASSES + 1], h1, w2, wg, wu, wd, layer, n_tiles)
    return _add_ln(h1, ytok, ln_g, ln_b)


DEC_LANES = 6


def _decay_placement():
    pq = np.zeros((LANES, FOX_FD), np.float32)
    pk = np.zeros((LANES, FOX_FD), np.float32)
    oq = np.zeros((1, FOX_FD), np.float32)
    ok = np.zeros((1, FOX_FD), np.float32)
    for h in range(FOX_HEADS):
        base = (h // 2) * LANES + (h % 2) * DEC_LANES
        for part in range(3):
            pq[part * FOX_HEADS + h, base + part] = 1.0
            pk[part * FOX_HEADS + h, base + 3 + part] = -1.0
            oq[0, base + 3 + part] = 1.0
            ok[0, base + part] = 1.0
    return pq, pk, oq, ok


def _kv_kernel(x_ref, wk_ref, wv_ref, wf_ref, fb_ref, pq_ref, pk_ref, oq_ref, ok_ref,
               k_ref, v_ref, cq_ref, ck_ref, carry_ref, tril_ref, *, tiles_per_seq):
    i = pl.program_id(0)
    tm = x_ref.shape[0]

    @pl.when(i == 0)
    def _():
        r = lax.broadcasted_iota(I32, (tm, tm), 0)
        cc = lax.broadcasted_iota(I32, (tm, tm), 1)
        tril_ref[...] = (cc <= r).astype(BF16)

    @pl.when(i % tiles_per_seq == 0)
    def _():
        carry_ref[...] = jnp.zeros_like(carry_ref)

    xb = x_ref[...].astype(BF16)
    for w_ref, o_ref in ((wk_ref, k_ref), (wv_ref, v_ref)):
        n = o_ref.shape[1]
        for c in range(n // TN_PROJ):
            cs = slice(c * TN_PROJ, (c + 1) * TN_PROJ)
            o_ref[:, cs] = jnp.dot(xb, w_ref[:, cs], preferred_element_type=F32).astype(o_ref.dtype)
    fl = jnp.dot(xb, wf_ref[...], preferred_element_type=F32) + fb_ref[...]
    lane = lax.broadcasted_iota(I32, fl.shape, 1)
    log_f = jnp.where(lane < FOX_HEADS, _log_sigmoid(fl), 0.0)
    tril = tril_ref[...]
    cum = carry_ref[...]
    for part in _split3(log_f):
        cum = cum + jnp.dot(tril, part.astype(BF16), preferred_element_type=F32)
    carry_ref[...] = cum[tm - 1:tm, :]
    hi, mid, lo = _split3(cum * LOG2E)
    packed = hi + pltpu.roll(mid, FOX_HEADS, axis=1) + pltpu.roll(lo, 2 * FOX_HEADS, axis=1)
    packed = packed.astype(BF16)
    cq_ref[...] = (jnp.dot(packed, pq_ref[...], preferred_element_type=F32)
                   + oq_ref[...]).astype(cq_ref.dtype)
    ck_ref[...] = (jnp.dot(packed, pk_ref[...], preferred_element_type=F32)
                   + ok_ref[...]).astype(ck_ref.dtype)


def _fox_kv(h, wk, wv, wf_pad, fb_pad, seq):
    t, d = h.shape
    tm = TM_PROJ
    pq, pk, oq, ok = _decay_placement()
    full = lambda a: pl.BlockSpec(a.shape, lambda i: (0, 0))
    consts = [jnp.asarray(pq, BF16), jnp.asarray(pk, BF16), jnp.asarray(oq), jnp.asarray(ok)]
    row_out = pl.BlockSpec((tm, FOX_FD), lambda i: (i, 0))
    return pl.pallas_call(
        functools.partial(_kv_kernel, tiles_per_seq=seq // tm),
        grid=(t // tm,),
        in_specs=[pl.BlockSpec((tm, d), lambda i: (i, 0)), full(wk), full(wv), full(wf_pad),
                  full(fb_pad)] + [full(c) for c in consts],
        out_specs=[row_out, row_out, row_out, row_out],
        out_shape=[jax.ShapeDtypeStruct((t, FOX_FD), BF16)] * 4,
        scratch_shapes=[pltpu.VMEM((1, LANES), F32), pltpu.VMEM((tm, tm), BF16)],
        compiler_params=_cparams(1),
        name="fox_kv",
    )(h, wk, wv, wf_pad, fb_pad, *consts)


def _fox_kernel(q_ref, g_ref, cq_ref, k_ref, ck_ref, v_ref, o_ref,
                s_ref, mb_ref, m_ref, acc_ref):
    qi = pl.program_id(2)
    tq = q_ref.shape[0]
    tk = tq
    lane = lax.broadcasted_iota(I32, (tq, LANES), 1)
    klane = lax.broadcasted_iota(I32, (tk, LANES), 1)
    q = q_ref[...]
    cq = cq_ref[...]
    q_augs = []
    for hh in range(2):
        in_head = (lane >= hh * FOX_DH) & (lane < (hh + 1) * FOX_DH)
        in_dec = (lane >= hh * DEC_LANES) & (lane < (hh + 1) * DEC_LANES)
        q_augs.append(jnp.concatenate([jnp.where(in_head, q, jnp.zeros_like(q)),
                                       jnp.where(in_dec, cq, jnp.zeros_like(cq))], axis=1))
    one_lane = (FOX_DH, 0)
    v_keep = [klane < FOX_DH, klane >= FOX_DH]

    def scores(j):
        ks = pl.ds(pl.multiple_of(j * tk, tk), tk)
        k_aug = jnp.concatenate([k_ref[ks, :], ck_ref[ks, :]], axis=1)
        for hh in range(2):
            s = lax.dot_general(q_augs[hh], k_aug, (((1,), (1,)), ((), ())),
                                preferred_element_type=F32)
            s_ref[hh] = s
            mb_ref[hh] = jnp.broadcast_to(jnp.max(s, axis=1, keepdims=True), (tq, LANES))

    def absorb(j, masked):
        ks = pl.ds(pl.multiple_of(j * tk, tk), tk)
        v = v_ref[ks, :]
        for hh in range(2):
            ones = (klane == one_lane[hh]).astype(BF16)
            v_aug = jnp.where(v_keep[hh], v, ones)
            if masked:
                r = lax.broadcasted_iota(I32, (tq, tk), 0)
                cc = lax.broadcasted_iota(I32, (tq, tk), 1)
                s = jnp.where(cc <= r, s_ref[hh], -jnp.inf)
                mb = jnp.broadcast_to(jnp.max(s, axis=1, keepdims=True), (tq, LANES))
            else:
                s = s_ref[hh]
                mb = mb_ref[hh]
            m_old = m_ref[hh]
            m_new = jnp.maximum(m_old, mb)
            m_ref[hh] = m_new
            alpha = jnp.exp2(m_old - m_new)
            p = jnp.exp2((s - jnp.concatenate([m_new] * (tk // LANES), axis=1)).astype(BF16))
            acc_ref[hh] = alpha * acc_ref[hh] + jnp.dot(p, v_aug, preferred_element_type=F32)

    m_ref[...] = jnp.full(m_ref.shape, -jnp.inf, F32)
    acc_ref[...] = jnp.zeros(acc_ref.shape, F32)
    scores(0)

    def body(i, carry):
        for j in (2 * i, 2 * i + 1):
            absorb(j, False)
            scores(j + 1)
        return carry

    lax.fori_loop(0, qi // 2, body, 0)

    @pl.when(qi % 2 == 1)
    def _():
        absorb(qi - 1, False)
        scores(qi)

    absorb(qi, True)
    acc0 = acc_ref[0]
    acc1 = acc_ref[1]
    o = jnp.where(lane < FOX_DH, acc0 / acc0[:, FOX_DH:FOX_DH + 1], acc1 / acc1[:, 0:1])
    o_ref[...] = (o * _sigmoid(g_ref[...].astype(F32))).astype(o_ref.dtype)


def _fox_attn(qg, k, v, cq, ck, batch, seq):
    t = batch * seq
    tq = FOX_TQ
    nq = seq // tq
    n_pairs = FOX_FD // LANES
    q_blk = lambda b, p, i: (b * nq + i, p)
    kv_blk = lambda b, p, i: (b, p)
    return pl.pallas_call(
        _fox_kernel,
        grid=(batch, n_pairs, nq),
        in_specs=[
            pl.BlockSpec((tq, LANES), q_blk),
            pl.BlockSpec((tq, LANES), lambda b, p, i: (b * nq + i, n_pairs + p)),
            pl.BlockSpec((tq, LANES), q_blk),
            pl.BlockSpec((seq, LANES), kv_blk),
            pl.BlockSpec((seq, LANES), kv_blk),
            pl.BlockSpec((seq, LANES), kv_blk),
        ],
        out_specs=pl.BlockSpec((tq, LANES), q_blk),
        out_shape=jax.ShapeDtypeStruct((t, FOX_FD), BF16),
        scratch_shapes=[pltpu.VMEM((2, tq, tq), F32),
                        pltpu.VMEM((2, tq, LANES), F32),
                        pltpu.VMEM((2, tq, LANES), F32),
                        pltpu.VMEM((2, tq, LANES), F32)],
        compiler_params=_cparams(3),
        name="fox_attn",
    )(qg, qg, cq, k, ck, v)


def kernel(x, gla_w_in, gla_w_gk, gla_b_gk, gla_norm_g, gla_w_out, kv_w, forget_bias, fox_w_qg,
           fox_w_out, router_w, router_bias, moe_w_gate, moe_w_up, moe_w_down, ln_g, ln_b):
    batch, seq, d = x.shape
    t = batch * seq
    h = x.reshape(t, d)
    w2 = _router_weight(router_w)
    rb_col = router_bias.reshape(N_EXPERTS, 1)
    n_main = 2 * GLA_HK + 2 * GLA_HV
    k_sh = v_sh = cq_sh = ck_sh = None
    for layer in range(DEPTH):
        if layer < N_A_LAYERS:
            w_in = gla_w_in[layer]
            w_main = w_in[:, :n_main].astype(BF16)
            w_gr = jnp.pad(w_in[:, n_main:], ((0, 0), (0, LANES - GLA_RANK))).astype(BF16)
            qkvg, gr = _proj(h, [w_main, w_gr], [BF16, F32])
            wgk = jnp.pad(gla_w_gk[layer], ((0, LANES - GLA_RANK), (0, 0)))
            wgk_hi = wgk.astype(BF16)
            wgk_pad = jnp.stack([wgk_hi, (wgk - wgk_hi.astype(F32)).astype(BF16)])
            o = _gla_core(qkvg, gr, wgk_pad, gla_b_gk[layer].reshape(1, GLA_HK),
                          gla_norm_g[layer].reshape(1, GLA_DV), batch, seq)
            w_out = gla_w_out[layer].astype(BF16)
        else:
            j = layer - N_A_LAYERS
            w_qg = fox_w_qg[j]
            w_qg = jnp.concatenate([w_qg[:, :FOX_FD] * (FOX_DH ** -0.5 * LOG2E), w_qg[:, FOX_FD:]], axis=1)
            (qg,) = _proj(h, [w_qg.astype(BF16)], [BF16])
            o = _fox_attn(qg, k_sh, v_sh, cq_sh, ck_sh, batch, seq)
            w_out = fox_w_out[j].astype(BF16)
        h1, cls = _post_mixer(o, w_out, h, ln_g[layer, 0].reshape(1, d), ln_b[layer, 0].reshape(1, d),
                              w2, rb_col)
        h = _moe_block(h1, cls, w2, moe_w_gate, moe_w_up, moe_w_down, layer,
                       ln_g[layer, 1].reshape(1, d), ln_b[layer, 1].reshape(1, d))
        if layer == N_A_LAYERS - 1:
            wf_pad = jnp.pad(kv_w[:, 2 * FOX_FD:], ((0, 0), (0, LANES - FOX_HEADS))).astype(BF16)
            fb_pad = jnp.pad(forget_bias, (0, LANES - FOX_HEADS)).reshape(1, LANES)
            k_sh, v_sh, cq_sh, ck_sh = _fox_kv(h, kv_w[:, :FOX_FD].astype(BF16),
                                            kv_w[:, FOX_FD:2 * FOX_FD].astype(BF16), wf_pad, fb_pad, seq)
    return h.reshape(batch, seq, d)
```

```python
import functools
import math

import jax
import jax.numpy as jnp
import numpy as np
from jax import lax
from jax.experimental import pallas as pl
from jax.experimental.pallas import tpu as pltpu

F32 = jnp.float32
BF16 = jnp.bfloat16
I32 = jnp.int32

D_MODEL = 1024
DEPTH = 4
N_A_LAYERS = DEPTH // 2

GLA_HEADS = 4
GLA_DK = 128
GLA_DV = 256
GLA_RANK = 16
GLA_GATE_NORM = 16.0
GLA_CHUNK = 64
GLA_HK = GLA_HEADS * GLA_DK
GLA_HV = GLA_HEADS * GLA_DV

FOX_HEADS = 16
FOX_DH = 64
FOX_FD = FOX_HEADS * FOX_DH

N_EXPERTS = 16
N_GROUPS = 4
EPG = 4
D_EXPERT = 512
N_PAIRS = 6
N_CLASSES = N_GROUPS * N_PAIRS
CLASS_PAD = 32

ALPHA = float((2 * DEPTH) ** 0.25)
LN_EPS = 1e-5
RMS_EPS = 1e-6
LOG2E = 1.4426950408889634

LANES = 128
SUBLANES = 8
D_CHUNKS = D_MODEL // LANES
VMEM_LIMIT = 48 * 1024 * 1024

TM_PROJ = 512
TN_PROJ = 512
GLA_ROWS = 256
TM_POST = 512
TM_RANK = 512
TM_ROWS = 512
TM_MOE = 256
FOX_TQ = 512


def _cparams(n_axes):
    return pltpu.CompilerParams(dimension_semantics=("arbitrary",) * n_axes,
                                vmem_limit_bytes=VMEM_LIMIT)


def _log_sigmoid(x):
    return jnp.minimum(x, 0.0) - jnp.log(1.0 + jnp.exp(-jnp.abs(x)))


def _sigmoid(x):
    return 1.0 / (1.0 + jnp.exp(-x))


def _layer_norm(z, g, b):
    mu = jnp.mean(z, axis=-1, keepdims=True)
    zc = z - mu
    var = jnp.mean(zc * zc, axis=-1, keepdims=True)
    return zc * lax.rsqrt(var + LN_EPS) * g + b


def _split3(x):
    hi = x.astype(BF16).astype(F32)
    r = x - hi
    mid = r.astype(BF16).astype(F32)
    lo = (r - mid).astype(BF16).astype(F32)
    return hi, mid, lo


def _rows_to_tiles(tile_ref, x):
    for j in range(D_CHUNKS):
        tile_ref[:, j, :] = x[:, j * LANES:(j + 1) * LANES]


def _tiles_to_rows(tile_ref):
    return jnp.concatenate([tile_ref[:, j, :] for j in range(D_CHUNKS)], axis=1)


def _proj_kernel(x_ref, *refs, n_w):
    xb = x_ref[...].astype(BF16)
    for w_ref, o_ref in zip(refs[:n_w], refs[n_w:]):
        n = o_ref.shape[1]
        tn = min(TN_PROJ, n)
        for c in range(n // tn):
            o_ref[:, c * tn:(c + 1) * tn] = jnp.dot(
                xb, w_ref[:, c * tn:(c + 1) * tn], preferred_element_type=F32).astype(o_ref.dtype)


def _proj(x, ws, out_dtypes):
    t, k = x.shape
    tm = TM_PROJ
    return pl.pallas_call(
        functools.partial(_proj_kernel, n_w=len(ws)),
        grid=(t // tm,),
        in_specs=[pl.BlockSpec((tm, k), lambda i: (i, 0))]
        + [pl.BlockSpec(w.shape, lambda i: (0, 0)) for w in ws],
        out_specs=[pl.BlockSpec((tm, w.shape[1]), lambda i: (i, 0)) for w in ws],
        out_shape=[jax.ShapeDtypeStruct((t, w.shape[1]), dt) for w, dt in zip(ws, out_dtypes)],
        compiler_params=_cparams(1),
        name="proj",
    )(x, *ws)


def _gla_kernel(q_ref, k_ref, v_ref, g_ref, gr_ref, wgk_ref, bgk_ref, ng_ref, o_ref, state_ref):
    c = GLA_CHUNK

    @pl.when(pl.program_id(1) == 0)
    def _():
        state_ref[...] = jnp.zeros_like(state_ref)

    gr = gr_ref[...]
    gr_hi = gr.astype(BF16)
    gr_lo = (gr - gr_hi.astype(F32)).astype(BF16)
    w_hi = wgk_ref[0]
    gkz = (jnp.dot(gr_hi, w_hi, preferred_element_type=F32)
           + jnp.dot(gr_lo, w_hi, preferred_element_type=F32)
           + jnp.dot(gr_hi, wgk_ref[1], preferred_element_type=F32)) + bgk_ref[...]
    gk = _log_sigmoid(gkz) * (1.0 / GLA_GATE_NORM)
    row = lax.broadcasted_iota(I32, (c, c), 0)
    col = lax.broadcasted_iota(I32, (c, c), 1)
    causal = col <= row
    tril = jnp.where(causal, 1.0, 0.0).astype(BF16)
    gk_parts = [part.astype(BF16) for part in _split3(gk)]
    scale = GLA_DK ** -0.5
    ng = ng_ref[...]

    for ci in range(GLA_ROWS // c):
        rs = slice(ci * c, (ci + 1) * c)
        bc = sum(jnp.dot(tril, part[rs], preferred_element_type=F32) for part in gk_parts)
        b_last = bc[c - 1:c, :]
        qf = q_ref[rs, :].astype(F32)
        kf = k_ref[rs, :].astype(F32)
        q_dec = (qf * scale * jnp.exp(bc)).astype(BF16)
        k_inv = (kf * jnp.exp(-bc)).astype(BF16)
        k_end = (kf * jnp.exp(b_last - bc)).astype(BF16)
        dec = jnp.exp(b_last)
        for h in range(GLA_HEADS):
            ks = slice(h * GLA_DK, (h + 1) * GLA_DK)
            vs = slice(h * GLA_DV, (h + 1) * GLA_DV)
            v_h = v_ref[rs, vs]
            attn = lax.dot_general(q_dec[:, ks], k_inv[:, ks], (((1,), (1,)), ((), ())),
                                   preferred_element_type=F32)
            attn = jnp.where(causal, attn, 0.0).astype(BF16)
            st = state_ref[h]
            o = jnp.dot(attn, v_h, preferred_element_type=F32)
            o = o + lax.dot_general(q_dec[:, ks], st.astype(BF16), (((1,), (1,)), ((), ())),
                                    preferred_element_type=F32)
            kv_t = lax.dot_general(v_h, k_end[:, ks], (((0,), (0,)), ((), ())),
                                   preferred_element_type=F32)
            state_ref[h] = st * dec[:, ks] + kv_t
            o = o * lax.rsqrt(jnp.mean(o * o, axis=-1, keepdims=True) + RMS_EPS) * ng
            gate = g_ref[rs, vs].astype(F32)
            o = o * (gate * _sigmoid(gate))
            o_ref[rs, vs] = o.astype(o_ref.dtype)


def _gla_core(qkvg, gr, wgk_pad, bgk, ng, batch, seq):
    t = batch * seq
    r = GLA_ROWS
    nblk = seq // r
    rowmap = lambda b, i: b * nblk + i
    return pl.pallas_call(
        _gla_kernel,
        grid=(batch, nblk),
        in_specs=[
            pl.BlockSpec((r, GLA_HK), lambda b, i: (rowmap(b, i), 0)),
            pl.BlockSpec((r, GLA_HK), lambda b, i: (rowmap(b, i), 1)),
            pl.BlockSpec((r, GLA_HV), lambda b, i: (rowmap(b, i), 1)),
            pl.BlockSpec((r, GLA_HV), lambda b, i: (rowmap(b, i), 2)),
            pl.BlockSpec((r, LANES), lambda b, i: (rowmap(b, i), 0)),
            pl.BlockSpec((2, LANES, GLA_HK), lambda b, i: (0, 0, 0)),
            pl.BlockSpec((1, GLA_HK), lambda b, i: (0, 0)),
            pl.BlockSpec((1, GLA_DV), lambda b, i: (0, 0)),
        ],
        out_specs=pl.BlockSpec((r, GLA_HV), lambda b, i: (rowmap(b, i), 0)),
        out_shape=jax.ShapeDtypeStruct((t, GLA_HV), BF16),
        scratch_shapes=[pltpu.VMEM((GLA_HEADS, GLA_DV, GLA_DK), F32)],
        compiler_params=_cparams(2),
        name="gla_core",
    )(qkvg, qkvg, qkvg, qkvg, gr, wgk_pad, bgk, ng)


def _route_class(logits_t, bias_col):
    mx = jnp.max(logits_t, axis=0, keepdims=True)
    e = jnp.exp(logits_t - mx)
    scores = e / jnp.sum(e, axis=0, keepdims=True)
    sel = scores + bias_col
    rows = [sel[j:j + 1, :] for j in range(N_EXPERTS)]
    best_g = None
    best_s = None
    for g in range(N_GROUPS):
        m = rows[g * EPG:(g + 1) * EPG]
        gs = None
        for a in range(EPG):
            for b in range(a + 1, EPG):
                s = m[a] + m[b]
                gs = s if gs is None else jnp.maximum(gs, s)
        if g == 0:
            best_g = jnp.zeros(gs.shape, I32)
            best_s = gs
        else:
            better = gs > best_s
            best_g = jnp.where(better, g, best_g)
            best_s = jnp.where(better, gs, best_s)
    mem = []
    for j in range(EPG):
        vj = rows[j]
        for g in range(1, N_GROUPS):
            vj = jnp.where(best_g == g, rows[g * EPG + j], vj)
        mem.append(vj)
    i1 = jnp.zeros(best_g.shape, I32)
    b1 = mem[0]
    for j in range(1, EPG):
        better = mem[j] > b1
        i1 = jnp.where(better, j, i1)
        b1 = jnp.where(better, mem[j], b1)
    i2 = jnp.where(i1 == 0, 1, 0).astype(I32)
    b2 = jnp.where(i1 == 0, mem[1], mem[0])
    for j in range(1, EPG):
        better = (mem[j] > b2) & (i1 != j) & (i2 != j)
        i2 = jnp.where(better, j, i2)
        b2 = jnp.where(better, mem[j], b2)
    lo = jnp.minimum(i1, i2)
    hi = jnp.maximum(i1, i2)
    pair = jnp.where(lo == 0, hi - 1, jnp.where(lo == 1, hi + 1, 5))
    return best_g * N_PAIRS + pair


def _router_weight(router_w):
    hi = router_w.astype(BF16)
    lo = (router_w - hi.astype(F32)).astype(BF16)
    pad = jnp.zeros((router_w.shape[0], LANES - 2 * N_EXPERTS), BF16)
    return jnp.concatenate([hi, lo, pad], axis=1)


def _router_logits(x, w2_ref):
    hi = x.astype(BF16)
    lo = (x - hi.astype(F32)).astype(BF16)
    w2 = w2_ref[...]
    a = jnp.dot(hi, w2, preferred_element_type=F32) + jnp.dot(lo, w2, preferred_element_type=F32)
    return a + pltpu.roll(a, LANES - N_EXPERTS, axis=1)


def _post_kernel(o_ref, w_ref, h_ref, g_ref, b_ref, w2_ref, rb_ref, h1_ref, cls_ref):
    mix = jnp.dot(o_ref[...], w_ref[...], preferred_element_type=F32)
    h1 = _layer_norm(ALPHA * h_ref[...] + mix, g_ref[...], b_ref[...])
    _rows_to_tiles(h1_ref, h1)
    logits_t = _router_logits(h1, w2_ref).T[:N_EXPERTS, :]
    cls_ref[...] = _route_class(logits_t, rb_ref[...])


def _post_mixer(o, w_out, h, ln_g, ln_b, w2, rb_col):
    t, d = h.shape
    tm = TM_POST
    kdim = o.shape[1]
    return pl.pallas_call(
        _post_kernel,
        grid=(t // tm,),
        in_specs=[
            pl.BlockSpec((tm, kdim), lambda i: (i, 0)),
            pl.BlockSpec((kdim, d), lambda i: (0, 0)),
            pl.BlockSpec((tm, d), lambda i: (i, 0)),
            pl.BlockSpec((1, d), lambda i: (0, 0)),
            pl.BlockSpec((1, d), lambda i: (0, 0)),
            pl.BlockSpec((d, LANES), lambda i: (0, 0)),
            pl.BlockSpec((N_EXPERTS, 1), lambda i: (0, 0)),
        ],
        out_specs=[pl.BlockSpec((tm, D_CHUNKS, LANES), lambda i: (i, 0, 0)),
                   pl.BlockSpec((1, tm), lambda i: (0, i))],
        out_shape=[jax.ShapeDtypeStruct((t, D_CHUNKS, LANES), F32), jax.ShapeDtypeStruct((1, t), I32)],
        compiler_params=_cparams(1),
        name="post_mixer",
    )(o, w_out, h, ln_g, ln_b, w2, rb_col)


def _rank_kernel(cls_ref, pos_ref, toff_ref, cnt_out_ref, cnt_ref, carry_ref, off_ref, tri_ref):
    phase = pl.program_id(0)
    i = pl.program_id(1)
    tm = cls_ref.shape[1]
    onehot = (lax.broadcasted_iota(I32, (CLASS_PAD, tm), 0) == cls_ref[...]).astype(F32)
    tile_count = jnp.sum(onehot, axis=1, keepdims=True)

    @pl.when((phase == 0) & (i == 0))
    def _():
        cnt_ref[...] = jnp.zeros_like(cnt_ref)
        r = lax.broadcasted_iota(I32, (tm, tm), 0)
        cc = lax.broadcasted_iota(I32, (tm, tm), 1)
        tri_ref[...] = (r <= cc).astype(BF16)

    @pl.when(phase == 0)
    def _():
        cnt_ref[...] += jnp.broadcast_to(tile_count, cnt_ref.shape)

    @pl.when((phase == 1) & (i == 0))
    def _():
        ntile = jnp.floor((cnt_ref[...] + (TM_MOE - 1)) * (1.0 / TM_MOE))
        r = lax.broadcasted_iota(I32, (CLASS_PAD, CLASS_PAD), 0)
        cc = lax.broadcasted_iota(I32, (CLASS_PAD, CLASS_PAD), 1)
        strict = (cc < r).astype(BF16)
        first_tile = jnp.dot(strict, ntile.astype(BF16), preferred_element_type=F32)
        toff_ref[...] = first_tile.astype(I32)
        cnt_out_ref[...] = cnt_ref[...].astype(I32)
        off_ref[...] = first_tile * float(TM_MOE)
        carry_ref[...] = jnp.zeros_like(carry_ref)

    @pl.when(phase == 1)
    def _():
        prefix = jnp.dot(onehot.astype(BF16), tri_ref[...], preferred_element_type=F32)
        base = off_ref[:, 0:1] + carry_ref[:, 0:1] - 1.0
        posf = jnp.sum(onehot * (prefix + base), axis=0, keepdims=True)
        pos_ref[...] = posf.astype(I32)
        carry_ref[...] += jnp.broadcast_to(tile_count, carry_ref.shape)


def _rank(cls):
    t = cls.shape[1]
    tm = TM_RANK
    return pl.pallas_call(
        _rank_kernel,
        grid=(2, t // tm),
        in_specs=[pl.BlockSpec((1, tm), lambda p, i: (0, i))],
        out_specs=[pl.BlockSpec((1, tm), lambda p, i: (0, i * p)),
                   pl.BlockSpec((CLASS_PAD, LANES), lambda p, i: (0, 0)),
                   pl.BlockSpec((CLASS_PAD, LANES), lambda p, i: (0, 0))],
        out_shape=[jax.ShapeDtypeStruct((1, t), I32),
                   jax.ShapeDtypeStruct((CLASS_PAD, LANES), I32),
                   jax.ShapeDtypeStruct((CLASS_PAD, LANES), I32)],
        scratch_shapes=[pltpu.VMEM((CLASS_PAD, LANES), F32),
                        pltpu.VMEM((CLASS_PAD, LANES), F32),
                        pltpu.VMEM((CLASS_PAD, LANES), F32),
                        pltpu.VMEM((tm, tm), BF16)],
        compiler_params=_cparams(2),
        name="rank",
    )(cls)


def _slot_map_kernel(pos_ref, cnt_ref, first_ref, src_ref, dst_ref, *, n_tiles):
    t = pos_ref.shape[0]
    tm = TM_MOE

    def lead(r, carry):
        dst_ref[r] = t + r
        return carry

    lax.fori_loop(0, tm, lead, 0, unroll=8)

    def pad(p, carry):
        src_ref[p] = 0
        dst_ref[tm + p] = t + (p & (tm - 1))
        return carry

    for c in range(N_CLASSES):
        lax.fori_loop(first_ref[c] * tm + cnt_ref[c], first_ref[c + 1] * tm, pad, 0)
    lax.fori_loop(first_ref[N_CLASSES] * tm, n_tiles * tm, pad, 0)

    def last(r, carry):
        src_ref[n_tiles * tm + r] = 0
        return carry

    lax.fori_loop(0, tm, last, 0, unroll=8)

    def token(tok, carry):
        p = pos_ref[tok]
        src_ref[p] = tok
        dst_ref[tm + p] = tok
        return carry

    lax.fori_loop(0, t, token, 0, unroll=8)


def _slot_map(pos, cnt, first, n_tiles):
    n = (n_tiles + 1) * TM_MOE
    smem = pl.BlockSpec(memory_space=pltpu.SMEM)
    return pl.pallas_call(
        functools.partial(_slot_map_kernel, n_tiles=n_tiles),
        grid_spec=pltpu.PrefetchScalarGridSpec(
            num_scalar_prefetch=3, grid=(1,), in_specs=[], out_specs=[smem, smem]),
        out_shape=[jax.ShapeDtypeStruct((n,), I32), jax.ShapeDtypeStruct((n,), I32)],
        compiler_params=_cparams(1),
        name="slot_map",
    )(pos, cnt, first)


def _moe_kernel(src_ref, dst_ref, ea_ref, eb_ref, nt_ref, h1_ref, w2_ref, wga_ref, wua_ref, wda_ref,
                wgb_ref, wub_ref, wdb_ref, ytok_ref, gbuf, xcur, ycur, sbuf, wbf_up_ref, wbf_dn_ref,
                gsem, ssem):
    i = pl.program_id(0)
    total = nt_ref[0]
    tm = TM_MOE

    def gather_start(tile):
        base = tile * tm
        for r in range(tm):
            pltpu.make_async_copy(h1_ref.at[pl.ds(src_ref[base + r], 1)],
                                  gbuf.at[pl.ds(r, 1)], gsem).start()

    def gather_wait():
        pltpu.make_async_copy(h1_ref.at[pl.ds(0, tm)], gbuf, gsem).wait()

    def scatter_start(tile):
        base = (tile + 1) * tm
        for r in range(tm):
            pltpu.make_async_copy(sbuf.at[pl.ds(r, 1)],
                                  ytok_ref.at[pl.ds(dst_ref[base + r], 1)], ssem).start()

    def scatter_wait():
        pltpu.make_async_copy(sbuf, ytok_ref.at[pl.ds(0, tm)], ssem).wait()

    @pl.when(i == 0)
    def _():
        sbuf[...] = jnp.zeros(sbuf.shape, F32)
        gather_start(0)

    @pl.when(i < total)
    def _():
        prev = jnp.maximum(i - 1, 0)
        fresh = (i == 0) | (ea_ref[i] != ea_ref[prev]) | (eb_ref[i] != eb_ref[prev])

        @pl.when(fresh)
        def _():
            for slot, w_ref in enumerate((wga_ref, wua_ref, wgb_ref, wub_ref)):
                wbf_up_ref[slot] = w_ref[...].astype(BF16)
            for slot, w_ref in enumerate((wda_ref, wdb_ref)):
                wbf_dn_ref[slot] = w_ref[...].astype(BF16)

        gather_wait()
        xcur[...] = _tiles_to_rows(gbuf)

        @pl.when(i >= 1)
        def _():
            scatter_wait()
            _rows_to_tiles(sbuf, ycur[...])

        scatter_start(i - 1)
        gather_start(i + 1)
        x = xcur[...]
        logits = _router_logits(x, w2_ref)
        lane = lax.broadcasted_iota(I32, logits.shape, 1)
        logits = jnp.where(lane < N_EXPERTS, logits, -jnp.inf)
        e = jnp.exp(logits - jnp.max(logits, axis=1, keepdims=True))
        scores = e / jnp.sum(e, axis=1, keepdims=True)
        sa = jnp.sum(jnp.where(lane == ea_ref[i], scores, 0.0), axis=1, keepdims=True)
        sb = jnp.sum(jnp.where(lane == eb_ref[i], scores, 0.0), axis=1, keepdims=True)
        tot = sa + sb
        xb = x.astype(BF16)

        def expert(slot, gate):
            a = jnp.dot(xb, wbf_up_ref[2 * slot], preferred_element_type=F32)
            u = jnp.dot(xb, wbf_up_ref[2 * slot + 1], preferred_element_type=F32)
            hid = (a * _sigmoid(a)) * u * gate
            return jnp.dot(hid.astype(BF16), wbf_dn_ref[slot], preferred_element_type=F32)

        ycur[...] = expert(0, sa / tot) + expert(1, sb / tot)

        @pl.when(i == total - 1)
        def _():
            scatter_wait()
            _rows_to_tiles(sbuf, ycur[...])
            scatter_start(i)
            gather_wait()
            scatter_wait()


def _moe(src, dst, ea, eb, nt, h1, w2, wg, wu, wd, layer, n_tiles):
    t = h1.shape[0]
    d = D_MODEL
    tm = TM_MOE
    f = wg.shape[3]
    idx_a = lambda i, src, dst, ea, eb, nt: (layer, ea[i], 0, 0)
    idx_b = lambda i, src, dst, ea, eb, nt: (layer, eb[i], 0, 0)
    up_a = pl.BlockSpec((None, None, d, f), idx_a)
    up_b = pl.BlockSpec((None, None, d, f), idx_b)
    dn_a = pl.BlockSpec((None, None, f, d), idx_a)
    dn_b = pl.BlockSpec((None, None, f, d), idx_b)
    return pl.pallas_call(
        _moe_kernel,
        grid_spec=pltpu.PrefetchScalarGridSpec(
            num_scalar_prefetch=5,
            grid=(n_tiles,),
            in_specs=[pl.BlockSpec(memory_space=pl.ANY),
                      pl.BlockSpec((d, LANES), lambda i, *_: (0, 0)),
                      up_a, up_a, dn_a, up_b, up_b, dn_b],
            out_specs=pl.BlockSpec(memory_space=pl.ANY),
            scratch_shapes=[pltpu.VMEM((tm, D_CHUNKS, LANES), F32), pltpu.VMEM((tm, d), F32),
                            pltpu.VMEM((tm, d), F32), pltpu.VMEM((tm, D_CHUNKS, LANES), F32),
                            pltpu.VMEM((4, d, f), BF16), pltpu.VMEM((2, f, d), BF16),
                            pltpu.SemaphoreType.DMA(()), pltpu.SemaphoreType.DMA(())],
        ),
        out_shape=jax.ShapeDtypeStruct((t + tm, D_CHUNKS, LANES), F32),
        compiler_params=_cparams(1),
        name="moe",
    )(src, dst, ea, eb, nt, h1, w2, wg, wu, wd, wg, wu, wd)


def _add_ln_kernel(h_ref, y_ref, g_ref, b_ref, o_ref):
    z = ALPHA * _tiles_to_rows(h_ref) + _tiles_to_rows(y_ref)
    o_ref[...] = _layer_norm(z, g_ref[...], b_ref[...])


def _add_ln(h1, ytok, ln_g, ln_b):
    t = h1.shape[0]
    d = D_MODEL
    tm = TM_ROWS
    tiles = pl.BlockSpec((tm, D_CHUNKS, LANES), lambda i: (i, 0, 0))
    vec = pl.BlockSpec((1, d), lambda i: (0, 0))
    return pl.pallas_call(
        _add_ln_kernel,
        grid=(t // tm,),
        in_specs=[tiles, tiles, vec, vec],
        out_specs=pl.BlockSpec((tm, d), lambda i: (i, 0)),
        out_shape=jax.ShapeDtypeStruct((t, d), F32),
        compiler_params=_cparams(1),
        name="add_ln",
    )(h1, ytok, ln_g, ln_b)


_PAIR_LO = (0, 0, 0, 1, 1, 2)
_PAIR_HI = (1, 2, 3, 2, 3, 3)


def _tile_plan(first, n_tiles):
    tiles = jnp.arange(n_tiles, dtype=I32)
    cls = jnp.sum((first[None, :N_CLASSES] <= tiles[:, None]).astype(I32), axis=1) - 1
    group = cls // N_PAIRS
    pair = cls % N_PAIRS
    lo = jnp.asarray(_PAIR_LO, I32)[pair]
    hi = jnp.asarray(_PAIR_HI, I32)[pair]
    return group * EPG + lo, group * EPG + hi


def _moe_block(h1, cls, w2, wg, wu, wd, layer, ln_g, ln_b):
    t = h1.shape[0]
    n_tiles = t // TM_MOE + N_CLASSES
    pos2d, toff, cnt = _rank(cls)
    first = toff[:, 0]
    src, dst = _slot_map(pos2d.reshape(t), cnt[:, 0], first, n_tiles)
    ea, eb = _tile_plan(first, n_tiles)
    ytok = _moe(src, dst, ea, eb, first[N_CLASSES:N_CLASSES + 1], h1, w2, wg, wu, wd, layer, n_tiles)
    return _add_ln(h1, ytok, ln_g, ln_b)


DEC_LANES = 6


def _decay_placement():
    pq = np.zeros((LANES, FOX_FD), np.float32)
    pk = np.zeros((LANES, FOX_FD), np.float32)
    oq = np.zeros((1, FOX_FD), np.float32)
    ok = np.zeros((1, FOX_FD), np.float32)
    for h in range(FOX_HEADS):
        base = (h // 2) * LANES + (h % 2) * DEC_LANES
        for part in range(3):
            pq[part * FOX_HEADS + h, base + part] = 1.0
            pk[part * FOX_HEADS + h, base + 3 + part] = -1.0
            oq[0, base + 3 + part] = 1.0
            ok[0, base + part] = 1.0
    return pq, pk, oq, ok


def _kv_kernel(x_ref, wk_ref, wv_ref, wf_ref, fb_ref, pq_ref, pk_ref, oq_ref, ok_ref,
               k_ref, v_ref, cq_ref, ck_ref, carry_ref, tril_ref, *, tiles_per_seq):
    i = pl.program_id(0)
    tm = x_ref.shape[0]

    @pl.when(i == 0)
    def _():
        r = lax.broadcasted_iota(I32, (tm, tm), 0)
        cc = lax.broadcasted_iota(I32, (tm, tm), 1)
        tril_ref[...] = (cc <= r).astype(BF16)

    @pl.when(i % tiles_per_seq == 0)
    def _():
        carry_ref[...] = jnp.zeros_like(carry_ref)

    xb = x_ref[...].astype(BF16)
    for w_ref, o_ref in ((wk_ref, k_ref), (wv_ref, v_ref)):
        n = o_ref.shape[1]
        for c in range(n // TN_PROJ):
            cs = slice(c * TN_PROJ, (c + 1) * TN_PROJ)
            o_ref[:, cs] = jnp.dot(xb, w_ref[:, cs], preferred_element_type=F32).astype(o_ref.dtype)
    fl = jnp.dot(xb, wf_ref[...], preferred_element_type=F32) + fb_ref[...]
    lane = lax.broadcasted_iota(I32, fl.shape, 1)
    log_f = jnp.where(lane < FOX_HEADS, _log_sigmoid(fl), 0.0)
    tril = tril_ref[...]
    cum = carry_ref[...]
    for part in _split3(log_f):
        cum = cum + jnp.dot(tril, part.astype(BF16), preferred_element_type=F32)
    carry_ref[...] = cum[tm - 1:tm, :]
    hi, mid, lo = _split3(cum * LOG2E)
    packed = hi + pltpu.roll(mid, FOX_HEADS, axis=1) + pltpu.roll(lo, 2 * FOX_HEADS, axis=1)
    packed = packed.astype(BF16)
    cq_ref[...] = (jnp.dot(packed, pq_ref[...], preferred_element_type=F32)
                   + oq_ref[...]).astype(cq_ref.dtype)
    ck_ref[...] = (jnp.dot(packed, pk_ref[...], preferred_element_type=F32)
                   + ok_ref[...]).astype(ck_ref.dtype)


def _fox_kv(h, wk, wv, wf_pad, fb_pad, seq):
    t, d = h.shape
    tm = TM_PROJ
    pq, pk, oq, ok = _decay_placement()
    full = lambda a: pl.BlockSpec(a.shape, lambda i: (0, 0))
    consts = [jnp.asarray(pq, BF16), jnp.asarray(pk, BF16), jnp.asarray(oq), jnp.asarray(ok)]
    row_out = pl.BlockSpec((tm, FOX_FD), lambda i: (i, 0))
    return pl.pallas_call(
        functools.partial(_kv_kernel, tiles_per_seq=seq // tm),
        grid=(t // tm,),
        in_specs=[pl.BlockSpec((tm, d), lambda i: (i, 0)), full(wk), full(wv), full(wf_pad),
                  full(fb_pad)] + [full(c) for c in consts],
        out_specs=[row_out, row_out, row_out, row_out],
        out_shape=[jax.ShapeDtypeStruct((t, FOX_FD), BF16)] * 4,
        scratch_shapes=[pltpu.VMEM((1, LANES), F32), pltpu.VMEM((tm, tm), BF16)],
        compiler_params=_cparams(1),
        name="fox_kv",
    )(h, wk, wv, wf_pad, fb_pad, *consts)


def _fox_kernel(q_ref, g_ref, cq_ref, k_ref, ck_ref, v_ref, o_ref,
                s_ref, mb_ref, m_ref, acc_ref):
    qi = pl.program_id(2)
    tq = q_ref.shape[0]
    tk = tq
    lane = lax.broadcasted_iota(I32, (tq, LANES), 1)
    klane = lax.broadcasted_iota(I32, (tk, LANES), 1)
    q = q_ref[...]
    cq = cq_ref[...]
    q_augs = []
    for hh in range(2):
        in_head = (lane >= hh * FOX_DH) & (lane < (hh + 1) * FOX_DH)
        in_dec = (lane >= hh * DEC_LANES) & (lane < (hh + 1) * DEC_LANES)
        q_augs.append(jnp.concatenate([jnp.where(in_head, q, jnp.zeros_like(q)),
                                       jnp.where(in_dec, cq, jnp.zeros_like(cq))], axis=1))
    one_lane = (FOX_DH, 0)
    v_keep = [klane < FOX_DH, klane >= FOX_DH]

    def scores(j):
        ks = pl.ds(pl.multiple_of(j * tk, tk), tk)
        k_aug = jnp.concatenate([k_ref[ks, :], ck_ref[ks, :]], axis=1)
        for hh in range(2):
            s = lax.dot_general(q_augs[hh], k_aug, (((1,), (1,)), ((), ())),
                                preferred_element_type=F32)
            s_ref[hh] = s
            mb_ref[hh] = jnp.broadcast_to(jnp.max(s, axis=1, keepdims=True), (tq, LANES))

    def absorb(j, masked):
        ks = pl.ds(pl.multiple_of(j * tk, tk), tk)
        v = v_ref[ks, :]
        for hh in range(2):
            ones = (klane == one_lane[hh]).astype(BF16)
            v_aug = jnp.where(v_keep[hh], v, ones)
            if masked:
                r = lax.broadcasted_iota(I32, (tq, tk), 0)
                cc = lax.broadcasted_iota(I32, (tq, tk), 1)
                s = jnp.where(cc <= r, s_ref[hh], -jnp.inf)
                mb = jnp.broadcast_to(jnp.max(s, axis=1, keepdims=True), (tq, LANES))
            else:
                s = s_ref[hh]
                mb = mb_ref[hh]
            m_old = m_ref[hh]
            m_new = jnp.maximum(m_old, mb)
            m_ref[hh] = m_new
            alpha = jnp.exp2(m_old - m_new)
            p = jnp.exp2((s - jnp.concatenate([m_new] * (tk // LANES), axis=1)).astype(BF16))
            acc_ref[hh] = alpha * acc_ref[hh] + jnp.dot(p, v_aug, preferred_element_type=F32)

    m_ref[...] = jnp.full(m_ref.shape, -jnp.inf, F32)
    acc_ref[...] = jnp.zeros(acc_ref.shape, F32)
    scores(0)

    def body(i, carry):
        for j in (2 * i, 2 * i + 1):
            absorb(j, False)
            scores(j + 1)
        return carry

    lax.fori_loop(0, qi // 2, body, 0)

    @pl.when(qi % 2 == 1)
    def _():
        absorb(qi - 1, False)
        scores(qi)

    absorb(qi, True)
    acc0 = acc_ref[0]
    acc1 = acc_ref[1]
    o = jnp.where(lane < FOX_DH, acc0 / acc0[:, FOX_DH:FOX_DH + 1], acc1 / acc1[:, 0:1])
    o_ref[...] = (o * _sigmoid(g_ref[...].astype(F32))).astype(o_ref.dtype)


def _fox_attn(qg, k, v, cq, ck, batch, seq):
    t = batch * seq
    tq = FOX_TQ
    nq = seq // tq
    n_pairs = FOX_FD // LANES
    q_blk = lambda b, p, i: (b * nq + i, p)
    kv_blk = lambda b, p, i: (b, p)
    return pl.pallas_call(
        _fox_kernel,
        grid=(batch, n_pairs, nq),
        in_specs=[
            pl.BlockSpec((tq, LANES), q_blk),
            pl.BlockSpec((tq, LANES), lambda b, p, i: (b * nq + i, n_pairs + p)),
            pl.BlockSpec((tq, LANES), q_blk),
            pl.BlockSpec((seq, LANES), kv_blk),
            pl.BlockSpec((seq, LANES), kv_blk),
            pl.BlockSpec((seq, LANES), kv_blk),
        ],
        out_specs=pl.BlockSpec((tq, LANES), q_blk),
        out_shape=jax.ShapeDtypeStruct((t, FOX_FD), BF16),
        scratch_shapes=[pltpu.VMEM((2, tq, tq), F32),
                        pltpu.VMEM((2, tq, LANES), F32),
                        pltpu.VMEM((2, tq, LANES), F32),
                        pltpu.VMEM((2, tq, LANES), F32)],
        compiler_params=_cparams(3),
        name="fox_attn",
    )(qg, qg, cq, k, ck, v)


def kernel(x, gla_w_in, gla_w_gk, gla_b_gk, gla_norm_g, gla_w_out, kv_w, forget_bias, fox_w_qg,
           fox_w_out, router_w, router_bias, moe_w_gate, moe_w_up, moe_w_down, ln_g, ln_b):
    batch, seq, d = x.shape
    t = batch * seq
    h = x.reshape(t, d)
    w2 = _router_weight(router_w)
    rb_col = router_bias.reshape(N_EXPERTS, 1)
    n_main = 2 * GLA_HK + 2 * GLA_HV
    k_sh = v_sh = cq_sh = ck_sh = None
    for layer in range(DEPTH):
        if layer < N_A_LAYERS:
            w_in = gla_w_in[layer]
            w_main = w_in[:, :n_main].astype(BF16)
            w_gr = jnp.pad(w_in[:, n_main:], ((0, 0), (0, LANES - GLA_RANK))).astype(BF16)
            qkvg, gr = _proj(h, [w_main, w_gr], [BF16, F32])
            wgk = jnp.pad(gla_w_gk[layer], ((0, LANES - GLA_RANK), (0, 0)))
            wgk_hi = wgk.astype(BF16)
            wgk_pad = jnp.stack([wgk_hi, (wgk - wgk_hi.astype(F32)).astype(BF16)])
            o = _gla_core(qkvg, gr, wgk_pad, gla_b_gk[layer].reshape(1, GLA_HK),
                          gla_norm_g[layer].reshape(1, GLA_DV), batch, seq)
            w_out = gla_w_out[layer].astype(BF16)
        else:
            j = layer - N_A_LAYERS
            w_qg = fox_w_qg[j]
            w_qg = jnp.concatenate([w_qg[:, :FOX_FD] * (FOX_DH ** -0.5 * LOG2E), w_qg[:, FOX_FD:]], axis=1)
            (qg,) = _proj(h, [w_qg.astype(BF16)], [BF16])
            o = _fox_attn(qg, k_sh, v_sh, cq_sh, ck_sh, batch, seq)
            w_out = fox_w_out[j].astype(BF16)
        h1, cls = _post_mixer(o, w_out, h, ln_g[layer, 0].reshape(1, d), ln_b[layer, 0].reshape(1, d),
                              w2, rb_col)
        h = _moe_block(h1, cls, w2, moe_w_gate, moe_w_up, moe_w_down, layer,
                       ln_g[layer, 1].reshape(1, d), ln_b[layer, 1].reshape(1, d))
        if layer == N_A_LAYERS - 1:
            wf_pad = jnp.pad(kv_w[:, 2 * FOX_FD:], ((0, 0), (0, LANES - FOX_HEADS))).astype(BF16)
            fb_pad = jnp.pad(forget_bias, (0, LANES - FOX_HEADS)).reshape(1, LANES)
            k_sh, v_sh, cq_sh, ck_sh = _fox_kv(h, kv_w[:, :FOX_FD].astype(BF16),
                                            kv_w[:, FOX_FD:2 * FOX_FD].astype(BF16), wf_pad, fb_pad, seq)
    return h.reshape(batch, seq, d)
```

```python
import functools
import math

import jax
import jax.numpy as jnp
import numpy as np
from jax import lax
from jax.experimental import pallas as pl
from jax.experimental.pallas import tpu as pltpu

F32 = jnp.float32
BF16 = jnp.bfloat16
I32 = jnp.int32

D_MODEL = 1024
DEPTH = 4
N_A_LAYERS = DEPTH // 2

GLA_HEADS = 4
GLA_DK = 128
GLA_DV = 256
GLA_RANK = 16
GLA_GATE_NORM = 16.0
GLA_CHUNK = 64
GLA_HK = GLA_HEADS * GLA_DK
GLA_HV = GLA_HEADS * GLA_DV

FOX_HEADS = 16
FOX_DH = 64
FOX_FD = FOX_HEADS * FOX_DH

N_EXPERTS = 16
N_GROUPS = 4
EPG = 4
D_EXPERT = 512
N_PAIRS = 6
N_CLASSES = N_GROUPS * N_PAIRS
CLASS_PAD = 32

ALPHA = float((2 * DEPTH) ** 0.25)
LN_EPS = 1e-5
RMS_EPS = 1e-6
LOG2E = 1.4426950408889634

LANES = 128
VMEM_LIMIT = 48 * 1024 * 1024

TM_PROJ = 512
TN_PROJ = 512
GLA_ROWS = 256
TM_POST = 512
TM_RANK = 512
TM_ROWS = 512
TM_MOE = 256
ROW_DMA_UNROLL = 8
FOX_TQ = 512


def _cparams(n_axes):
    return pltpu.CompilerParams(dimension_semantics=("arbitrary",) * n_axes,
                                vmem_limit_bytes=VMEM_LIMIT)


def _log_sigmoid(x):
    return jnp.minimum(x, 0.0) - jnp.log(1.0 + jnp.exp(-jnp.abs(x)))


def _sigmoid(x):
    return 1.0 / (1.0 + jnp.exp(-x))


def _layer_norm(z, g, b):
    mu = jnp.mean(z, axis=-1, keepdims=True)
    zc = z - mu
    var = jnp.mean(zc * zc, axis=-1, keepdims=True)
    return zc * lax.rsqrt(var + LN_EPS) * g + b


def _split3(x):
    hi = x.astype(BF16).astype(F32)
    r = x - hi
    mid = r.astype(BF16).astype(F32)
    lo = (r - mid).astype(BF16).astype(F32)
    return hi, mid, lo


def _proj_kernel(x_ref, *refs, n_w):
    xb = x_ref[...].astype(BF16)
    for w_ref, o_ref in zip(refs[:n_w], refs[n_w:]):
        n = o_ref.shape[1]
        tn = min(TN_PROJ, n)
        for c in range(n // tn):
            o_ref[:, c * tn:(c + 1) * tn] = jnp.dot(
                xb, w_ref[:, c * tn:(c + 1) * tn], preferred_element_type=F32).astype(o_ref.dtype)


def _proj(x, ws, out_dtypes):
    t, k = x.shape
    tm = TM_PROJ
    return pl.pallas_call(
        functools.partial(_proj_kernel, n_w=len(ws)),
        grid=(t // tm,),
        in_specs=[pl.BlockSpec((tm, k), lambda i: (i, 0))]
        + [pl.BlockSpec(w.shape, lambda i: (0, 0)) for w in ws],
        out_specs=[pl.BlockSpec((tm, w.shape[1]), lambda i: (i, 0)) for w in ws],
        out_shape=[jax.ShapeDtypeStruct((t, w.shape[1]), dt) for w, dt in zip(ws, out_dtypes)],
        compiler_params=_cparams(1),
        name="proj",
    )(x, *ws)


def _gla_kernel(q_ref, k_ref, v_ref, g_ref, gr_ref, wgk_ref, bgk_ref, ng_ref, o_ref, state_ref):
    c = GLA_CHUNK

    @pl.when(pl.program_id(1) == 0)
    def _():
        state_ref[...] = jnp.zeros_like(state_ref)

    gr = gr_ref[...]
    gr_hi = gr.astype(BF16)
    gr_lo = (gr - gr_hi.astype(F32)).astype(BF16)
    w_hi = wgk_ref[0]
    gkz = (jnp.dot(gr_hi, w_hi, preferred_element_type=F32)
           + jnp.dot(gr_lo, w_hi, preferred_element_type=F32)
           + jnp.dot(gr_hi, wgk_ref[1], preferred_element_type=F32)) + bgk_ref[...]
    gk = _log_sigmoid(gkz) * (1.0 / GLA_GATE_NORM)
    row = lax.broadcasted_iota(I32, (c, c), 0)
    col = lax.broadcasted_iota(I32, (c, c), 1)
    causal = col <= row
    tril = jnp.where(causal, 1.0, 0.0).astype(BF16)
    gk_parts = [part.astype(BF16) for part in _split3(gk)]
    scale = GLA_DK ** -0.5
    ng = ng_ref[...]

    for ci in range(GLA_ROWS // c):
        rs = slice(ci * c, (ci + 1) * c)
        bc = sum(jnp.dot(tril, part[rs], preferred_element_type=F32) for part in gk_parts)
        b_last = bc[c - 1:c, :]
        qf = q_ref[rs, :].astype(F32)
        kf = k_ref[rs, :].astype(F32)
        q_dec = (qf * scale * jnp.exp(bc)).astype(BF16)
        k_inv = (kf * jnp.exp(-bc)).astype(BF16)
        k_end = (kf * jnp.exp(b_last - bc)).astype(BF16)
        dec = jnp.exp(b_last)
        for h in range(GLA_HEADS):
            ks = slice(h * GLA_DK, (h + 1) * GLA_DK)
            vs = slice(h * GLA_DV, (h + 1) * GLA_DV)
            v_h = v_ref[rs, vs]
            attn = lax.dot_general(q_dec[:, ks], k_inv[:, ks], (((1,), (1,)), ((), ())),
                                   preferred_element_type=F32)
            attn = jnp.where(causal, attn, 0.0).astype(BF16)
            st = state_ref[h]
            o = jnp.dot(attn, v_h, preferred_element_type=F32)
            o = o + lax.dot_general(q_dec[:, ks], st.astype(BF16), (((1,), (1,)), ((), ())),
                                    preferred_element_type=F32)
            kv_t = lax.dot_general(v_h, k_end[:, ks], (((0,), (0,)), ((), ())),
                                   preferred_element_type=F32)
            state_ref[h] = st * dec[:, ks] + kv_t
            o = o * lax.rsqrt(jnp.mean(o * o, axis=-1, keepdims=True) + RMS_EPS) * ng
            gate = g_ref[rs, vs].astype(F32)
            o = o * (gate * _sigmoid(gate))
            o_ref[rs, vs] = o.astype(o_ref.dtype)


def _gla_core(qkvg, gr, wgk_pad, bgk, ng, batch, seq):
    t = batch * seq
    r = GLA_ROWS
    nblk = seq // r
    rowmap = lambda b, i: b * nblk + i
    return pl.pallas_call(
        _gla_kernel,
        grid=(batch, nblk),
        in_specs=[
            pl.BlockSpec((r, GLA_HK), lambda b, i: (rowmap(b, i), 0)),
            pl.BlockSpec((r, GLA_HK), lambda b, i: (rowmap(b, i), 1)),
            pl.BlockSpec((r, GLA_HV), lambda b, i: (rowmap(b, i), 1)),
            pl.BlockSpec((r, GLA_HV), lambda b, i: (rowmap(b, i), 2)),
            pl.BlockSpec((r, LANES), lambda b, i: (rowmap(b, i), 0)),
            pl.BlockSpec((2, LANES, GLA_HK), lambda b, i: (0, 0, 0)),
            pl.BlockSpec((1, GLA_HK), lambda b, i: (0, 0)),
            pl.BlockSpec((1, GLA_DV), lambda b, i: (0, 0)),
        ],
        out_specs=pl.BlockSpec((r, GLA_HV), lambda b, i: (rowmap(b, i), 0)),
        out_shape=jax.ShapeDtypeStruct((t, GLA_HV), BF16),
        scratch_shapes=[pltpu.VMEM((GLA_HEADS, GLA_DV, GLA_DK), F32)],
        compiler_params=_cparams(2),
        name="gla_core",
    )(qkvg, qkvg, qkvg, qkvg, gr, wgk_pad, bgk, ng)


def _route_class(logits_t, bias_col):
    mx = jnp.max(logits_t, axis=0, keepdims=True)
    e = jnp.exp(logits_t - mx)
    scores = e / jnp.sum(e, axis=0, keepdims=True)
    sel = scores + bias_col
    rows = [sel[j:j + 1, :] for j in range(N_EXPERTS)]
    best_g = None
    best_s = None
    for g in range(N_GROUPS):
        m = rows[g * EPG:(g + 1) * EPG]
        gs = None
        for a in range(EPG):
            for b in range(a + 1, EPG):
                s = m[a] + m[b]
                gs = s if gs is None else jnp.maximum(gs, s)
        if g == 0:
            best_g = jnp.zeros(gs.shape, I32)
            best_s = gs
        else:
            better = gs > best_s
            best_g = jnp.where(better, g, best_g)
            best_s = jnp.where(better, gs, best_s)
    mem = []
    for j in range(EPG):
        vj = rows[j]
        for g in range(1, N_GROUPS):
            vj = jnp.where(best_g == g, rows[g * EPG + j], vj)
        mem.append(vj)
    i1 = jnp.zeros(best_g.shape, I32)
    b1 = mem[0]
    for j in range(1, EPG):
        better = mem[j] > b1
        i1 = jnp.where(better, j, i1)
        b1 = jnp.where(better, mem[j], b1)
    i2 = jnp.where(i1 == 0, 1, 0).astype(I32)
    b2 = jnp.where(i1 == 0, mem[1], mem[0])
    for j in range(1, EPG):
        better = (mem[j] > b2) & (i1 != j) & (i2 != j)
        i2 = jnp.where(better, j, i2)
        b2 = jnp.where(better, mem[j], b2)
    lo = jnp.minimum(i1, i2)
    hi = jnp.maximum(i1, i2)
    pair = jnp.where(lo == 0, hi - 1, jnp.where(lo == 1, hi + 1, 5))
    return best_g * N_PAIRS + pair


def _router_weight(router_w):
    hi = router_w.astype(BF16)
    lo = (router_w - hi.astype(F32)).astype(BF16)
    pad = jnp.zeros((router_w.shape[0], LANES - 2 * N_EXPERTS), BF16)
    return jnp.concatenate([hi, lo, pad], axis=1)


def _router_logits(x, w2_ref):
    hi = x.astype(BF16)
    lo = (x - hi.astype(F32)).astype(BF16)
    w2 = w2_ref[...]
    a = jnp.dot(hi, w2, preferred_element_type=F32) + jnp.dot(lo, w2, preferred_element_type=F32)
    return a + pltpu.roll(a, LANES - N_EXPERTS, axis=1)


def _post_kernel(o_ref, w_ref, h_ref, g_ref, b_ref, w2_ref, rb_ref, h1_ref, cls_ref):
    half = o_ref.shape[0] // 2
    for part in range(2):
        rs = slice(part * half, (part + 1) * half)
        mix = jnp.dot(o_ref[rs, :], w_ref[...], preferred_element_type=F32)
        h1 = _layer_norm(ALPHA * h_ref[rs, :] + mix, g_ref[...], b_ref[...])
        h1_ref[rs, :] = h1
        logits_t = _router_logits(h1, w2_ref).T[:N_EXPERTS, :]
        cls_ref[:, rs] = _route_class(logits_t, rb_ref[...])


def _post_mixer(o, w_out, h, ln_g, ln_b, w2, rb_col):
    t, d = h.shape
    tm = TM_POST
    kdim = o.shape[1]
    return pl.pallas_call(
        _post_kernel,
        grid=(t // tm,),
        in_specs=[
            pl.BlockSpec((tm, kdim), lambda i: (i, 0)),
            pl.BlockSpec((kdim, d), lambda i: (0, 0)),
            pl.BlockSpec((tm, d), lambda i: (i, 0)),
            pl.BlockSpec((1, d), lambda i: (0, 0)),
            pl.BlockSpec((1, d), lambda i: (0, 0)),
            pl.BlockSpec((d, LANES), lambda i: (0, 0)),
            pl.BlockSpec((N_EXPERTS, 1), lambda i: (0, 0)),
        ],
        out_specs=[pl.BlockSpec((tm, d), lambda i: (i, 0)),
                   pl.BlockSpec((1, tm), lambda i: (0, i))],
        out_shape=[jax.ShapeDtypeStruct((t, d), F32), jax.ShapeDtypeStruct((1, t), I32)],
        compiler_params=_cparams(1),
        name="post_mixer",
    )(o, w_out, h, ln_g, ln_b, w2, rb_col)


def _rank_kernel(cls_ref, pos_ref, toff_ref, cnt_ref, carry_ref, off_ref, tri_ref):
    phase = pl.program_id(0)
    i = pl.program_id(1)
    tm = cls_ref.shape[1]
    onehot = (lax.broadcasted_iota(I32, (CLASS_PAD, tm), 0) == cls_ref[...]).astype(F32)
    tile_count = jnp.sum(onehot, axis=1, keepdims=True)

    @pl.when((phase == 0) & (i == 0))
    def _():
        cnt_ref[...] = jnp.zeros_like(cnt_ref)
        r = lax.broadcasted_iota(I32, (tm, tm), 0)
        cc = lax.broadcasted_iota(I32, (tm, tm), 1)
        tri_ref[...] = (r <= cc).astype(BF16)

    @pl.when(phase == 0)
    def _():
        cnt_ref[...] += jnp.broadcast_to(tile_count, cnt_ref.shape)

    @pl.when((phase == 1) & (i == 0))
    def _():
        ntile = jnp.floor((cnt_ref[...] + (TM_MOE - 1)) * (1.0 / TM_MOE))
        r = lax.broadcasted_iota(I32, (CLASS_PAD, CLASS_PAD), 0)
        cc = lax.broadcasted_iota(I32, (CLASS_PAD, CLASS_PAD), 1)
        strict = (cc < r).astype(BF16)
        first_tile = jnp.dot(strict, ntile.astype(BF16), preferred_element_type=F32)
        toff_ref[...] = first_tile.astype(I32)
        off_ref[...] = first_tile * float(TM_MOE)
        carry_ref[...] = jnp.zeros_like(carry_ref)

    @pl.when(phase == 1)
    def _():
        prefix = jnp.dot(onehot.astype(BF16), tri_ref[...], preferred_element_type=F32)
        base = off_ref[:, 0:1] + carry_ref[:, 0:1] - 1.0
        posf = jnp.sum(onehot * (prefix + base), axis=0, keepdims=True)
        pos_ref[...] = posf.astype(I32)
        carry_ref[...] += jnp.broadcast_to(tile_count, carry_ref.shape)


def _rank(cls):
    t = cls.shape[1]
    tm = TM_RANK
    return pl.pallas_call(
        _rank_kernel,
        grid=(2, t // tm),
        in_specs=[pl.BlockSpec((1, tm), lambda p, i: (0, i))],
        out_specs=[pl.BlockSpec((1, tm), lambda p, i: (0, i * p)),
                   pl.BlockSpec((CLASS_PAD, LANES), lambda p, i: (0, 0))],
        out_shape=[jax.ShapeDtypeStruct((1, t), I32),
                   jax.ShapeDtypeStruct((CLASS_PAD, LANES), I32)],
        scratch_shapes=[pltpu.VMEM((CLASS_PAD, LANES), F32),
                        pltpu.VMEM((CLASS_PAD, LANES), F32),
                        pltpu.VMEM((CLASS_PAD, LANES), F32),
                        pltpu.VMEM((tm, tm), BF16)],
        compiler_params=_cparams(2),
        name="rank",
    )(cls)


def _dispatch_kernel(pos_ref, h_ref, xs_in_ref, xs_ref, sem):
    del xs_in_ref
    tm = h_ref.shape[0]
    base = pl.program_id(0) * tm

    def issue(r, carry):
        p = pos_ref[base + r]
        pltpu.make_async_copy(h_ref.at[pl.ds(r, 1), :], xs_ref.at[pl.ds(p, 1), :], sem).start()
        return carry

    lax.fori_loop(0, tm, issue, 0, unroll=ROW_DMA_UNROLL)
    pltpu.make_async_copy(h_ref, xs_ref.at[pl.ds(0, tm), :], sem).wait()


def _dispatch(pos, h1, n_sorted):
    t, d = h1.shape
    tm = TM_ROWS
    xs0 = jnp.zeros((n_sorted, d), F32)
    return pl.pallas_call(
        _dispatch_kernel,
        grid_spec=pltpu.PrefetchScalarGridSpec(
            num_scalar_prefetch=1,
            grid=(t // tm,),
            in_specs=[pl.BlockSpec((tm, d), lambda i, pos: (i, 0)),
                      pl.BlockSpec(memory_space=pl.ANY)],
            out_specs=pl.BlockSpec(memory_space=pl.ANY),
            scratch_shapes=[pltpu.SemaphoreType.DMA(())],
        ),
        out_shape=jax.ShapeDtypeStruct((n_sorted, d), F32),
        input_output_aliases={2: 0},
        compiler_params=_cparams(1),
        name="dispatch",
    )(pos, h1, xs0)


def _combine_kernel(pos_ref, h_ref, ys_ref, g_ref, b_ref, o_ref, buf_ref, sem):
    tm = h_ref.shape[0]
    base = pl.program_id(0) * tm

    def issue(r, carry):
        p = pos_ref[base + r]
        pltpu.make_async_copy(ys_ref.at[pl.ds(p, 1), :], buf_ref.at[pl.ds(r, 1), :], sem).start()
        return carry

    lax.fori_loop(0, tm, issue, 0, unroll=ROW_DMA_UNROLL)
    pltpu.make_async_copy(ys_ref.at[pl.ds(0, tm), :], buf_ref, sem).wait()
    o_ref[...] = _layer_norm(ALPHA * h_ref[...] + buf_ref[...], g_ref[...], b_ref[...])


def _combine(pos, h1, ys, ln_g, ln_b):
    t, d = h1.shape
    tm = TM_ROWS
    return pl.pallas_call(
        _combine_kernel,
        grid_spec=pltpu.PrefetchScalarGridSpec(
            num_scalar_prefetch=1,
            grid=(t // tm,),
            in_specs=[pl.BlockSpec((tm, d), lambda i, pos: (i, 0)),
                      pl.BlockSpec(memory_space=pl.ANY),
                      pl.BlockSpec((1, d), lambda i, pos: (0, 0)),
                      pl.BlockSpec((1, d), lambda i, pos: (0, 0))],
            out_specs=pl.BlockSpec((tm, d), lambda i, pos: (i, 0)),
            scratch_shapes=[pltpu.VMEM((tm, d), F32), pltpu.SemaphoreType.DMA(())],
        ),
        out_shape=jax.ShapeDtypeStruct((t, d), F32),
        compiler_params=_cparams(1),
        name="combine",
    )(pos, h1, ys, ln_g, ln_b)


def _moe_kernel(ea_ref, eb_ref, valid_ref, xs_ref, w2_ref, wga_ref, wua_ref, wda_ref,
                wgb_ref, wub_ref, wdb_ref, ys_ref, wbf_up_ref, wbf_dn_ref):
    i = pl.program_id(0)
    prev = jnp.maximum(i - 1, 0)
    fresh = (i == 0) | (ea_ref[i] != ea_ref[prev]) | (eb_ref[i] != eb_ref[prev])

    @pl.when(fresh)
    def _():
        for slot, w_ref in enumerate((wga_ref, wua_ref, wgb_ref, wub_ref)):
            wbf_up_ref[slot] = w_ref[...].astype(BF16)
        for slot, w_ref in enumerate((wda_ref, wdb_ref)):
            wbf_dn_ref[slot] = w_ref[...].astype(BF16)

    @pl.when(valid_ref[i] == 0)
    def _():
        ys_ref[...] = jnp.zeros_like(ys_ref)

    @pl.when(valid_ref[i] != 0)
    def _():
        x = xs_ref[...]
        logits = _router_logits(x, w2_ref)
        lane = lax.broadcasted_iota(I32, logits.shape, 1)
        logits = jnp.where(lane < N_EXPERTS, logits, -jnp.inf)
        e = jnp.exp(logits - jnp.max(logits, axis=1, keepdims=True))
        scores = e / jnp.sum(e, axis=1, keepdims=True)
        sa = jnp.sum(jnp.where(lane == ea_ref[i], scores, 0.0), axis=1, keepdims=True)
        sb = jnp.sum(jnp.where(lane == eb_ref[i], scores, 0.0), axis=1, keepdims=True)
        tot = sa + sb
        xb = x.astype(BF16)

        def expert(slot, gate):
            a = jnp.dot(xb, wbf_up_ref[2 * slot], preferred_element_type=F32)
            u = jnp.dot(xb, wbf_up_ref[2 * slot + 1], preferred_element_type=F32)
            hid = (a * _sigmoid(a)) * u * gate
            return jnp.dot(hid.astype(BF16), wbf_dn_ref[slot], preferred_element_type=F32)

        ys_ref[...] = expert(0, sa / tot) + expert(1, sb / tot)


def _moe(ea, eb, valid, xs, w2, wg, wu, wd, layer):
    n_sorted, d = xs.shape
    tm = TM_MOE
    f = wg.shape[3]
    idx_a = lambda i, ea, eb, va: (layer, ea[i], 0, 0)
    idx_b = lambda i, ea, eb, va: (layer, eb[i], 0, 0)
    up_a = pl.BlockSpec((None, None, d, f), idx_a)
    up_b = pl.BlockSpec((None, None, d, f), idx_b)
    dn_a = pl.BlockSpec((None, None, f, d), idx_a)
    dn_b = pl.BlockSpec((None, None, f, d), idx_b)
    return pl.pallas_call(
        _moe_kernel,
        grid_spec=pltpu.PrefetchScalarGridSpec(
            num_scalar_prefetch=3,
            grid=(n_sorted // tm,),
            in_specs=[pl.BlockSpec((tm, d), lambda i, ea, eb, va: (i, 0)),
                      pl.BlockSpec((d, LANES), lambda i, ea, eb, va: (0, 0)),
                      up_a, up_a, dn_a, up_b, up_b, dn_b],
            out_specs=pl.BlockSpec((tm, d), lambda i, ea, eb, va: (i, 0)),
            scratch_shapes=[pltpu.VMEM((4, d, f), BF16), pltpu.VMEM((2, f, d), BF16)],
        ),
        out_shape=jax.ShapeDtypeStruct((n_sorted, d), F32),
        compiler_params=_cparams(1),
        name="moe",
    )(ea, eb, valid, xs, w2, wg, wu, wd, wg, wu, wd)


_PAIR_LO = (0, 0, 0, 1, 1, 2)
_PAIR_HI = (1, 2, 3, 2, 3, 3)


def _tile_plan(toff, n_tiles):
    first = toff[:N_CLASSES, 0]
    tiles = jnp.arange(n_tiles, dtype=I32)
    cls = jnp.sum((first[None, :] <= tiles[:, None]).astype(I32), axis=1) - 1
    group = cls // N_PAIRS
    pair = cls % N_PAIRS
    lo = jnp.asarray(_PAIR_LO, I32)[pair]
    hi = jnp.asarray(_PAIR_HI, I32)[pair]
    total = toff[N_CLASSES, 0]
    valid = (tiles < total).astype(I32)
    return group * EPG + lo, group * EPG + hi, valid


def _moe_block(h1, cls, w2, wg, wu, wd, layer, ln_g, ln_b):
    t, _ = h1.shape
    n_tiles = t // TM_MOE + N_CLASSES
    pos2d, toff = _rank(cls)
    pos = pos2d.reshape(t)
    ea, eb, valid = _tile_plan(toff, n_tiles)
    xs = _dispatch(pos, h1, n_tiles * TM_MOE)
    ys = _moe(ea, eb, valid, xs, w2, wg, wu, wd, layer)
    return _combine(pos, h1, ys, ln_g, ln_b)


DEC_LANES = 6


def _decay_placement():
    pq = np.zeros((LANES, FOX_FD), np.float32)
    pk = np.zeros((LANES, FOX_FD), np.float32)
    oq = np.zeros((1, FOX_FD), np.float32)
    ok = np.zeros((1, FOX_FD), np.float32)
    for h in range(FOX_HEADS):
        base = (h // 2) * LANES + (h % 2) * DEC_LANES
        for part in range(3):
            pq[part * FOX_HEADS + h, base + part] = 1.0
            pk[part * FOX_HEADS + h, base + 3 + part] = -1.0
            oq[0, base + 3 + part] = 1.0
            ok[0, base + part] = 1.0
    return pq, pk, oq, ok


def _kv_kernel(x_ref, wk_ref, wv_ref, wf_ref, fb_ref, pq_ref, pk_ref, oq_ref, ok_ref,
               k_ref, v_ref, cq_ref, ck_ref, carry_ref, tril_ref, *, tiles_per_seq):
    i = pl.program_id(0)
    tm = x_ref.shape[0]

    @pl.when(i == 0)
    def _():
        r = lax.broadcasted_iota(I32, (tm, tm), 0)
        cc = lax.broadcasted_iota(I32, (tm, tm), 1)
        tril_ref[...] = (cc <= r).astype(BF16)

    @pl.when(i % tiles_per_seq == 0)
    def _():
        carry_ref[...] = jnp.zeros_like(carry_ref)

    xb = x_ref[...].astype(BF16)
    for w_ref, o_ref in ((wk_ref, k_ref), (wv_ref, v_ref)):
        n = o_ref.shape[1]
        for c in range(n // TN_PROJ):
            cs = slice(c * TN_PROJ, (c + 1) * TN_PROJ)
            o_ref[:, cs] = jnp.dot(xb, w_ref[:, cs], preferred_element_type=F32).astype(o_ref.dtype)
    fl = jnp.dot(xb, wf_ref[...], preferred_element_type=F32) + fb_ref[...]
    lane = lax.broadcasted_iota(I32, fl.shape, 1)
    log_f = jnp.where(lane < FOX_HEADS, _log_sigmoid(fl), 0.0)
    tril = tril_ref[...]
    cum = carry_ref[...]
    for part in _split3(log_f):
        cum = cum + jnp.dot(tril, part.astype(BF16), preferred_element_type=F32)
    carry_ref[...] = cum[tm - 1:tm, :]
    hi, mid, lo = _split3(cum * LOG2E)
    packed = hi + pltpu.roll(mid, FOX_HEADS, axis=1) + pltpu.roll(lo, 2 * FOX_HEADS, axis=1)
    packed = packed.astype(BF16)
    cq_ref[...] = (jnp.dot(packed, pq_ref[...], preferred_element_type=F32)
                   + oq_ref[...]).astype(cq_ref.dtype)
    ck_ref[...] = (jnp.dot(packed, pk_ref[...], preferred_element_type=F32)
                   + ok_ref[...]).astype(ck_ref.dtype)


def _fox_kv(h, wk, wv, wf_pad, fb_pad, seq):
    t, d = h.shape
    tm = TM_PROJ
    pq, pk, oq, ok = _decay_placement()
    full = lambda a: pl.BlockSpec(a.shape, lambda i: (0, 0))
    consts = [jnp.asarray(pq, BF16), jnp.asarray(pk, BF16), jnp.asarray(oq), jnp.asarray(ok)]
    row_out = pl.BlockSpec((tm, FOX_FD), lambda i: (i, 0))
    return pl.pallas_call(
        functools.partial(_kv_kernel, tiles_per_seq=seq // tm),
        grid=(t // tm,),
        in_specs=[pl.BlockSpec((tm, d), lambda i: (i, 0)), full(wk), full(wv), full(wf_pad),
                  full(fb_pad)] + [full(c) for c in consts],
        out_specs=[row_out, row_out, row_out, row_out],
        out_shape=[jax.ShapeDtypeStruct((t, FOX_FD), BF16)] * 4,
        scratch_shapes=[pltpu.VMEM((1, LANES), F32), pltpu.VMEM((tm, tm), BF16)],
        compiler_params=_cparams(1),
        name="fox_kv",
    )(h, wk, wv, wf_pad, fb_pad, *consts)


def _fox_kernel(q_ref, g_ref, cq_ref, k_ref, ck_ref, v_ref, o_ref,
                s_ref, mb_ref, m_ref, acc_ref):
    qi = pl.program_id(2)
    tq = q_ref.shape[0]
    tk = tq
    lane = lax.broadcasted_iota(I32, (tq, LANES), 1)
    klane = lax.broadcasted_iota(I32, (tk, LANES), 1)
    q = q_ref[...]
    cq = cq_ref[...]
    q_augs = []
    for hh in range(2):
        in_head = (lane >= hh * FOX_DH) & (lane < (hh + 1) * FOX_DH)
        in_dec = (lane >= hh * DEC_LANES) & (lane < (hh + 1) * DEC_LANES)
        q_augs.append(jnp.concatenate([jnp.where(in_head, q, jnp.zeros_like(q)),
                                       jnp.where(in_dec, cq, jnp.zeros_like(cq))], axis=1))
    one_lane = (FOX_DH, 0)
    v_keep = [klane < FOX_DH, klane >= FOX_DH]

    def scores(j):
        ks = pl.ds(pl.multiple_of(j * tk, tk), tk)
        k_aug = jnp.concatenate([k_ref[ks, :], ck_ref[ks, :]], axis=1)
        for hh in range(2):
            s = lax.dot_general(q_augs[hh], k_aug, (((1,), (1,)), ((), ())),
                                preferred_element_type=F32)
            s_ref[hh] = s
            mb_ref[hh] = jnp.broadcast_to(jnp.max(s, axis=1, keepdims=True), (tq, LANES))

    def absorb(j, masked):
        ks = pl.ds(pl.multiple_of(j * tk, tk), tk)
        v = v_ref[ks, :]
        for hh in range(2):
            ones = (klane == one_lane[hh]).astype(BF16)
            v_aug = jnp.where(v_keep[hh], v, ones)
            if masked:
                r = lax.broadcasted_iota(I32, (tq, tk), 0)
                cc = lax.broadcasted_iota(I32, (tq, tk), 1)
                s = jnp.where(cc <= r, s_ref[hh], -jnp.inf)
                mb = jnp.broadcast_to(jnp.max(s, axis=1, keepdims=True), (tq, LANES))
            else:
                s = s_ref[hh]
                mb = mb_ref[hh]
            m_old = m_ref[hh]
            m_new = jnp.maximum(m_old, mb)
            m_ref[hh] = m_new
            alpha = jnp.exp2(m_old - m_new)
            p = jnp.exp2((s - jnp.concatenate([m_new] * (tk // LANES), axis=1)).astype(BF16))
            acc_ref[hh] = alpha * acc_ref[hh] + jnp.dot(p, v_aug, preferred_element_type=F32)

    m_ref[...] = jnp.full(m_ref.shape, -jnp.inf, F32)
    acc_ref[...] = jnp.zeros(acc_ref.shape, F32)
    scores(0)

    def body(i, carry):
        for j in (2 * i, 2 * i + 1):
            absorb(j, False)
            scores(j + 1)
        return carry

    lax.fori_loop(0, qi // 2, body, 0)

    @pl.when(qi % 2 == 1)
    def _():
        absorb(qi - 1, False)
        scores(qi)

    absorb(qi, True)
    acc0 = acc_ref[0]
    acc1 = acc_ref[1]
    o = jnp.where(lane < FOX_DH, acc0 / acc0[:, FOX_DH:FOX_DH + 1], acc1 / acc1[:, 0:1])
    o_ref[...] = (o * _sigmoid(g_ref[...].astype(F32))).astype(o_ref.dtype)


def _fox_attn(qg, k, v, cq, ck, batch, seq):
    t = batch * seq
    tq = FOX_TQ
    nq = seq // tq
    n_pairs = FOX_FD // LANES
    q_blk = lambda b, p, i: (b * nq + i, p)
    kv_blk = lambda b, p, i: (b, p)
    return pl.pallas_call(
        _fox_kernel,
        grid=(batch, n_pairs, nq),
        in_specs=[
            pl.BlockSpec((tq, LANES), q_blk),
            pl.BlockSpec((tq, LANES), lambda b, p, i: (b * nq + i, n_pairs + p)),
            pl.BlockSpec((tq, LANES), q_blk),
            pl.BlockSpec((seq, LANES), kv_blk),
            pl.BlockSpec((seq, LANES), kv_blk),
            pl.BlockSpec((seq, LANES), kv_blk),
        ],
        out_specs=pl.BlockSpec((tq, LANES), q_blk),
        out_shape=jax.ShapeDtypeStruct((t, FOX_FD), BF16),
        scratch_shapes=[pltpu.VMEM((2, tq, tq), F32),
                        pltpu.VMEM((2, tq, LANES), F32),
                        pltpu.VMEM((2, tq, LANES), F32),
                        pltpu.VMEM((2, tq, LANES), F32)],
        compiler_params=_cparams(3),
        name="fox_attn",
    )(qg, qg, cq, k, ck, v)


def kernel(x, gla_w_in, gla_w_gk, gla_b_gk, gla_norm_g, gla_w_out, kv_w, forget_bias, fox_w_qg,
           fox_w_out, router_w, router_bias, moe_w_gate, moe_w_up, moe_w_down, ln_g, ln_b):
    batch, seq, d = x.shape
    t = batch * seq
    h = x.reshape(t, d)
    w2 = _router_weight(router_w)
    rb_col = router_bias.reshape(N_EXPERTS, 1)
    n_main = 2 * GLA_HK + 2 * GLA_HV
    k_sh = v_sh = cq_sh = ck_sh = None
    for layer in range(DEPTH):
        if layer < N_A_LAYERS:
            w_in = gla_w_in[layer]
            w_main = w_in[:, :n_main].astype(BF16)
            w_gr = jnp.pad(w_in[:, n_main:], ((0, 0), (0, LANES - GLA_RANK))).astype(BF16)
            qkvg, gr = _proj(h, [w_main, w_gr], [BF16, F32])
            wgk = jnp.pad(gla_w_gk[layer], ((0, LANES - GLA_RANK), (0, 0)))
            wgk_hi = wgk.astype(BF16)
            wgk_pad = jnp.stack([wgk_hi, (wgk - wgk_hi.astype(F32)).astype(BF16)])
            o = _gla_core(qkvg, gr, wgk_pad, gla_b_gk[layer].reshape(1, GLA_HK),
                          gla_norm_g[layer].reshape(1, GLA_DV), batch, seq)
            w_out = gla_w_out[layer].astype(BF16)
        else:
            j = layer - N_A_LAYERS
            w_qg = fox_w_qg[j]
            w_qg = jnp.concatenate([w_qg[:, :FOX_FD] * (FOX_DH ** -0.5 * LOG2E), w_qg[:, FOX_FD:]], axis=1)
            (qg,) = _proj(h, [w_qg.astype(BF16)], [BF16])
            o = _fox_attn(qg, k_sh, v_sh, cq_sh, ck_sh, batch, seq)
            w_out = fox_w_out[j].astype(BF16)
        h1, cls = _post_mixer(o, w_out, h, ln_g[layer, 0].reshape(1, d), ln_b[layer, 0].reshape(1, d),
                              w2, rb_col)
        h = _moe_block(h1, cls, w2, moe_w_gate, moe_w_up, moe_w_down, layer,
                       ln_g[layer, 1].reshape(1, d), ln_b[layer, 1].reshape(1, d))
        if layer == N_A_LAYERS - 1:
            wf_pad = jnp.pad(kv_w[:, 2 * FOX_FD:], ((0, 0), (0, LANES - FOX_HEADS))).astype(BF16)
            fb_pad = jnp.pad(forget_bias, (0, LANES - FOX_HEADS)).reshape(1, LANES)
            k_sh, v_sh, cq_sh, ck_sh = _fox_kv(h, kv_w[:, :FOX_FD].astype(BF16),
                                            kv_w[:, FOX_FD:2 * FOX_FD].astype(BF16), wf_pad, fb_pad, seq)
    return h.reshape(batch, seq, d)
```

```python
import functools
import math

import jax
import jax.numpy as jnp
import numpy as np
from jax import lax
from jax.experimental import pallas as pl
from jax.experimental.pallas import tpu as pltpu

F32 = jnp.float32
BF16 = jnp.bfloat16
I32 = jnp.int32

D_MODEL = 1024
DEPTH = 4
N_A_LAYERS = DEPTH // 2

GLA_HEADS = 4
GLA_DK = 128
GLA_DV = 256
GLA_RANK = 16
GLA_GATE_NORM = 16.0
GLA_CHUNK = 64
GLA_HK = GLA_HEADS * GLA_DK
GLA_HV = GLA_HEADS * GLA_DV

FOX_HEADS = 16
FOX_DH = 64
FOX_FD = FOX_HEADS * FOX_DH

N_EXPERTS = 16
N_GROUPS = 4
EPG = 4
D_EXPERT = 512
N_PAIRS = 6
N_CLASSES = N_GROUPS * N_PAIRS
CLASS_PAD = 32

ALPHA = float((2 * DEPTH) ** 0.25)
LN_EPS = 1e-5
RMS_EPS = 1e-6
LOG2E = 1.4426950408889634

LANES = 128
VMEM_LIMIT = 48 * 1024 * 1024

TM_PROJ = 512
TN_PROJ = 512
GLA_ROWS = 256
TM_POST = 512
TM_RANK = 512
TM_ROWS = 512
TM_MOE = 256
FOX_TQ = 512


def _cparams(n_axes):
    return pltpu.CompilerParams(dimension_semantics=("arbitrary",) * n_axes,
                                vmem_limit_bytes=VMEM_LIMIT)


def _log_sigmoid(x):
    return jnp.minimum(x, 0.0) - jnp.log(1.0 + jnp.exp(-jnp.abs(x)))


def _sigmoid(x):
    return 1.0 / (1.0 + jnp.exp(-x))


def _layer_norm(z, g, b):
    mu = jnp.mean(z, axis=-1, keepdims=True)
    zc = z - mu
    var = jnp.mean(zc * zc, axis=-1, keepdims=True)
    return zc * lax.rsqrt(var + LN_EPS) * g + b


def _split3(x):
    hi = x.astype(BF16).astype(F32)
    r = x - hi
    mid = r.astype(BF16).astype(F32)
    lo = (r - mid).astype(BF16).astype(F32)
    return hi, mid, lo


def _proj_kernel(x_ref, *refs, n_w):
    xb = x_ref[...].astype(BF16)
    for w_ref, o_ref in zip(refs[:n_w], refs[n_w:]):
        n = o_ref.shape[1]
        tn = min(TN_PROJ, n)
        for c in range(n // tn):
            o_ref[:, c * tn:(c + 1) * tn] = jnp.dot(
                xb, w_ref[:, c * tn:(c + 1) * tn], preferred_element_type=F32).astype(o_ref.dtype)


def _proj(x, ws, out_dtypes):
    t, k = x.shape
    tm = TM_PROJ
    return pl.pallas_call(
        functools.partial(_proj_kernel, n_w=len(ws)),
        grid=(t // tm,),
        in_specs=[pl.BlockSpec((tm, k), lambda i: (i, 0))]
        + [pl.BlockSpec(w.shape, lambda i: (0, 0)) for w in ws],
        out_specs=[pl.BlockSpec((tm, w.shape[1]), lambda i: (i, 0)) for w in ws],
        out_shape=[jax.ShapeDtypeStruct((t, w.shape[1]), dt) for w, dt in zip(ws, out_dtypes)],
        compiler_params=_cparams(1),
        name="proj",
    )(x, *ws)


def _gla_kernel(q_ref, k_ref, v_ref, g_ref, gr_ref, wgk_ref, bgk_ref, ng_ref, o_ref, state_ref):
    c = GLA_CHUNK

    @pl.when(pl.program_id(1) == 0)
    def _():
        state_ref[...] = jnp.zeros_like(state_ref)

    gr = gr_ref[...]
    gr_hi = gr.astype(BF16)
    gr_lo = (gr - gr_hi.astype(F32)).astype(BF16)
    w_hi = wgk_ref[0]
    gkz = (jnp.dot(gr_hi, w_hi, preferred_element_type=F32)
           + jnp.dot(gr_lo, w_hi, preferred_element_type=F32)
           + jnp.dot(gr_hi, wgk_ref[1], preferred_element_type=F32)) + bgk_ref[...]
    gk = _log_sigmoid(gkz) * (1.0 / GLA_GATE_NORM)
    row = lax.broadcasted_iota(I32, (c, c), 0)
    col = lax.broadcasted_iota(I32, (c, c), 1)
    causal = col <= row
    tril = jnp.where(causal, 1.0, 0.0).astype(BF16)
    gk_parts = [part.astype(BF16) for part in _split3(gk)]
    scale = GLA_DK ** -0.5
    ng = ng_ref[...]

    for ci in range(GLA_ROWS // c):
        rs = slice(ci * c, (ci + 1) * c)
        bc = sum(jnp.dot(tril, part[rs], preferred_element_type=F32) for part in gk_parts)
        b_last = bc[c - 1:c, :]
        qf = q_ref[rs, :].astype(F32)
        kf = k_ref[rs, :].astype(F32)
        q_dec = (qf * scale * jnp.exp(bc)).astype(BF16)
        k_inv = (kf * jnp.exp(-bc)).astype(BF16)
        k_end = (kf * jnp.exp(b_last - bc)).astype(BF16)
        dec = jnp.exp(b_last)
        for h in range(GLA_HEADS):
            ks = slice(h * GLA_DK, (h + 1) * GLA_DK)
            vs = slice(h * GLA_DV, (h + 1) * GLA_DV)
            v_h = v_ref[rs, vs]
            attn = lax.dot_general(q_dec[:, ks], k_inv[:, ks], (((1,), (1,)), ((), ())),
                                   preferred_element_type=F32)
            attn = jnp.where(causal, attn, 0.0).astype(BF16)
            st = state_ref[h]
            o = jnp.dot(attn, v_h, preferred_element_type=F32)
            o = o + lax.dot_general(q_dec[:, ks], st.astype(BF16), (((1,), (1,)), ((), ())),
                                    preferred_element_type=F32)
            kv_t = lax.dot_general(v_h, k_end[:, ks], (((0,), (0,)), ((), ())),
                                   preferred_element_type=F32)
            state_ref[h] = st * dec[:, ks] + kv_t
            o = o * lax.rsqrt(jnp.mean(o * o, axis=-1, keepdims=True) + RMS_EPS) * ng
            gate = g_ref[rs, vs].astype(F32)
            o = o * (gate * _sigmoid(gate))
            o_ref[rs, vs] = o.astype(o_ref.dtype)


def _gla_core(qkvg, gr, wgk_pad, bgk, ng, batch, seq):
    t = batch * seq
    r = GLA_ROWS
    nblk = seq // r
    rowmap = lambda b, i: b * nblk + i
    return pl.pallas_call(
        _gla_kernel,
        grid=(batch, nblk),
        in_specs=[
            pl.BlockSpec((r, GLA_HK), lambda b, i: (rowmap(b, i), 0)),
            pl.BlockSpec((r, GLA_HK), lambda b, i: (rowmap(b, i), 1)),
            pl.BlockSpec((r, GLA_HV), lambda b, i: (rowmap(b, i), 1)),
            pl.BlockSpec((r, GLA_HV), lambda b, i: (rowmap(b, i), 2)),
            pl.BlockSpec((r, LANES), lambda b, i: (rowmap(b, i), 0)),
            pl.BlockSpec((2, LANES, GLA_HK), lambda b, i: (0, 0, 0)),
            pl.BlockSpec((1, GLA_HK), lambda b, i: (0, 0)),
            pl.BlockSpec((1, GLA_DV), lambda b, i: (0, 0)),
        ],
        out_specs=pl.BlockSpec((r, GLA_HV), lambda b, i: (rowmap(b, i), 0)),
        out_shape=jax.ShapeDtypeStruct((t, GLA_HV), BF16),
        scratch_shapes=[pltpu.VMEM((GLA_HEADS, GLA_DV, GLA_DK), F32)],
        compiler_params=_cparams(2),
        name="gla_core",
    )(qkvg, qkvg, qkvg, qkvg, gr, wgk_pad, bgk, ng)


def _route_class(logits_t, bias_col):
    mx = jnp.max(logits_t, axis=0, keepdims=True)
    e = jnp.exp(logits_t - mx)
    scores = e / jnp.sum(e, axis=0, keepdims=True)
    sel = scores + bias_col
    rows = [sel[j:j + 1, :] for j in range(N_EXPERTS)]
    best_g = None
    best_s = None
    for g in range(N_GROUPS):
        m = rows[g * EPG:(g + 1) * EPG]
        gs = None
        for a in range(EPG):
            for b in range(a + 1, EPG):
                s = m[a] + m[b]
                gs = s if gs is None else jnp.maximum(gs, s)
        if g == 0:
            best_g = jnp.zeros(gs.shape, I32)
            best_s = gs
        else:
            better = gs > best_s
            best_g = jnp.where(better, g, best_g)
            best_s = jnp.where(better, gs, best_s)
    mem = []
    for j in range(EPG):
        vj = rows[j]
        for g in range(1, N_GROUPS):
            vj = jnp.where(best_g == g, rows[g * EPG + j], vj)
        mem.append(vj)
    i1 = jnp.zeros(best_g.shape, I32)
    b1 = mem[0]
    for j in range(1, EPG):
        better = mem[j] > b1
        i1 = jnp.where(better, j, i1)
        b1 = jnp.where(better, mem[j], b1)
    i2 = jnp.where(i1 == 0, 1, 0).astype(I32)
    b2 = jnp.where(i1 == 0, mem[1], mem[0])
    for j in range(1, EPG):
        better = (mem[j] > b2) & (i1 != j) & (i2 != j)
        i2 = jnp.where(better, j, i2)
        b2 = jnp.where(better, mem[j], b2)
    lo = jnp.minimum(i1, i2)
    hi = jnp.maximum(i1, i2)
    pair = jnp.where(lo == 0, hi - 1, jnp.where(lo == 1, hi + 1, 5))
    return best_g * N_PAIRS + pair


def _router_weight(router_w):
    hi = router_w.astype(BF16)
    lo = (router_w - hi.astype(F32)).astype(BF16)
    pad = jnp.zeros((router_w.shape[0], LANES - 2 * N_EXPERTS), BF16)
    return jnp.concatenate([hi, lo, pad], axis=1)


def _router_logits(x, w2_ref):
    hi = x.astype(BF16)
    lo = (x - hi.astype(F32)).astype(BF16)
    w2 = w2_ref[...]
    a = jnp.dot(hi, w2, preferred_element_type=F32) + jnp.dot(lo, w2, preferred_element_type=F32)
    return a + pltpu.roll(a, LANES - N_EXPERTS, axis=1)


def _post_kernel(o_ref, w_ref, h_ref, g_ref, b_ref, w2_ref, rb_ref, h1_ref, cls_ref):
    half = h_ref.shape[0] // 2
    for part in range(2):
        rs = slice(part * half, (part + 1) * half)
        mix = jnp.dot(o_ref[rs, :], w_ref[...], preferred_element_type=F32)
        h1 = _layer_norm(ALPHA * h_ref[rs, :] + mix, g_ref[...], b_ref[...])
        h1_ref[rs, :] = h1
        logits_t = _router_logits(h1, w2_ref).T[:N_EXPERTS, :]
        cls_ref[:, rs] = _route_class(logits_t, rb_ref[...])


def _post_mixer(o4, o_map, w_out, h, ln_g, ln_b, w2, rb_col):
    t, d = h.shape
    tm = TM_POST
    kdim = o4.shape[3]
    return pl.pallas_call(
        _post_kernel,
        grid=(t // tm,),
        in_specs=[
            pl.BlockSpec((None, None, tm, kdim), lambda i: (*o_map(i), 0, 0)),
            pl.BlockSpec((kdim, d), lambda i: (0, 0)),
            pl.BlockSpec((tm, d), lambda i: (i, 0)),
            pl.BlockSpec((1, d), lambda i: (0, 0)),
            pl.BlockSpec((1, d), lambda i: (0, 0)),
            pl.BlockSpec((d, LANES), lambda i: (0, 0)),
            pl.BlockSpec((N_EXPERTS, 1), lambda i: (0, 0)),
        ],
        out_specs=[pl.BlockSpec((tm, d), lambda i: (i, 0)),
                   pl.BlockSpec((1, tm), lambda i: (0, i))],
        out_shape=[jax.ShapeDtypeStruct((t, d), F32), jax.ShapeDtypeStruct((1, t), I32)],
        compiler_params=_cparams(1),
        name="post_mixer",
    )(o4, w_out, h, ln_g, ln_b, w2, rb_col)


def _rank_kernel(cls_ref, pos_ref, toff_ref, cnt_ref, carry_ref, off_ref, tri_ref):
    phase = pl.program_id(0)
    i = pl.program_id(1)
    tm = cls_ref.shape[1]
    onehot = (lax.broadcasted_iota(I32, (CLASS_PAD, tm), 0) == cls_ref[...]).astype(F32)
    tile_count = jnp.sum(onehot, axis=1, keepdims=True)

    @pl.when((phase == 0) & (i == 0))
    def _():
        cnt_ref[...] = jnp.zeros_like(cnt_ref)
        r = lax.broadcasted_iota(I32, (tm, tm), 0)
        cc = lax.broadcasted_iota(I32, (tm, tm), 1)
        tri_ref[...] = (r <= cc).astype(BF16)

    @pl.when(phase == 0)
    def _():
        cnt_ref[...] += jnp.broadcast_to(tile_count, cnt_ref.shape)

    @pl.when((phase == 1) & (i == 0))
    def _():
        ntile = jnp.floor((cnt_ref[...] + (TM_MOE - 1)) * (1.0 / TM_MOE))
        r = lax.broadcasted_iota(I32, (CLASS_PAD, CLASS_PAD), 0)
        cc = lax.broadcasted_iota(I32, (CLASS_PAD, CLASS_PAD), 1)
        strict = (cc < r).astype(BF16)
        first_tile = jnp.dot(strict, ntile.astype(BF16), preferred_element_type=F32)
        toff_ref[...] = first_tile.astype(I32)
        off_ref[...] = first_tile * float(TM_MOE)
        carry_ref[...] = jnp.zeros_like(carry_ref)

    @pl.when(phase == 1)
    def _():
        prefix = jnp.dot(onehot.astype(BF16), tri_ref[...], preferred_element_type=F32)
        base = off_ref[:, 0:1] + carry_ref[:, 0:1] - 1.0
        posf = jnp.sum(onehot * (prefix + base), axis=0, keepdims=True)
        pos_ref[...] = posf.astype(I32)
        carry_ref[...] += jnp.broadcast_to(tile_count, carry_ref.shape)


def _rank(cls):
    t = cls.shape[1]
    tm = TM_RANK
    return pl.pallas_call(
        _rank_kernel,
        grid=(2, t // tm),
        in_specs=[pl.BlockSpec((1, tm), lambda p, i: (0, i))],
        out_specs=[pl.BlockSpec((1, tm), lambda p, i: (0, i * p)),
                   pl.BlockSpec((CLASS_PAD, LANES), lambda p, i: (0, 0))],
        out_shape=[jax.ShapeDtypeStruct((1, t), I32),
                   jax.ShapeDtypeStruct((CLASS_PAD, LANES), I32)],
        scratch_shapes=[pltpu.VMEM((CLASS_PAD, LANES), F32),
                        pltpu.VMEM((CLASS_PAD, LANES), F32),
                        pltpu.VMEM((CLASS_PAD, LANES), F32),
                        pltpu.VMEM((tm, tm), BF16)],
        compiler_params=_cparams(2),
        name="rank",
    )(cls)


def _dispatch_kernel(pos_ref, h_ref, xs_in_ref, xs_ref, sem):
    del xs_in_ref
    tm = h_ref.shape[0]
    base = pl.program_id(0) * tm

    for r in range(tm):
        p = pos_ref[base + r]
        pltpu.make_async_copy(h_ref.at[pl.ds(r, 1), :], xs_ref.at[pl.ds(p, 1), :],
                              sem).start(priority=r % 2)
    pltpu.make_async_copy(h_ref, xs_ref.at[pl.ds(0, tm), :], sem).wait()


def _dispatch(pos, h1, n_sorted):
    t, d = h1.shape
    tm = TM_ROWS
    xs0 = jnp.zeros((n_sorted, d), F32)
    return pl.pallas_call(
        _dispatch_kernel,
        grid_spec=pltpu.PrefetchScalarGridSpec(
            num_scalar_prefetch=1,
            grid=(t // tm,),
            in_specs=[pl.BlockSpec((tm, d), lambda i, pos: (i, 0)),
                      pl.BlockSpec(memory_space=pl.ANY)],
            out_specs=pl.BlockSpec(memory_space=pl.ANY),
            scratch_shapes=[pltpu.SemaphoreType.DMA(())],
        ),
        out_shape=jax.ShapeDtypeStruct((n_sorted, d), F32),
        input_output_aliases={2: 0},
        compiler_params=_cparams(1),
        name="dispatch",
    )(pos, h1, xs0)


def _combine_kernel(pos_ref, h_ref, ys_ref, g_ref, b_ref, o_ref, buf_ref, sem):
    tm = h_ref.shape[0]
    base = pl.program_id(0) * tm

    for r in range(tm):
        p = pos_ref[base + r]
        pltpu.make_async_copy(ys_ref.at[pl.ds(p, 1), :], buf_ref.at[pl.ds(r, 1), :],
                              sem).start(priority=r % 2)
    pltpu.make_async_copy(ys_ref.at[pl.ds(0, tm), :], buf_ref, sem).wait()
    o_ref[...] = _layer_norm(ALPHA * h_ref[...] + buf_ref[...], g_ref[...], b_ref[...])


def _combine(pos, h1, ys, ln_g, ln_b):
    t, d = h1.shape
    tm = TM_ROWS
    return pl.pallas_call(
        _combine_kernel,
        grid_spec=pltpu.PrefetchScalarGridSpec(
            num_scalar_prefetch=1,
            grid=(t // tm,),
            in_specs=[pl.BlockSpec((tm, d), lambda i, pos: (i, 0)),
                      pl.BlockSpec(memory_space=pl.ANY),
                      pl.BlockSpec((1, d), lambda i, pos: (0, 0)),
                      pl.BlockSpec((1, d), lambda i, pos: (0, 0))],
            out_specs=pl.BlockSpec((tm, d), lambda i, pos: (i, 0)),
            scratch_shapes=[pltpu.VMEM((tm, d), F32), pltpu.SemaphoreType.DMA(())],
        ),
        out_shape=jax.ShapeDtypeStruct((t, d), F32),
        compiler_params=_cparams(1),
        name="combine",
    )(pos, h1, ys, ln_g, ln_b)


def _moe_kernel(ea_ref, eb_ref, valid_ref, xs_ref, w2_ref, wga_ref, wua_ref, wda_ref,
                wgb_ref, wub_ref, wdb_ref, ys_ref, wbf_up_ref, wbf_dn_ref):
    i = pl.program_id(0)
    prev = jnp.maximum(i - 1, 0)
    fresh = (i == 0) | (ea_ref[i] != ea_ref[prev]) | (eb_ref[i] != eb_ref[prev])

    @pl.when(fresh)
    def _():
        for slot, w_ref in enumerate((wga_ref, wua_ref, wgb_ref, wub_ref)):
            wbf_up_ref[slot] = w_ref[...].astype(BF16)
        for slot, w_ref in enumerate((wda_ref, wdb_ref)):
            wbf_dn_ref[slot] = w_ref[...].astype(BF16)

    @pl.when(valid_ref[i] == 0)
    def _():
        ys_ref[...] = jnp.zeros_like(ys_ref)

    @pl.when(valid_ref[i] != 0)
    def _():
        x = xs_ref[...]
        logits = _router_logits(x, w2_ref)
        lane = lax.broadcasted_iota(I32, logits.shape, 1)
        logits = jnp.where(lane < N_EXPERTS, logits, -jnp.inf)
        e = jnp.exp(logits - jnp.max(logits, axis=1, keepdims=True))
        scores = e / jnp.sum(e, axis=1, keepdims=True)
        sa = jnp.sum(jnp.where(lane == ea_ref[i], scores, 0.0), axis=1, keepdims=True)
        sb = jnp.sum(jnp.where(lane == eb_ref[i], scores, 0.0), axis=1, keepdims=True)
        tot = sa + sb
        xb = x.astype(BF16)

        def expert(slot, gate):
            a = jnp.dot(xb, wbf_up_ref[2 * slot], preferred_element_type=F32)
            u = jnp.dot(xb, wbf_up_ref[2 * slot + 1], preferred_element_type=F32)
            hid = (a * _sigmoid(a)) * u * gate
            return jnp.dot(hid.astype(BF16), wbf_dn_ref[slot], preferred_element_type=F32)

        ys_ref[...] = expert(0, sa / tot) + expert(1, sb / tot)


def _moe(ea, eb, valid, xs, w2, wg, wu, wd, layer):
    n_sorted, d = xs.shape
    tm = TM_MOE
    f = wg.shape[3]
    idx_a = lambda i, ea, eb, va: (layer, ea[i], 0, 0)
    idx_b = lambda i, ea, eb, va: (layer, eb[i], 0, 0)
    up_a = pl.BlockSpec((None, None, d, f), idx_a)
    up_b = pl.BlockSpec((None, None, d, f), idx_b)
    dn_a = pl.BlockSpec((None, None, f, d), idx_a)
    dn_b = pl.BlockSpec((None, None, f, d), idx_b)
    return pl.pallas_call(
        _moe_kernel,
        grid_spec=pltpu.PrefetchScalarGridSpec(
            num_scalar_prefetch=3,
            grid=(n_sorted // tm,),
            in_specs=[pl.BlockSpec((tm, d), lambda i, ea, eb, va: (i, 0)),
                      pl.BlockSpec((d, LANES), lambda i, ea, eb, va: (0, 0)),
                      up_a, up_a, dn_a, up_b, up_b, dn_b],
            out_specs=pl.BlockSpec((tm, d), lambda i, ea, eb, va: (i, 0)),
            scratch_shapes=[pltpu.VMEM((4, d, f), BF16), pltpu.VMEM((2, f, d), BF16)],
        ),
        out_shape=jax.ShapeDtypeStruct((n_sorted, d), F32),
        compiler_params=_cparams(1),
        name="moe",
    )(ea, eb, valid, xs, w2, wg, wu, wd, wg, wu, wd)


_PAIR_LO = (0, 0, 0, 1, 1, 2)
_PAIR_HI = (1, 2, 3, 2, 3, 3)


def _tile_plan(toff, n_tiles):
    first = toff[:N_CLASSES, 0]
    tiles = jnp.arange(n_tiles, dtype=I32)
    cls = jnp.sum((first[None, :] <= tiles[:, None]).astype(I32), axis=1) - 1
    group = cls // N_PAIRS
    pair = cls % N_PAIRS
    lo = jnp.asarray(_PAIR_LO, I32)[pair]
    hi = jnp.asarray(_PAIR_HI, I32)[pair]
    total = toff[N_CLASSES, 0]
    valid = (tiles < total).astype(I32)
    return group * EPG + lo, group * EPG + hi, valid


def _moe_block(h1, cls, w2, wg, wu, wd, layer, ln_g, ln_b):
    t, _ = h1.shape
    n_tiles = t // TM_MOE + N_CLASSES
    pos2d, toff = _rank(cls)
    pos = pos2d.reshape(t)
    ea, eb, valid = _tile_plan(toff, n_tiles)
    xs = _dispatch(pos, h1, n_tiles * TM_MOE)
    ys = _moe(ea, eb, valid, xs, w2, wg, wu, wd, layer)
    return _combine(pos, h1, ys, ln_g, ln_b)


DEC_LANES = 6


def _decay_placement():
    pq = np.zeros((LANES, FOX_FD), np.float32)
    pk = np.zeros((LANES, FOX_FD), np.float32)
    oq = np.zeros((1, FOX_FD), np.float32)
    ok = np.zeros((1, FOX_FD), np.float32)
    for h in range(FOX_HEADS):
        base = (h // 2) * LANES + (h % 2) * DEC_LANES
        for part in range(3):
            pq[part * FOX_HEADS + h, base + part] = 1.0
            pk[part * FOX_HEADS + h, base + 3 + part] = -1.0
            oq[0, base + 3 + part] = 1.0
            ok[0, base + part] = 1.0
    return pq, pk, oq, ok


def _kv_kernel(x_ref, wk_ref, wv_ref, wf_ref, fb_ref, pq_ref, pk_ref, oq_ref, ok_ref,
               k_ref, v_ref, cq_ref, ck_ref, carry_ref, tril_ref, *, tiles_per_seq):
    i = pl.program_id(0)
    tm = x_ref.shape[0]

    @pl.when(i == 0)
    def _():
        r = lax.broadcasted_iota(I32, (tm, tm), 0)
        cc = lax.broadcasted_iota(I32, (tm, tm), 1)
        tril_ref[...] = (cc <= r).astype(BF16)

    @pl.when(i % tiles_per_seq == 0)
    def _():
        carry_ref[...] = jnp.zeros_like(carry_ref)

    xb = x_ref[...].astype(BF16)
    for w_ref, o_ref in ((wk_ref, k_ref), (wv_ref, v_ref)):
        n = o_ref.shape[1]
        for c in range(n // TN_PROJ):
            cs = slice(c * TN_PROJ, (c + 1) * TN_PROJ)
            o_ref[:, cs] = jnp.dot(xb, w_ref[:, cs], preferred_element_type=F32).astype(o_ref.dtype)
    fl = jnp.dot(xb, wf_ref[...], preferred_element_type=F32) + fb_ref[...]
    lane = lax.broadcasted_iota(I32, fl.shape, 1)
    log_f = jnp.where(lane < FOX_HEADS, _log_sigmoid(fl), 0.0)
    tril = tril_ref[...]
    cum = carry_ref[...]
    for part in _split3(log_f):
        cum = cum + jnp.dot(tril, part.astype(BF16), preferred_element_type=F32)
    carry_ref[...] = cum[tm - 1:tm, :]
    hi, mid, lo = _split3(cum * LOG2E)
    packed = hi + pltpu.roll(mid, FOX_HEADS, axis=1) + pltpu.roll(lo, 2 * FOX_HEADS, axis=1)
    packed = packed.astype(BF16)
    cq_ref[...] = (jnp.dot(packed, pq_ref[...], preferred_element_type=F32)
                   + oq_ref[...]).astype(cq_ref.dtype)
    ck_ref[...] = (jnp.dot(packed, pk_ref[...], preferred_element_type=F32)
                   + ok_ref[...]).astype(ck_ref.dtype)


def _fox_kv(h, wk, wv, wf_pad, fb_pad, seq):
    t, d = h.shape
    tm = TM_PROJ
    pq, pk, oq, ok = _decay_placement()
    full = lambda a: pl.BlockSpec(a.shape, lambda i: (0, 0))
    consts = [jnp.asarray(pq, BF16), jnp.asarray(pk, BF16), jnp.asarray(oq), jnp.asarray(ok)]
    row_out = pl.BlockSpec((tm, FOX_FD), lambda i: (i, 0))
    return pl.pallas_call(
        functools.partial(_kv_kernel, tiles_per_seq=seq // tm),
        grid=(t // tm,),
        in_specs=[pl.BlockSpec((tm, d), lambda i: (i, 0)), full(wk), full(wv), full(wf_pad),
                  full(fb_pad)] + [full(c) for c in consts],
        out_specs=[row_out, row_out, row_out, row_out],
        out_shape=[jax.ShapeDtypeStruct((t, FOX_FD), BF16)] * 4,
        scratch_shapes=[pltpu.VMEM((1, LANES), F32), pltpu.VMEM((tm, tm), BF16)],
        compiler_params=_cparams(1),
        name="fox_kv",
    )(h, wk, wv, wf_pad, fb_pad, *consts)


def _fox_kernel(qa_ref, ga_ref, cqa_ref, qb_ref, gb_ref, cqb_ref, k_ref, ck_ref, v_ref, o_ref,
                s_ref, mb_ref, m_ref, acc_ref):
    step = pl.program_id(2)
    n_steps = pl.num_programs(2)
    tq = qa_ref.shape[0]
    tk = tq
    lane = lax.broadcasted_iota(I32, (tq, LANES), 1)
    klane = lax.broadcasted_iota(I32, (tk, LANES), 1)

    def augmented(q_ref, cq_ref):
        q = q_ref[...]
        cq = cq_ref[...]
        out = []
        for hh in range(2):
            in_head = (lane >= hh * FOX_DH) & (lane < (hh + 1) * FOX_DH)
            in_dec = (lane >= hh * DEC_LANES) & (lane < (hh + 1) * DEC_LANES)
            out.append(jnp.concatenate([jnp.where(in_head, q, jnp.zeros_like(q)),
                                        jnp.where(in_dec, cq, jnp.zeros_like(cq))], axis=1))
        return out

    one_lane = (FOX_DH, 0)
    v_keep = [klane < FOX_DH, klane >= FOX_DH]

    def scores(q_augs, j):
        ks = pl.ds(pl.multiple_of(j * tk, tk), tk)
        k_aug = jnp.concatenate([k_ref[ks, :], ck_ref[ks, :]], axis=1)
        for hh in range(2):
            s = lax.dot_general(q_augs[hh], k_aug, (((1,), (1,)), ((), ())),
                                preferred_element_type=F32)
            s_ref[hh] = s
            mb_ref[hh] = jnp.broadcast_to(jnp.max(s, axis=1, keepdims=True), (tq, LANES))

    def absorb(j, masked):
        ks = pl.ds(pl.multiple_of(j * tk, tk), tk)
        v = v_ref[ks, :]
        for hh in range(2):
            ones = (klane == one_lane[hh]).astype(BF16)
            v_aug = jnp.where(v_keep[hh], v, ones)
            if masked:
                r = lax.broadcasted_iota(I32, (tq, tk), 0)
                cc = lax.broadcasted_iota(I32, (tq, tk), 1)
                s = jnp.where(cc <= r, s_ref[hh], -jnp.inf)
                mb = jnp.broadcast_to(jnp.max(s, axis=1, keepdims=True), (tq, LANES))
            else:
                s = s_ref[hh]
                mb = mb_ref[hh]
            m_old = m_ref[hh]
            m_new = jnp.maximum(m_old, mb)
            m_ref[hh] = m_new
            alpha = jnp.exp2(m_old - m_new)
            p = jnp.exp2((s - jnp.concatenate([m_new] * (tk // LANES), axis=1)).astype(BF16))
            acc_ref[hh] = alpha * acc_ref[hh] + jnp.dot(p, v_aug, preferred_element_type=F32)

    def reset():
        m_ref[...] = jnp.full(m_ref.shape, -jnp.inf, F32)
        acc_ref[...] = jnp.zeros(acc_ref.shape, F32)

    def sweep(q_augs, qi):
        def body(i, carry):
            for j in (2 * i, 2 * i + 1):
                absorb(j, False)
                scores(q_augs, j + 1)
            return carry

        lax.fori_loop(0, qi // 2, body, 0)

        @pl.when(qi % 2 == 1)
        def _():
            absorb(qi - 1, False)
            scores(q_augs, qi)

    def finish(g_ref, slot):
        acc0 = acc_ref[0]
        acc1 = acc_ref[1]
        o = jnp.where(lane < FOX_DH, acc0 / acc0[:, FOX_DH:FOX_DH + 1], acc1 / acc1[:, 0:1])
        o_ref[slot] = (o * _sigmoid(g_ref[...].astype(F32))).astype(o_ref.dtype)

    qi_a = step
    qi_b = 2 * n_steps - 1 - step
    qa = augmented(qa_ref, cqa_ref)
    qb = augmented(qb_ref, cqb_ref)
    reset()
    scores(qa, 0)
    sweep(qa, qi_a)
    absorb(qi_a, True)
    scores(qb, 0)
    finish(ga_ref, 0)
    reset()
    sweep(qb, qi_b)
    absorb(qi_b, True)
    finish(gb_ref, 1)


def _fox_attn(qg, k, v, cq, ck, batch, seq):
    tq = FOX_TQ
    nq = seq // tq
    half = nq // 2
    n_pairs = FOX_FD // LANES
    qa_blk = lambda b, p, j: (b * nq + j, p)
    qb_blk = lambda b, p, j: (b * nq + nq - 1 - j, p)
    ga_blk = lambda b, p, j: (b * nq + j, n_pairs + p)
    gb_blk = lambda b, p, j: (b * nq + nq - 1 - j, n_pairs + p)
    kv_blk = lambda b, p, j: (b, p)
    blk = lambda m: pl.BlockSpec((tq, LANES), m)
    return pl.pallas_call(
        _fox_kernel,
        grid=(batch, n_pairs, half),
        in_specs=[
            blk(qa_blk), blk(ga_blk), blk(qa_blk),
            blk(qb_blk), blk(gb_blk), blk(qb_blk),
            pl.BlockSpec((seq, LANES), kv_blk),
            pl.BlockSpec((seq, LANES), kv_blk),
            pl.BlockSpec((seq, LANES), kv_blk),
        ],
        out_specs=pl.BlockSpec((None, 2, tq, LANES), lambda b, p, j: (b * half + j, 0, 0, p)),
        out_shape=jax.ShapeDtypeStruct((batch * half, 2, tq, FOX_FD), BF16),
        scratch_shapes=[pltpu.VMEM((2, tq, tq), F32),
                        pltpu.VMEM((2, tq, LANES), F32),
                        pltpu.VMEM((2, tq, LANES), F32),
                        pltpu.VMEM((2, tq, LANES), F32)],
        compiler_params=_cparams(3),
        name="fox_attn",
    )(qg, qg, cq, qg, qg, cq, k, ck, v)


def _fox_tile_map(nq):
    assert TM_POST == FOX_TQ
    half = nq // 2

    def o_map(i):
        b = i // nq
        qi = i % nq
        late = (qi >= half).astype(I32)
        return b * half + jnp.minimum(qi, nq - 1 - qi), late

    return o_map


def kernel(x, gla_w_in, gla_w_gk, gla_b_gk, gla_norm_g, gla_w_out, kv_w, forget_bias, fox_w_qg,
           fox_w_out, router_w, router_bias, moe_w_gate, moe_w_up, moe_w_down, ln_g, ln_b):
    batch, seq, d = x.shape
    t = batch * seq
    h = x.reshape(t, d)
    w2 = _router_weight(router_w)
    rb_col = router_bias.reshape(N_EXPERTS, 1)
    n_main = 2 * GLA_HK + 2 * GLA_HV
    k_sh = v_sh = cq_sh = ck_sh = None
    for layer in range(DEPTH):
        if layer < N_A_LAYERS:
            w_in = gla_w_in[layer]
            w_main = w_in[:, :n_main].astype(BF16)
            w_gr = jnp.pad(w_in[:, n_main:], ((0, 0), (0, LANES - GLA_RANK))).astype(BF16)
            qkvg, gr = _proj(h, [w_main, w_gr], [BF16, F32])
            wgk = jnp.pad(gla_w_gk[layer], ((0, LANES - GLA_RANK), (0, 0)))
            wgk_hi = wgk.astype(BF16)
            wgk_pad = jnp.stack([wgk_hi, (wgk - wgk_hi.astype(F32)).astype(BF16)])
            o = _gla_core(qkvg, gr, wgk_pad, gla_b_gk[layer].reshape(1, GLA_HK),
                          gla_norm_g[layer].reshape(1, GLA_DV), batch, seq)
            o4 = o.reshape(t // TM_POST, 1, TM_POST, GLA_HV)
            o_map = lambda i: (i, 0)
            w_out = gla_w_out[layer].astype(BF16)
        else:
            j = layer - N_A_LAYERS
            w_qg = fox_w_qg[j]
            w_qg = jnp.concatenate([w_qg[:, :FOX_FD] * (FOX_DH ** -0.5 * LOG2E), w_qg[:, FOX_FD:]], axis=1)
            (qg,) = _proj(h, [w_qg.astype(BF16)], [BF16])
            o4 = _fox_attn(qg, k_sh, v_sh, cq_sh, ck_sh, batch, seq)
            o_map = _fox_tile_map(seq // FOX_TQ)
            w_out = fox_w_out[j].astype(BF16)
        h1, cls = _post_mixer(o4, o_map, w_out, h, ln_g[layer, 0].reshape(1, d), ln_b[layer, 0].reshape(1, d),
                              w2, rb_col)
        h = _moe_block(h1, cls, w2, moe_w_gate, moe_w_up, moe_w_down, layer,
                       ln_g[layer, 1].reshape(1, d), ln_b[layer, 1].reshape(1, d))
        if layer == N_A_LAYERS - 1:
            wf_pad = jnp.pad(kv_w[:, 2 * FOX_FD:], ((0, 0), (0, LANES - FOX_HEADS))).astype(BF16)
            fb_pad = jnp.pad(forget_bias, (0, LANES - FOX_HEADS)).reshape(1, LANES)
            k_sh, v_sh, cq_sh, ck_sh = _fox_kv(h, kv_w[:, :FOX_FD].astype(BF16),
                                            kv_w[:, FOX_FD:2 * FOX_FD].astype(BF16), wf_pad, fb_pad, seq)
    return h.reshape(batch, seq, d)
```

```python
import functools
import math

import jax
import jax.numpy as jnp
import numpy as np
from jax import lax
from jax.experimental import pallas as pl
from jax.experimental.pallas import tpu as pltpu

F32 = jnp.float32
BF16 = jnp.bfloat16
I32 = jnp.int32

D_MODEL = 1024
DEPTH = 4
N_A_LAYERS = DEPTH // 2

GLA_HEADS = 4
GLA_DK = 128
GLA_DV = 256
GLA_RANK = 16
GLA_GATE_NORM = 16.0
GLA_CHUNK = 64
GLA_HK = GLA_HEADS * GLA_DK
GLA_HV = GLA_HEADS * GLA_DV

FOX_HEADS = 16
FOX_DH = 64
FOX_FD = FOX_HEADS * FOX_DH

N_EXPERTS = 16
N_GROUPS = 4
EPG = 4
D_EXPERT = 512
N_PAIRS = 6
N_CLASSES = N_GROUPS * N_PAIRS
CLASS_PAD = 32

ALPHA = float((2 * DEPTH) ** 0.25)
LN_EPS = 1e-5
RMS_EPS = 1e-6
LOG2E = 1.4426950408889634

LANES = 128
VMEM_LIMIT = 48 * 1024 * 1024

TM_PROJ = 1024
TM_KV = 512
TN_PROJ = 512
GLA_ROWS = 512
TM_POST = 512
TM_RANK = 512
TM_ROWS = 512
TM_MOE = 256
FOX_TQ = 512


def _cparams(n_axes):
    return pltpu.CompilerParams(dimension_semantics=("arbitrary",) * n_axes,
                                vmem_limit_bytes=VMEM_LIMIT)


def _log_sigmoid(x):
    return jnp.minimum(x, 0.0) - jnp.log(1.0 + jnp.exp(-jnp.abs(x)))


def _sigmoid(x):
    return 1.0 / (1.0 + jnp.exp(-x))


def _layer_norm(z, g, b):
    mu = jnp.mean(z, axis=-1, keepdims=True)
    zc = z - mu
    var = jnp.mean(zc * zc, axis=-1, keepdims=True)
    return zc * lax.rsqrt(var + LN_EPS) * g + b


def _split3(x):
    hi = x.astype(BF16).astype(F32)
    r = x - hi
    mid = r.astype(BF16).astype(F32)
    lo = (r - mid).astype(BF16).astype(F32)
    return hi, mid, lo


def _proj_kernel(x_ref, *refs, n_w):
    xb = x_ref[...].astype(BF16)
    for w_ref, o_ref in zip(refs[:n_w], refs[n_w:]):
        n = o_ref.shape[1]
        tn = min(TN_PROJ, n)
        for c in range(n // tn):
            o_ref[:, c * tn:(c + 1) * tn] = jnp.dot(
                xb, w_ref[:, c * tn:(c + 1) * tn], preferred_element_type=F32).astype(o_ref.dtype)


def _proj(x, ws, out_dtypes):
    t, k = x.shape
    tm = TM_PROJ
    return pl.pallas_call(
        functools.partial(_proj_kernel, n_w=len(ws)),
        grid=(t // tm,),
        in_specs=[pl.BlockSpec((tm, k), lambda i: (i, 0))]
        + [pl.BlockSpec(w.shape, lambda i: (0, 0)) for w in ws],
        out_specs=[pl.BlockSpec((tm, w.shape[1]), lambda i: (i, 0)) for w in ws],
        out_shape=[jax.ShapeDtypeStruct((t, w.shape[1]), dt) for w, dt in zip(ws, out_dtypes)],
        compiler_params=_cparams(1),
        name="proj",
    )(x, *ws)


def _gla_kernel(q_ref, k_ref, v_ref, g_ref, gr_ref, wgk_ref, bgk_ref, ng_ref, o_ref, state_ref):
    c = GLA_CHUNK

    @pl.when(pl.program_id(1) == 0)
    def _():
        state_ref[...] = jnp.zeros_like(state_ref)

    gr = gr_ref[...]
    gr_hi = gr.astype(BF16)
    gr_lo = (gr - gr_hi.astype(F32)).astype(BF16)
    w_hi = wgk_ref[0]
    gkz = (jnp.dot(gr_hi, w_hi, preferred_element_type=F32)
           + jnp.dot(gr_lo, w_hi, preferred_element_type=F32)
           + jnp.dot(gr_hi, wgk_ref[1], preferred_element_type=F32)) + bgk_ref[...]
    gk = _log_sigmoid(gkz) * (1.0 / GLA_GATE_NORM)
    row = lax.broadcasted_iota(I32, (c, c), 0)
    col = lax.broadcasted_iota(I32, (c, c), 1)
    causal = col <= row
    tril = jnp.where(causal, 1.0, 0.0).astype(BF16)
    gk_parts = [part.astype(BF16) for part in _split3(gk)]
    scale = GLA_DK ** -0.5
    ng = ng_ref[...]

    for ci in range(GLA_ROWS // c):
        rs = slice(ci * c, (ci + 1) * c)
        bc = sum(jnp.dot(tril, part[rs], preferred_element_type=F32) for part in gk_parts)
        b_last = bc[c - 1:c, :]
        qf = q_ref[rs, :].astype(F32)
        kf = k_ref[rs, :].astype(F32)
        q_dec = (qf * scale * jnp.exp(bc)).astype(BF16)
        k_inv = (kf * jnp.exp(-bc)).astype(BF16)
        k_end = (kf * jnp.exp(b_last - bc)).astype(BF16)
        dec = jnp.exp(b_last)
        for h in range(GLA_HEADS):
            ks = slice(h * GLA_DK, (h + 1) * GLA_DK)
            vs = slice(h * GLA_DV, (h + 1) * GLA_DV)
            v_h = v_ref[rs, vs]
            attn = lax.dot_general(q_dec[:, ks], k_inv[:, ks], (((1,), (1,)), ((), ())),
                                   preferred_element_type=F32)
            attn = jnp.where(causal, attn, 0.0).astype(BF16)
            st = state_ref[h]
            o = jnp.dot(attn, v_h, preferred_element_type=F32)
            o = o + lax.dot_general(q_dec[:, ks], st.astype(BF16), (((1,), (1,)), ((), ())),
                                    preferred_element_type=F32)
            kv_t = lax.dot_general(v_h, k_end[:, ks], (((0,), (0,)), ((), ())),
                                   preferred_element_type=F32)
            state_ref[h] = st * dec[:, ks] + kv_t
            o = o * lax.rsqrt(jnp.mean(o * o, axis=-1, keepdims=True) + RMS_EPS) * ng
            gate = g_ref[rs, vs].astype(F32)
            o = o * (gate * _sigmoid(gate))
            o_ref[rs, vs] = o.astype(o_ref.dtype)


def _gla_core(qkvg, gr, wgk_pad, bgk, ng, batch, seq):
    t = batch * seq
    r = GLA_ROWS
    nblk = seq // r
    rowmap = lambda b, i: b * nblk + i
    return pl.pallas_call(
        _gla_kernel,
        grid=(batch, nblk),
        in_specs=[
            pl.BlockSpec((r, GLA_HK), lambda b, i: (rowmap(b, i), 0)),
            pl.BlockSpec((r, GLA_HK), lambda b, i: (rowmap(b, i), 1)),
            pl.BlockSpec((r, GLA_HV), lambda b, i: (rowmap(b, i), 1)),
            pl.BlockSpec((r, GLA_HV), lambda b, i: (rowmap(b, i), 2)),
            pl.BlockSpec((r, LANES), lambda b, i: (rowmap(b, i), 0)),
            pl.BlockSpec((2, LANES, GLA_HK), lambda b, i: (0, 0, 0)),
            pl.BlockSpec((1, GLA_HK), lambda b, i: (0, 0)),
            pl.BlockSpec((1, GLA_DV), lambda b, i: (0, 0)),
        ],
        out_specs=pl.BlockSpec((r, GLA_HV), lambda b, i: (rowmap(b, i), 0)),
        out_shape=jax.ShapeDtypeStruct((t, GLA_HV), BF16),
        scratch_shapes=[pltpu.VMEM((GLA_HEADS, GLA_DV, GLA_DK), F32)],
        compiler_params=_cparams(2),
        name="gla_core",
    )(qkvg, qkvg, qkvg, qkvg, gr, wgk_pad, bgk, ng)


def _route_class(logits_t, bias_col):
    mx = jnp.max(logits_t, axis=0, keepdims=True)
    e = jnp.exp(logits_t - mx)
    scores = e / jnp.sum(e, axis=0, keepdims=True)
    sel = scores + bias_col
    rows = [sel[j:j + 1, :] for j in range(N_EXPERTS)]
    best_g = None
    best_s = None
    for g in range(N_GROUPS):
        m = rows[g * EPG:(g + 1) * EPG]
        gs = None
        for a in range(EPG):
            for b in range(a + 1, EPG):
                s = m[a] + m[b]
                gs = s if gs is None else jnp.maximum(gs, s)
        if g == 0:
            best_g = jnp.zeros(gs.shape, I32)
            best_s = gs
        else:
            better = gs > best_s
            best_g = jnp.where(better, g, best_g)
            best_s = jnp.where(better, gs, best_s)
    mem = []
    for j in range(EPG):
        vj = rows[j]
        for g in range(1, N_GROUPS):
            vj = jnp.where(best_g == g, rows[g * EPG + j], vj)
        mem.append(vj)
    i1 = jnp.zeros(best_g.shape, I32)
    b1 = mem[0]
    for j in range(1, EPG):
        better = mem[j] > b1
        i1 = jnp.where(better, j, i1)
        b1 = jnp.where(better, mem[j], b1)
    i2 = jnp.where(i1 == 0, 1, 0).astype(I32)
    b2 = jnp.where(i1 == 0, mem[1], mem[0])
    for j in range(1, EPG):
        better = (mem[j] > b2) & (i1 != j) & (i2 != j)
        i2 = jnp.where(better, j, i2)
        b2 = jnp.where(better, mem[j], b2)
    lo = jnp.minimum(i1, i2)
    hi = jnp.maximum(i1, i2)
    pair = jnp.where(lo == 0, hi - 1, jnp.where(lo == 1, hi + 1, 5))
    return best_g * N_PAIRS + pair


def _router_weight(router_w):
    hi = router_w.astype(BF16)
    lo = (router_w - hi.astype(F32)).astype(BF16)
    pad = jnp.zeros((router_w.shape[0], LANES - 2 * N_EXPERTS), BF16)
    return jnp.concatenate([hi, lo, pad], axis=1)


def _router_logits(x, w2_ref):
    hi = x.astype(BF16)
    lo = (x - hi.astype(F32)).astype(BF16)
    w2 = w2_ref[...]
    a = jnp.dot(hi, w2, preferred_element_type=F32) + jnp.dot(lo, w2, preferred_element_type=F32)
    return a + pltpu.roll(a, LANES - N_EXPERTS, axis=1)


def _post_kernel(o_ref, w_ref, h_ref, g_ref, b_ref, w2_ref, rb_ref, h1_ref, cls_ref):
    half = h_ref.shape[0] // 2
    for part in range(2):
        rs = slice(part * half, (part + 1) * half)
        mix = jnp.dot(o_ref[rs, :], w_ref[...], preferred_element_type=F32)
        h1 = _layer_norm(ALPHA * h_ref[rs, :] + mix, g_ref[...], b_ref[...])
        h1_ref[rs, :] = h1
        logits_t = _router_logits(h1, w2_ref).T[:N_EXPERTS, :]
        cls_ref[:, rs] = _route_class(logits_t, rb_ref[...])


def _post_mixer(o4, o_map, w_out, h, ln_g, ln_b, w2, rb_col):
    t, d = h.shape
    tm = TM_POST
    kdim = o4.shape[3]
    return pl.pallas_call(
        _post_kernel,
        grid=(t // tm,),
        in_specs=[
            pl.BlockSpec((None, None, tm, kdim), lambda i: (*o_map(i), 0, 0)),
            pl.BlockSpec((kdim, d), lambda i: (0, 0)),
            pl.BlockSpec((tm, d), lambda i: (i, 0)),
            pl.BlockSpec((1, d), lambda i: (0, 0)),
            pl.BlockSpec((1, d), lambda i: (0, 0)),
            pl.BlockSpec((d, LANES), lambda i: (0, 0)),
            pl.BlockSpec((N_EXPERTS, 1), lambda i: (0, 0)),
        ],
        out_specs=[pl.BlockSpec((tm, d), lambda i: (i, 0)),
                   pl.BlockSpec((1, tm), lambda i: (0, i))],
        out_shape=[jax.ShapeDtypeStruct((t, d), F32), jax.ShapeDtypeStruct((1, t), I32)],
        compiler_params=_cparams(1),
        name="post_mixer",
    )(o4, w_out, h, ln_g, ln_b, w2, rb_col)


def _rank_kernel(cls_ref, pos_ref, toff_ref, cnt_ref, carry_ref, off_ref, tri_ref):
    phase = pl.program_id(0)
    i = pl.program_id(1)
    tm = cls_ref.shape[1]
    onehot = (lax.broadcasted_iota(I32, (CLASS_PAD, tm), 0) == cls_ref[...]).astype(F32)
    tile_count = jnp.sum(onehot, axis=1, keepdims=True)

    @pl.when((phase == 0) & (i == 0))
    def _():
        cnt_ref[...] = jnp.zeros_like(cnt_ref)
        r = lax.broadcasted_iota(I32, (tm, tm), 0)
        cc = lax.broadcasted_iota(I32, (tm, tm), 1)
        tri_ref[...] = (r <= cc).astype(BF16)

    @pl.when(phase == 0)
    def _():
        cnt_ref[...] += jnp.broadcast_to(tile_count, cnt_ref.shape)

    @pl.when((phase == 1) & (i == 0))
    def _():
        ntile = jnp.floor((cnt_ref[...] + (TM_MOE - 1)) * (1.0 / TM_MOE))
        r = lax.broadcasted_iota(I32, (CLASS_PAD, CLASS_PAD), 0)
        cc = lax.broadcasted_iota(I32, (CLASS_PAD, CLASS_PAD), 1)
        strict = (cc < r).astype(BF16)
        first_tile = jnp.dot(strict, ntile.astype(BF16), preferred_element_type=F32)
        toff_ref[...] = first_tile.astype(I32)
        off_ref[...] = first_tile * float(TM_MOE)
        carry_ref[...] = jnp.zeros_like(carry_ref)

    @pl.when(phase == 1)
    def _():
        prefix = jnp.dot(onehot.astype(BF16), tri_ref[...], preferred_element_type=F32)
        base = off_ref[:, 0:1] + carry_ref[:, 0:1] - 1.0
        posf = jnp.sum(onehot * (prefix + base), axis=0, keepdims=True)
        pos_ref[...] = posf.astype(I32)
        carry_ref[...] += jnp.broadcast_to(tile_count, carry_ref.shape)


def _rank(cls):
    t = cls.shape[1]
    tm = TM_RANK
    return pl.pallas_call(
        _rank_kernel,
        grid=(2, t // tm),
        in_specs=[pl.BlockSpec((1, tm), lambda p, i: (0, i))],
        out_specs=[pl.BlockSpec((1, tm), lambda p, i: (0, i * p)),
                   pl.BlockSpec((CLASS_PAD, LANES), lambda p, i: (0, 0))],
        out_shape=[jax.ShapeDtypeStruct((1, t), I32),
                   jax.ShapeDtypeStruct((CLASS_PAD, LANES), I32)],
        scratch_shapes=[pltpu.VMEM((CLASS_PAD, LANES), F32),
                        pltpu.VMEM((CLASS_PAD, LANES), F32),
                        pltpu.VMEM((CLASS_PAD, LANES), F32),
                        pltpu.VMEM((tm, tm), BF16)],
        compiler_params=_cparams(2),
        name="rank",
    )(cls)


def _dispatch_kernel(pos_ref, h_ref, xs_in_ref, xs_ref, sem):
    del xs_in_ref
    tm = h_ref.shape[0]
    base = pl.program_id(0) * tm

    for r in range(tm):
        p = pos_ref[base + r]
        pltpu.make_async_copy(h_ref.at[pl.ds(r, 1), :], xs_ref.at[pl.ds(p, 1), :],
                              sem).start(priority=r % 2)
    pltpu.make_async_copy(h_ref, xs_ref.at[pl.ds(0, tm), :], sem).wait()


def _dispatch(pos, h1, xs0):
    t, d = h1.shape
    tm = TM_ROWS
    n_sorted = xs0.shape[0]
    return pl.pallas_call(
        _dispatch_kernel,
        grid_spec=pltpu.PrefetchScalarGridSpec(
            num_scalar_prefetch=1,
            grid=(t // tm,),
            in_specs=[pl.BlockSpec((tm, d), lambda i, pos: (i, 0)),
                      pl.BlockSpec(memory_space=pl.ANY)],
            out_specs=pl.BlockSpec(memory_space=pl.ANY),
            scratch_shapes=[pltpu.SemaphoreType.DMA(())],
        ),
        out_shape=jax.ShapeDtypeStruct((n_sorted, d), F32),
        input_output_aliases={2: 0},
        compiler_params=_cparams(1),
        name="dispatch",
    )(pos, h1, xs0)


def _combine_kernel(pos_ref, h_ref, ys_ref, g_ref, b_ref, o_ref, buf_ref, sem):
    tm = h_ref.shape[0]
    base = pl.program_id(0) * tm

    for r in range(tm):
        p = pos_ref[base + r]
        pltpu.make_async_copy(ys_ref.at[pl.ds(p, 1), :], buf_ref.at[pl.ds(r, 1), :],
                              sem).start(priority=r % 2)
    pltpu.make_async_copy(ys_ref.at[pl.ds(0, tm), :], buf_ref, sem).wait()
    o_ref[...] = _layer_norm(ALPHA * h_ref[...] + buf_ref[...], g_ref[...], b_ref[...])


def _combine(pos, h1, ys, ln_g, ln_b):
    t, d = h1.shape
    tm = TM_ROWS
    return pl.pallas_call(
        _combine_kernel,
        grid_spec=pltpu.PrefetchScalarGridSpec(
            num_scalar_prefetch=1,
            grid=(t // tm,),
            in_specs=[pl.BlockSpec((tm, d), lambda i, pos: (i, 0)),
                      pl.BlockSpec(memory_space=pl.ANY),
                      pl.BlockSpec((1, d), lambda i, pos: (0, 0)),
                      pl.BlockSpec((1, d), lambda i, pos: (0, 0))],
            out_specs=pl.BlockSpec((tm, d), lambda i, pos: (i, 0)),
            scratch_shapes=[pltpu.VMEM((tm, d), F32), pltpu.SemaphoreType.DMA(())],
        ),
        out_shape=jax.ShapeDtypeStruct((t, d), F32),
        compiler_params=_cparams(1),
        name="combine",
    )(pos, h1, ys, ln_g, ln_b)


def _moe_kernel(ea_ref, eb_ref, valid_ref, xs_ref, w2_ref, wga_ref, wua_ref, wda_ref,
                wgb_ref, wub_ref, wdb_ref, ys_ref, wbf_up_ref, wbf_dn_ref):
    i = pl.program_id(0)
    prev = jnp.maximum(i - 1, 0)
    fresh = (i == 0) | (ea_ref[i] != ea_ref[prev]) | (eb_ref[i] != eb_ref[prev])

    @pl.when(fresh)
    def _():
        for slot, w_ref in enumerate((wga_ref, wua_ref, wgb_ref, wub_ref)):
            wbf_up_ref[slot] = w_ref[...].astype(BF16)
        for slot, w_ref in enumerate((wda_ref, wdb_ref)):
            wbf_dn_ref[slot] = w_ref[...].astype(BF16)

    @pl.when(valid_ref[i] == 0)
    def _():
        ys_ref[...] = jnp.zeros_like(ys_ref)

    @pl.when(valid_ref[i] != 0)
    def _():
        x = xs_ref[...]
        logits = _router_logits(x, w2_ref)
        lane = lax.broadcasted_iota(I32, logits.shape, 1)
        logits = jnp.where(lane < N_EXPERTS, logits, -jnp.inf)
        e = jnp.exp(logits - jnp.max(logits, axis=1, keepdims=True))
        scores = e / jnp.sum(e, axis=1, keepdims=True)
        sa = jnp.sum(jnp.where(lane == ea_ref[i], scores, 0.0), axis=1, keepdims=True)
        sb = jnp.sum(jnp.where(lane == eb_ref[i], scores, 0.0), axis=1, keepdims=True)
        tot = sa + sb
        xb = x.astype(BF16)

        def expert(slot, gate):
            a = jnp.dot(xb, wbf_up_ref[2 * slot], preferred_element_type=F32)
            u = jnp.dot(xb, wbf_up_ref[2 * slot + 1], preferred_element_type=F32)
            hid = (a * _sigmoid(a)) * u * gate
            return jnp.dot(hid.astype(BF16), wbf_dn_ref[slot], preferred_element_type=F32)

        ys_ref[...] = expert(0, sa / tot) + expert(1, sb / tot)


def _moe(ea, eb, valid, xs, w2, wg, wu, wd, layer):
    n_sorted, d = xs.shape
    tm = TM_MOE
    f = wg.shape[3]
    idx_a = lambda i, ea, eb, va: (layer, ea[i], 0, 0)
    idx_b = lambda i, ea, eb, va: (layer, eb[i], 0, 0)
    up_a = pl.BlockSpec((None, None, d, f), idx_a)
    up_b = pl.BlockSpec((None, None, d, f), idx_b)
    dn_a = pl.BlockSpec((None, None, f, d), idx_a)
    dn_b = pl.BlockSpec((None, None, f, d), idx_b)
    return pl.pallas_call(
        _moe_kernel,
        grid_spec=pltpu.PrefetchScalarGridSpec(
            num_scalar_prefetch=3,
            grid=(n_sorted // tm,),
            in_specs=[pl.BlockSpec((tm, d), lambda i, ea, eb, va: (i, 0)),
                      pl.BlockSpec((d, LANES), lambda i, ea, eb, va: (0, 0)),
                      up_a, up_a, dn_a, up_b, up_b, dn_b],
            out_specs=pl.BlockSpec((tm, d), lambda i, ea, eb, va: (i, 0)),
            scratch_shapes=[pltpu.VMEM((4, d, f), BF16), pltpu.VMEM((2, f, d), BF16)],
        ),
        out_shape=jax.ShapeDtypeStruct((n_sorted, d), F32),
        compiler_params=_cparams(1),
        name="moe",
    )(ea, eb, valid, xs, w2, wg, wu, wd, wg, wu, wd)


_PAIR_LO = (0, 0, 0, 1, 1, 2)
_PAIR_HI = (1, 2, 3, 2, 3, 3)


def _tile_plan(toff, n_tiles):
    first = toff[:N_CLASSES, 0]
    tiles = jnp.arange(n_tiles, dtype=I32)
    cls = jnp.sum((first[None, :] <= tiles[:, None]).astype(I32), axis=1) - 1
    group = cls // N_PAIRS
    pair = cls % N_PAIRS
    lo = jnp.asarray(_PAIR_LO, I32)[pair]
    hi = jnp.asarray(_PAIR_HI, I32)[pair]
    total = toff[N_CLASSES, 0]
    valid = (tiles < total).astype(I32)
    return group * EPG + lo, group * EPG + hi, valid


def _moe_block(h1, cls, w2, wg, wu, wd, layer, ln_g, ln_b, xs_buf):
    t, _ = h1.shape
    n_tiles = xs_buf.shape[0] // TM_MOE
    pos2d, toff = _rank(cls)
    pos = pos2d.reshape(t)
    ea, eb, valid = _tile_plan(toff, n_tiles)
    xs = _dispatch(pos, h1, xs_buf)
    ys = _moe(ea, eb, valid, xs, w2, wg, wu, wd, layer)
    return _combine(pos, h1, ys, ln_g, ln_b), xs


DEC_LANES = 6


def _decay_placement():
    pq = np.zeros((LANES, FOX_FD), np.float32)
    pk = np.zeros((LANES, FOX_FD), np.float32)
    oq = np.zeros((1, FOX_FD), np.float32)
    ok = np.zeros((1, FOX_FD), np.float32)
    for h in range(FOX_HEADS):
        base = (h // 2) * LANES + (h % 2) * DEC_LANES
        for part in range(3):
            pq[part * FOX_HEADS + h, base + part] = 1.0
            pk[part * FOX_HEADS + h, base + 3 + part] = -1.0
            oq[0, base + 3 + part] = 1.0
            ok[0, base + part] = 1.0
    return pq, pk, oq, ok


def _kv_kernel(x_ref, wk_ref, wv_ref, wf_ref, fb_ref, pq_ref, pk_ref, oq_ref, ok_ref,
               k_ref, v_ref, cq_ref, ck_ref, carry_ref, tril_ref, *, tiles_per_seq):
    i = pl.program_id(0)
    tm = x_ref.shape[0]

    @pl.when(i == 0)
    def _():
        r = lax.broadcasted_iota(I32, (tm, tm), 0)
        cc = lax.broadcasted_iota(I32, (tm, tm), 1)
        tril_ref[...] = (cc <= r).astype(BF16)

    @pl.when(i % tiles_per_seq == 0)
    def _():
        carry_ref[...] = jnp.zeros_like(carry_ref)

    xb = x_ref[...].astype(BF16)
    for w_ref, o_ref in ((wk_ref, k_ref), (wv_ref, v_ref)):
        n = o_ref.shape[1]
        for c in range(n // TN_PROJ):
            cs = slice(c * TN_PROJ, (c + 1) * TN_PROJ)
            o_ref[:, cs] = jnp.dot(xb, w_ref[:, cs], preferred_element_type=F32).astype(o_ref.dtype)
    fl = jnp.dot(xb, wf_ref[...], preferred_element_type=F32) + fb_ref[...]
    lane = lax.broadcasted_iota(I32, fl.shape, 1)
    log_f = jnp.where(lane < FOX_HEADS, _log_sigmoid(fl), 0.0)
    tril = tril_ref[...]
    cum = carry_ref[...]
    for part in _split3(log_f):
        cum = cum + jnp.dot(tril, part.astype(BF16), preferred_element_type=F32)
    carry_ref[...] = cum[tm - 1:tm, :]
    hi, mid, lo = _split3(cum * LOG2E)
    packed = hi + pltpu.roll(mid, FOX_HEADS, axis=1) + pltpu.roll(lo, 2 * FOX_HEADS, axis=1)
    packed = packed.astype(BF16)
    cq_ref[...] = (jnp.dot(packed, pq_ref[...], preferred_element_type=F32)
                   + oq_ref[...]).astype(cq_ref.dtype)
    ck_ref[...] = (jnp.dot(packed, pk_ref[...], preferred_element_type=F32)
                   + ok_ref[...]).astype(ck_ref.dtype)


def _fox_kv(h, wk, wv, wf_pad, fb_pad, seq):
    t, d = h.shape
    tm = TM_KV
    pq, pk, oq, ok = _decay_placement()
    full = lambda a: pl.BlockSpec(a.shape, lambda i: (0, 0))
    consts = [jnp.asarray(pq, BF16), jnp.asarray(pk, BF16), jnp.asarray(oq), jnp.asarray(ok)]
    row_out = pl.BlockSpec((tm, FOX_FD), lambda i: (i, 0))
    return pl.pallas_call(
        functools.partial(_kv_kernel, tiles_per_seq=seq // tm),
        grid=(t // tm,),
        in_specs=[pl.BlockSpec((tm, d), lambda i: (i, 0)), full(wk), full(wv), full(wf_pad),
                  full(fb_pad)] + [full(c) for c in consts],
        out_specs=[row_out, row_out, row_out, row_out],
        out_shape=[jax.ShapeDtypeStruct((t, FOX_FD), BF16)] * 4,
        scratch_shapes=[pltpu.VMEM((1, LANES), F32), pltpu.VMEM((tm, tm), BF16)],
        compiler_params=_cparams(1),
        name="fox_kv",
    )(h, wk, wv, wf_pad, fb_pad, *consts)


def _fox_kernel(qa_ref, ga_ref, cqa_ref, qb_ref, gb_ref, cqb_ref, k_ref, ck_ref, v_ref, o_ref,
                s_ref, mb_ref, m_ref, acc_ref):
    step = pl.program_id(2)
    n_steps = pl.num_programs(2)
    tq = qa_ref.shape[0]
    tk = tq
    lane = lax.broadcasted_iota(I32, (tq, LANES), 1)
    klane = lax.broadcasted_iota(I32, (tk, LANES), 1)

    def augmented(q_ref, cq_ref):
        q = q_ref[...]
        cq = cq_ref[...]
        out = []
        for hh in range(2):
            in_head = (lane >= hh * FOX_DH) & (lane < (hh + 1) * FOX_DH)
            in_dec = (lane >= hh * DEC_LANES) & (lane < (hh + 1) * DEC_LANES)
            out.append(jnp.concatenate([jnp.where(in_head, q, jnp.zeros_like(q)),
                                        jnp.where(in_dec, cq, jnp.zeros_like(cq))], axis=1))
        return out

    one_lane = (FOX_DH, 0)
    v_keep = [klane < FOX_DH, klane >= FOX_DH]

    def scores(q_augs, j):
        ks = pl.ds(pl.multiple_of(j * tk, tk), tk)
        k_aug = jnp.concatenate([k_ref[ks, :], ck_ref[ks, :]], axis=1)
        for hh in range(2):
            s = lax.dot_general(q_augs[hh], k_aug, (((1,), (1,)), ((), ())),
                                preferred_element_type=F32)
            s_ref[hh] = s
            mb_ref[hh] = jnp.broadcast_to(jnp.max(s, axis=1, keepdims=True), (tq, LANES))

    def absorb(j, masked):
        ks = pl.ds(pl.multiple_of(j * tk, tk), tk)
        v = v_ref[ks, :]
        for hh in range(2):
            ones = (klane == one_lane[hh]).astype(BF16)
            v_aug = jnp.where(v_keep[hh], v, ones)
            if masked:
                r = lax.broadcasted_iota(I32, (tq, tk), 0)
                cc = lax.broadcasted_iota(I32, (tq, tk), 1)
                s = jnp.where(cc <= r, s_ref[hh], -jnp.inf)
                mb = jnp.broadcast_to(jnp.max(s, axis=1, keepdims=True), (tq, LANES))
            else:
                s = s_ref[hh]
                mb = mb_ref[hh]
            m_old = m_ref[hh]
            m_new = jnp.maximum(m_old, mb)
            m_ref[hh] = m_new
            alpha = jnp.exp2(m_old - m_new)
            p = jnp.exp2((s - jnp.concatenate([m_new] * (tk // LANES), axis=1)).astype(BF16))
            acc_ref[hh] = alpha * acc_ref[hh] + jnp.dot(p, v_aug, preferred_element_type=F32)

    def reset():
        m_ref[...] = jnp.full(m_ref.shape, -jnp.inf, F32)
        acc_ref[...] = jnp.zeros(acc_ref.shape, F32)

    def sweep(q_augs, qi):
        def body(i, carry):
            for j in (2 * i, 2 * i + 1):
                absorb(j, False)
                scores(q_augs, j + 1)
            return carry

        lax.fori_loop(0, qi // 2, body, 0)

        @pl.when(qi % 2 == 1)
        def _():
            absorb(qi - 1, False)
            scores(q_augs, qi)

    def finish(g_ref, slot):
        acc0 = acc_ref[0]
        acc1 = acc_ref[1]
        o = jnp.where(lane < FOX_DH, acc0 / acc0[:, FOX_DH:FOX_DH + 1], acc1 / acc1[:, 0:1])
        o_ref[slot] = (o * _sigmoid(g_ref[...].astype(F32))).astype(o_ref.dtype)

    qi_a = step
    qi_b = 2 * n_steps - 1 - step
    qa = augmented(qa_ref, cqa_ref)
    qb = augmented(qb_ref, cqb_ref)
    reset()
    scores(qa, 0)
    sweep(qa, qi_a)
    absorb(qi_a, True)
    scores(qb, 0)
    finish(ga_ref, 0)
    reset()
    sweep(qb, qi_b)
    absorb(qi_b, True)
    finish(gb_ref, 1)


def _fox_attn(qg, k, v, cq, ck, batch, seq):
    tq = FOX_TQ
    nq = seq // tq
    half = nq // 2
    n_pairs = FOX_FD // LANES
    qa_blk = lambda b, p, j: (b * nq + j, p)
    qb_blk = lambda b, p, j: (b * nq + nq - 1 - j, p)
    ga_blk = lambda b, p, j: (b * nq + j, n_pairs + p)
    gb_blk = lambda b, p, j: (b * nq + nq - 1 - j, n_pairs + p)
    kv_blk = lambda b, p, j: (b, p)
    blk = lambda m: pl.BlockSpec((tq, LANES), m)
    return pl.pallas_call(
        _fox_kernel,
        grid=(batch, n_pairs, half),
        in_specs=[
            blk(qa_blk), blk(ga_blk), blk(qa_blk),
            blk(qb_blk), blk(gb_blk), blk(qb_blk),
            pl.BlockSpec((seq, LANES), kv_blk),
            pl.BlockSpec((seq, LANES), kv_blk),
            pl.BlockSpec((seq, LANES), kv_blk),
        ],
        out_specs=pl.BlockSpec((None, 2, tq, LANES), lambda b, p, j: (b * half + j, 0, 0, p)),
        out_shape=jax.ShapeDtypeStruct((batch * half, 2, tq, FOX_FD), BF16),
        scratch_shapes=[pltpu.VMEM((2, tq, tq), F32),
                        pltpu.VMEM((2, tq, LANES), F32),
                        pltpu.VMEM((2, tq, LANES), F32),
                        pltpu.VMEM((2, tq, LANES), F32)],
        compiler_params=_cparams(3),
        name="fox_attn",
    )(qg, qg, cq, qg, qg, cq, k, ck, v)


def _fox_tile_map(nq):
    assert TM_POST == FOX_TQ
    half = nq // 2

    def o_map(i):
        b = i // nq
        qi = i % nq
        late = (qi >= half).astype(I32)
        return b * half + jnp.minimum(qi, nq - 1 - qi), late

    return o_map


def kernel(x, gla_w_in, gla_w_gk, gla_b_gk, gla_norm_g, gla_w_out, kv_w, forget_bias, fox_w_qg,
           fox_w_out, router_w, router_bias, moe_w_gate, moe_w_up, moe_w_down, ln_g, ln_b):
    batch, seq, d = x.shape
    t = batch * seq
    h = x.reshape(t, d)
    w2 = _router_weight(router_w)
    rb_col = router_bias.reshape(N_EXPERTS, 1)
    n_main = 2 * GLA_HK + 2 * GLA_HV
    k_sh = v_sh = cq_sh = ck_sh = None
    xs_buf = jnp.zeros(((t // TM_MOE + N_CLASSES) * TM_MOE, d), F32)
    for layer in range(DEPTH):
        if layer < N_A_LAYERS:
            w_in = gla_w_in[layer]
            w_main = w_in[:, :n_main].astype(BF16)
            w_gr = jnp.pad(w_in[:, n_main:], ((0, 0), (0, LANES - GLA_RANK))).astype(BF16)
            qkvg, gr = _proj(h, [w_main, w_gr], [BF16, F32])
            wgk = jnp.pad(gla_w_gk[layer], ((0, LANES - GLA_RANK), (0, 0)))
            wgk_hi = wgk.astype(BF16)
            wgk_pad = jnp.stack([wgk_hi, (wgk - wgk_hi.astype(F32)).astype(BF16)])
            o = _gla_core(qkvg, gr, wgk_pad, gla_b_gk[layer].reshape(1, GLA_HK),
                          gla_norm_g[layer].reshape(1, GLA_DV), batch, seq)
            o4 = o.reshape(t // TM_POST, 1, TM_POST, GLA_HV)
            o_map = lambda i: (i, 0)
            w_out = gla_w_out[layer].astype(BF16)
        else:
            j = layer - N_A_LAYERS
            w_qg = fox_w_qg[j]
            w_qg = jnp.concatenate([w_qg[:, :FOX_FD] * (FOX_DH ** -0.5 * LOG2E), w_qg[:, FOX_FD:]], axis=1)
            (qg,) = _proj(h, [w_qg.astype(BF16)], [BF16])
            o4 = _fox_attn(qg, k_sh, v_sh, cq_sh, ck_sh, batch, seq)
            o_map = _fox_tile_map(seq // FOX_TQ)
            w_out = fox_w_out[j].astype(BF16)
        h1, cls = _post_mixer(o4, o_map, w_out, h, ln_g[layer, 0].reshape(1, d), ln_b[layer, 0].reshape(1, d),
                              w2, rb_col)
        h, xs_buf = _moe_block(h1, cls, w2, moe_w_gate, moe_w_up, moe_w_down, layer,
                               ln_g[layer, 1].reshape(1, d), ln_b[layer, 1].reshape(1, d), xs_buf)
        if layer == N_A_LAYERS - 1:
            wf_pad = jnp.pad(kv_w[:, 2 * FOX_FD:], ((0, 0), (0, LANES - FOX_HEADS))).astype(BF16)
            fb_pad = jnp.pad(forget_bias, (0, LANES - FOX_HEADS)).reshape(1, LANES)
            k_sh, v_sh, cq_sh, ck_sh = _fox_kv(h, kv_w[:, :FOX_FD].astype(BF16),
                                            kv_w[:, FOX_FD:2 * FOX_FD].astype(BF16), wf_pad, fb_pad, seq)
    return h.reshape(batch, seq, d)
```

```python
import functools
import math

import jax
import jax.numpy as jnp
import numpy as np
from jax import lax
from jax.experimental import pallas as pl
from jax.experimental.pallas import tpu as pltpu

F32 = jnp.float32
BF16 = jnp.bfloat16
I32 = jnp.int32

D_MODEL = 1024
DEPTH = 4
N_A_LAYERS = DEPTH // 2

GLA_HEADS = 4
GLA_DK = 128
GLA_DV = 256
GLA_RANK = 16
GLA_GATE_NORM = 16.0
GLA_CHUNK = 64
GLA_HK = GLA_HEADS * GLA_DK
GLA_HV = GLA_HEADS * GLA_DV

FOX_HEADS = 16
FOX_DH = 64
FOX_FD = FOX_HEADS * FOX_DH

N_EXPERTS = 16
N_GROUPS = 4
EPG = 4
D_EXPERT = 512
N_PAIRS = 6
N_CLASSES = N_GROUPS * N_PAIRS
CLASS_PAD = 32

ALPHA = float((2 * DEPTH) ** 0.25)
LN_EPS = 1e-5
RMS_EPS = 1e-6
LOG2E = 1.4426950408889634

LANES = 128
VMEM_LIMIT = 48 * 1024 * 1024

TM_PROJ = 1024
TM_KV = 512
TN_PROJ = 512
GLA_ROWS = 512
TM_POST = 1024
TM_RANK = 512
TM_ROWS = 512
TM_MOE = 256
FOX_TQ = 1024


def _cparams(n_axes):
    return pltpu.CompilerParams(dimension_semantics=("arbitrary",) * n_axes,
                                vmem_limit_bytes=VMEM_LIMIT)


def _log_sigmoid(x):
    return jnp.minimum(x, 0.0) - jnp.log(1.0 + jnp.exp(-jnp.abs(x)))


def _sigmoid(x):
    return 1.0 / (1.0 + jnp.exp(-x))


def _layer_norm(z, g, b):
    mu = jnp.mean(z, axis=-1, keepdims=True)
    zc = z - mu
    var = jnp.mean(zc * zc, axis=-1, keepdims=True)
    return zc * lax.rsqrt(var + LN_EPS) * g + b


def _split3(x):
    hi = x.astype(BF16).astype(F32)
    r = x - hi
    mid = r.astype(BF16).astype(F32)
    lo = (r - mid).astype(BF16).astype(F32)
    return hi, mid, lo


def _proj_kernel(x_ref, *refs, n_w):
    xb = x_ref[...].astype(BF16)
    for w_ref, o_ref in zip(refs[:n_w], refs[n_w:]):
        n = o_ref.shape[1]
        tn = min(TN_PROJ, n)
        for c in range(n // tn):
            o_ref[:, c * tn:(c + 1) * tn] = jnp.dot(
                xb, w_ref[:, c * tn:(c + 1) * tn], preferred_element_type=F32).astype(o_ref.dtype)


def _proj(x, ws, out_dtypes):
    t, k = x.shape
    tm = TM_PROJ
    return pl.pallas_call(
        functools.partial(_proj_kernel, n_w=len(ws)),
        grid=(t // tm,),
        in_specs=[pl.BlockSpec((tm, k), lambda i: (i, 0))]
        + [pl.BlockSpec(w.shape, lambda i: (0, 0)) for w in ws],
        out_specs=[pl.BlockSpec((tm, w.shape[1]), lambda i: (i, 0)) for w in ws],
        out_shape=[jax.ShapeDtypeStruct((t, w.shape[1]), dt) for w, dt in zip(ws, out_dtypes)],
        compiler_params=_cparams(1),
        name="proj",
    )(x, *ws)


def _gla_kernel(q_ref, k_ref, v_ref, g_ref, gr_ref, wgk_ref, bgk_ref, ng_ref, o_ref, state_ref):
    c = GLA_CHUNK

    @pl.when(pl.program_id(1) == 0)
    def _():
        state_ref[...] = jnp.zeros_like(state_ref)

    gr = gr_ref[...]
    gr_hi = gr.astype(BF16)
    gr_lo = (gr - gr_hi.astype(F32)).astype(BF16)
    w_hi = wgk_ref[0]
    gkz = (jnp.dot(gr_hi, w_hi, preferred_element_type=F32)
           + jnp.dot(gr_lo, w_hi, preferred_element_type=F32)
           + jnp.dot(gr_hi, wgk_ref[1], preferred_element_type=F32)) + bgk_ref[...]
    gk = _log_sigmoid(gkz) * (1.0 / GLA_GATE_NORM)
    row = lax.broadcasted_iota(I32, (c, c), 0)
    col = lax.broadcasted_iota(I32, (c, c), 1)
    causal = col <= row
    tril = jnp.where(causal, 1.0, 0.0).astype(BF16)
    gk_parts = [part.astype(BF16) for part in _split3(gk)]
    scale = GLA_DK ** -0.5
    ng = ng_ref[...]

    for ci in range(GLA_ROWS // c):
        rs = slice(ci * c, (ci + 1) * c)
        bc = sum(jnp.dot(tril, part[rs], preferred_element_type=F32) for part in gk_parts)
        b_last = bc[c - 1:c, :]
        qf = q_ref[rs, :].astype(F32)
        kf = k_ref[rs, :].astype(F32)
        q_dec = (qf * scale * jnp.exp(bc)).astype(BF16)
        k_inv = (kf * jnp.exp(-bc)).astype(BF16)
        k_end = (kf * jnp.exp(b_last - bc)).astype(BF16)
        dec = jnp.exp(b_last)
        for h in range(GLA_HEADS):
            ks = slice(h * GLA_DK, (h + 1) * GLA_DK)
            vs = slice(h * GLA_DV, (h + 1) * GLA_DV)
            v_h = v_ref[rs, vs]
            attn = lax.dot_general(q_dec[:, ks], k_inv[:, ks], (((1,), (1,)), ((), ())),
                                   preferred_element_type=F32)
            attn = jnp.where(causal, attn, 0.0).astype(BF16)
            st = state_ref[h]
            o = jnp.dot(attn, v_h, preferred_element_type=F32)
            o = o + lax.dot_general(q_dec[:, ks], st.astype(BF16), (((1,), (1,)), ((), ())),
                                    preferred_element_type=F32)
            kv_t = lax.dot_general(v_h, k_end[:, ks], (((0,), (0,)), ((), ())),
                                   preferred_element_type=F32)
            state_ref[h] = st * dec[:, ks] + kv_t
            o = o * lax.rsqrt(jnp.mean(o * o, axis=-1, keepdims=True) + RMS_EPS) * ng
            gate = g_ref[rs, vs].astype(F32)
            o = o * (gate * _sigmoid(gate))
            o_ref[rs, vs] = o.astype(o_ref.dtype)


def _gla_core(qkvg, gr, wgk_pad, bgk, ng, batch, seq):
    t = batch * seq
    r = GLA_ROWS
    nblk = seq // r
    rowmap = lambda b, i: b * nblk + i
    return pl.pallas_call(
        _gla_kernel,
        grid=(batch, nblk),
        in_specs=[
            pl.BlockSpec((r, GLA_HK), lambda b, i: (rowmap(b, i), 0)),
            pl.BlockSpec((r, GLA_HK), lambda b, i: (rowmap(b, i), 1)),
            pl.BlockSpec((r, GLA_HV), lambda b, i: (rowmap(b, i), 1)),
            pl.BlockSpec((r, GLA_HV), lambda b, i: (rowmap(b, i), 2)),
            pl.BlockSpec((r, LANES), lambda b, i: (rowmap(b, i), 0)),
            pl.BlockSpec((2, LANES, GLA_HK), lambda b, i: (0, 0, 0)),
            pl.BlockSpec((1, GLA_HK), lambda b, i: (0, 0)),
            pl.BlockSpec((1, GLA_DV), lambda b, i: (0, 0)),
        ],
        out_specs=pl.BlockSpec((r, GLA_HV), lambda b, i: (rowmap(b, i), 0)),
        out_shape=jax.ShapeDtypeStruct((t, GLA_HV), BF16),
        scratch_shapes=[pltpu.VMEM((GLA_HEADS, GLA_DV, GLA_DK), F32)],
        compiler_params=_cparams(2),
        name="gla_core",
    )(qkvg, qkvg, qkvg, qkvg, gr, wgk_pad, bgk, ng)


def _route_class(logits_t, bias_col):
    mx = jnp.max(logits_t, axis=0, keepdims=True)
    e = jnp.exp(logits_t - mx)
    scores = e / jnp.sum(e, axis=0, keepdims=True)
    sel = scores + bias_col
    rows = [sel[j:j + 1, :] for j in range(N_EXPERTS)]
    best_g = None
    best_s = None
    for g in range(N_GROUPS):
        m = rows[g * EPG:(g + 1) * EPG]
        gs = None
        for a in range(EPG):
            for b in range(a + 1, EPG):
                s = m[a] + m[b]
                gs = s if gs is None else jnp.maximum(gs, s)
        if g == 0:
            best_g = jnp.zeros(gs.shape, I32)
            best_s = gs
        else:
            better = gs > best_s
            best_g = jnp.where(better, g, best_g)
            best_s = jnp.where(better, gs, best_s)
    mem = []
    for j in range(EPG):
        vj = rows[j]
        for g in range(1, N_GROUPS):
            vj = jnp.where(best_g == g, rows[g * EPG + j], vj)
        mem.append(vj)
    i1 = jnp.zeros(best_g.shape, I32)
    b1 = mem[0]
    for j in range(1, EPG):
        better = mem[j] > b1
        i1 = jnp.where(better, j, i1)
        b1 = jnp.where(better, mem[j], b1)
    i2 = jnp.where(i1 == 0, 1, 0).astype(I32)
    b2 = jnp.where(i1 == 0, mem[1], mem[0])
    for j in range(1, EPG):
        better = (mem[j] > b2) & (i1 != j) & (i2 != j)
        i2 = jnp.where(better, j, i2)
        b2 = jnp.where(better, mem[j], b2)
    lo = jnp.minimum(i1, i2)
    hi = jnp.maximum(i1, i2)
    pair = jnp.where(lo == 0, hi - 1, jnp.where(lo == 1, hi + 1, 5))
    return best_g * N_PAIRS + pair


def _router_weight(router_w):
    hi = router_w.astype(BF16)
    lo = (router_w - hi.astype(F32)).astype(BF16)
    pad = jnp.zeros((router_w.shape[0], LANES - 2 * N_EXPERTS), BF16)
    return jnp.concatenate([hi, lo, pad], axis=1)


def _router_logits(x, w2_ref):
    hi = x.astype(BF16)
    lo = (x - hi.astype(F32)).astype(BF16)
    w2 = w2_ref[...]
    a = jnp.dot(hi, w2, preferred_element_type=F32) + jnp.dot(lo, w2, preferred_element_type=F32)
    return a + pltpu.roll(a, LANES - N_EXPERTS, axis=1)


def _post_kernel(o_ref, w_ref, h_ref, g_ref, b_ref, w2_ref, rb_ref, h1_ref, cls_ref):
    half = h_ref.shape[0] // 2
    for part in range(2):
        rs = slice(part * half, (part + 1) * half)
        mix = jnp.dot(o_ref[rs, :], w_ref[...], preferred_element_type=F32)
        h1 = _layer_norm(ALPHA * h_ref[rs, :] + mix, g_ref[...], b_ref[...])
        h1_ref[rs, :] = h1
        logits_t = _router_logits(h1, w2_ref).T[:N_EXPERTS, :]
        cls_ref[:, rs] = _route_class(logits_t, rb_ref[...])


def _post_mixer(o4, o_map, w_out, h, ln_g, ln_b, w2, rb_col):
    t, d = h.shape
    tm = TM_POST
    kdim = o4.shape[3]
    return pl.pallas_call(
        _post_kernel,
        grid=(t // tm,),
        in_specs=[
            pl.BlockSpec((None, None, tm, kdim), lambda i: (*o_map(i), 0, 0)),
            pl.BlockSpec((kdim, d), lambda i: (0, 0)),
            pl.BlockSpec((tm, d), lambda i: (i, 0)),
            pl.BlockSpec((1, d), lambda i: (0, 0)),
            pl.BlockSpec((1, d), lambda i: (0, 0)),
            pl.BlockSpec((d, LANES), lambda i: (0, 0)),
            pl.BlockSpec((N_EXPERTS, 1), lambda i: (0, 0)),
        ],
        out_specs=[pl.BlockSpec((tm, d), lambda i: (i, 0)),
                   pl.BlockSpec((1, tm), lambda i: (0, i))],
        out_shape=[jax.ShapeDtypeStruct((t, d), F32), jax.ShapeDtypeStruct((1, t), I32)],
        compiler_params=_cparams(1),
        name="post_mixer",
    )(o4, w_out, h, ln_g, ln_b, w2, rb_col)


def _rank_kernel(cls_ref, pos_ref, toff_ref, cnt_ref, carry_ref, off_ref, tri_ref):
    phase = pl.program_id(0)
    i = pl.program_id(1)
    tm = cls_ref.shape[1]
    onehot = (lax.broadcasted_iota(I32, (CLASS_PAD, tm), 0) == cls_ref[...]).astype(F32)
    tile_count = jnp.sum(onehot, axis=1, keepdims=True)

    @pl.when((phase == 0) & (i == 0))
    def _():
        cnt_ref[...] = jnp.zeros_like(cnt_ref)
        r = lax.broadcasted_iota(I32, (tm, tm), 0)
        cc = lax.broadcasted_iota(I32, (tm, tm), 1)
        tri_ref[...] = (r <= cc).astype(BF16)

    @pl.when(phase == 0)
    def _():
        cnt_ref[...] += jnp.broadcast_to(tile_count, cnt_ref.shape)

    @pl.when((phase == 1) & (i == 0))
    def _():
        ntile = jnp.floor((cnt_ref[...] + (TM_MOE - 1)) * (1.0 / TM_MOE))
        r = lax.broadcasted_iota(I32, (CLASS_PAD, CLASS_PAD), 0)
        cc = lax.broadcasted_iota(I32, (CLASS_PAD, CLASS_PAD), 1)
        strict = (cc < r).astype(BF16)
        first_tile = jnp.dot(strict, ntile.astype(BF16), preferred_element_type=F32)
        toff_ref[...] = first_tile.astype(I32)
        off_ref[...] = first_tile * float(TM_MOE)
        carry_ref[...] = jnp.zeros_like(carry_ref)

    @pl.when(phase == 1)
    def _():
        prefix = jnp.dot(onehot.astype(BF16), tri_ref[...], preferred_element_type=F32)
        base = off_ref[:, 0:1] + carry_ref[:, 0:1] - 1.0
        posf = jnp.sum(onehot * (prefix + base), axis=0, keepdims=True)
        pos_ref[...] = posf.astype(I32)
        carry_ref[...] += jnp.broadcast_to(tile_count, carry_ref.shape)


def _rank(cls):
    t = cls.shape[1]
    tm = TM_RANK
    return pl.pallas_call(
        _rank_kernel,
        grid=(2, t // tm),
        in_specs=[pl.BlockSpec((1, tm), lambda p, i: (0, i))],
        out_specs=[pl.BlockSpec((1, tm), lambda p, i: (0, i * p)),
                   pl.BlockSpec((CLASS_PAD, LANES), lambda p, i: (0, 0))],
        out_shape=[jax.ShapeDtypeStruct((1, t), I32),
                   jax.ShapeDtypeStruct((CLASS_PAD, LANES), I32)],
        scratch_shapes=[pltpu.VMEM((CLASS_PAD, LANES), F32),
                        pltpu.VMEM((CLASS_PAD, LANES), F32),
                        pltpu.VMEM((CLASS_PAD, LANES), F32),
                        pltpu.VMEM((tm, tm), BF16)],
        compiler_params=_cparams(2),
        name="rank",
    )(cls)


def _dispatch_kernel(pos_ref, h_ref, xs_in_ref, xs_ref, sem):
    del xs_in_ref
    tm = h_ref.shape[0]
    base = pl.program_id(0) * tm

    for r in range(tm):
        p = pos_ref[base + r]
        pltpu.make_async_copy(h_ref.at[pl.ds(r, 1), :], xs_ref.at[pl.ds(p, 1), :],
                              sem).start(priority=r % 2)
    pltpu.make_async_copy(h_ref, xs_ref.at[pl.ds(0, tm), :], sem).wait()


def _dispatch(pos, h1, xs0):
    t, d = h1.shape
    tm = TM_ROWS
    n_sorted = xs0.shape[0]
    return pl.pallas_call(
        _dispatch_kernel,
        grid_spec=pltpu.PrefetchScalarGridSpec(
            num_scalar_prefetch=1,
            grid=(t // tm,),
            in_specs=[pl.BlockSpec((tm, d), lambda i, pos: (i, 0)),
                      pl.BlockSpec(memory_space=pl.ANY)],
            out_specs=pl.BlockSpec(memory_space=pl.ANY),
            scratch_shapes=[pltpu.SemaphoreType.DMA(())],
        ),
        out_shape=jax.ShapeDtypeStruct((n_sorted, d), F32),
        input_output_aliases={2: 0},
        compiler_params=_cparams(1),
        name="dispatch",
    )(pos, h1, xs0)


def _combine_kernel(pos_ref, h_ref, ys_ref, g_ref, b_ref, o_ref, buf_ref, sem):
    tm = h_ref.shape[0]
    base = pl.program_id(0) * tm

    for r in range(tm):
        p = pos_ref[base + r]
        pltpu.make_async_copy(ys_ref.at[pl.ds(p, 1), :], buf_ref.at[pl.ds(r, 1), :],
                              sem).start(priority=r % 2)
    pltpu.make_async_copy(ys_ref.at[pl.ds(0, tm), :], buf_ref, sem).wait()
    o_ref[...] = _layer_norm(ALPHA * h_ref[...] + buf_ref[...], g_ref[...], b_ref[...])


def _combine(pos, h1, ys, ln_g, ln_b):
    t, d = h1.shape
    tm = TM_ROWS
    return pl.pallas_call(
        _combine_kernel,
        grid_spec=pltpu.PrefetchScalarGridSpec(
            num_scalar_prefetch=1,
            grid=(t // tm,),
            in_specs=[pl.BlockSpec((tm, d), lambda i, pos: (i, 0)),
                      pl.BlockSpec(memory_space=pl.ANY),
                      pl.BlockSpec((1, d), lambda i, pos: (0, 0)),
                      pl.BlockSpec((1, d), lambda i, pos: (0, 0))],
            out_specs=pl.BlockSpec((tm, d), lambda i, pos: (i, 0)),
            scratch_shapes=[pltpu.VMEM((tm, d), F32), pltpu.SemaphoreType.DMA(())],
        ),
        out_shape=jax.ShapeDtypeStruct((t, d), F32),
        compiler_params=_cparams(1),
        name="combine",
    )(pos, h1, ys, ln_g, ln_b)


def _moe_kernel(ea_ref, eb_ref, valid_ref, xs_ref, w2_ref, wga_ref, wua_ref, wda_ref,
                wgb_ref, wub_ref, wdb_ref, ys_ref, wbf_up_ref, wbf_dn_ref):
    i = pl.program_id(0)
    prev = jnp.maximum(i - 1, 0)
    fresh = (i == 0) | (ea_ref[i] != ea_ref[prev]) | (eb_ref[i] != eb_ref[prev])

    @pl.when(fresh)
    def _():
        for slot, w_ref in enumerate((wga_ref, wua_ref, wgb_ref, wub_ref)):
            wbf_up_ref[slot] = w_ref[...].astype(BF16)
        for slot, w_ref in enumerate((wda_ref, wdb_ref)):
            wbf_dn_ref[slot] = w_ref[...].astype(BF16)

    @pl.when(valid_ref[i] == 0)
    def _():
        ys_ref[...] = jnp.zeros_like(ys_ref)

    @pl.when(valid_ref[i] != 0)
    def _():
        x = xs_ref[...]
        logits = _router_logits(x, w2_ref)
        lane = lax.broadcasted_iota(I32, logits.shape, 1)
        logits = jnp.where(lane < N_EXPERTS, logits, -jnp.inf)
        e = jnp.exp(logits - jnp.max(logits, axis=1, keepdims=True))
        scores = e / jnp.sum(e, axis=1, keepdims=True)
        sa = jnp.sum(jnp.where(lane == ea_ref[i], scores, 0.0), axis=1, keepdims=True)
        sb = jnp.sum(jnp.where(lane == eb_ref[i], scores, 0.0), axis=1, keepdims=True)
        tot = sa + sb
        xb = x.astype(BF16)

        def expert(slot, gate):
            a = jnp.dot(xb, wbf_up_ref[2 * slot], preferred_element_type=F32)
            u = jnp.dot(xb, wbf_up_ref[2 * slot + 1], preferred_element_type=F32)
            hid = (a * _sigmoid(a)) * u * gate
            return jnp.dot(hid.astype(BF16), wbf_dn_ref[slot], preferred_element_type=F32)

        ys_ref[...] = expert(0, sa / tot) + expert(1, sb / tot)


def _moe(ea, eb, valid, xs, w2, wg, wu, wd, layer):
    n_sorted, d = xs.shape
    tm = TM_MOE
    f = wg.shape[3]
    idx_a = lambda i, ea, eb, va: (layer, ea[i], 0, 0)
    idx_b = lambda i, ea, eb, va: (layer, eb[i], 0, 0)
    up_a = pl.BlockSpec((None, None, d, f), idx_a)
    up_b = pl.BlockSpec((None, None, d, f), idx_b)
    dn_a = pl.BlockSpec((None, None, f, d), idx_a)
    dn_b = pl.BlockSpec((None, None, f, d), idx_b)
    return pl.pallas_call(
        _moe_kernel,
        grid_spec=pltpu.PrefetchScalarGridSpec(
            num_scalar_prefetch=3,
            grid=(n_sorted // tm,),
            in_specs=[pl.BlockSpec((tm, d), lambda i, ea, eb, va: (i, 0)),
                      pl.BlockSpec((d, LANES), lambda i, ea, eb, va: (0, 0)),
                      up_a, up_a, dn_a, up_b, up_b, dn_b],
            out_specs=pl.BlockSpec((tm, d), lambda i, ea, eb, va: (i, 0)),
            scratch_shapes=[pltpu.VMEM((4, d, f), BF16), pltpu.VMEM((2, f, d), BF16)],
        ),
        out_shape=jax.ShapeDtypeStruct((n_sorted, d), F32),
        compiler_params=_cparams(1),
        name="moe",
    )(ea, eb, valid, xs, w2, wg, wu, wd, wg, wu, wd)


_PAIR_LO = (0, 0, 0, 1, 1, 2)
_PAIR_HI = (1, 2, 3, 2, 3, 3)


def _tile_plan(toff, n_tiles):
    first = toff[:N_CLASSES, 0]
    tiles = jnp.arange(n_tiles, dtype=I32)
    cls = jnp.sum((first[None, :] <= tiles[:, None]).astype(I32), axis=1) - 1
    group = cls // N_PAIRS
    pair = cls % N_PAIRS
    lo = jnp.asarray(_PAIR_LO, I32)[pair]
    hi = jnp.asarray(_PAIR_HI, I32)[pair]
    total = toff[N_CLASSES, 0]
    valid = (tiles < total).astype(I32)
    return group * EPG + lo, group * EPG + hi, valid


def _moe_block(h1, cls, w2, wg, wu, wd, layer, ln_g, ln_b, xs_buf):
    t, _ = h1.shape
    n_tiles = xs_buf.shape[0] // TM_MOE
    pos2d, toff = _rank(cls)
    pos = pos2d.reshape(t)
    ea, eb, valid = _tile_plan(toff, n_tiles)
    xs = _dispatch(pos, h1, xs_buf)
    ys = _moe(ea, eb, valid, xs, w2, wg, wu, wd, layer)
    return _combine(pos, h1, ys, ln_g, ln_b), xs


DEC_LANES = 6


def _decay_placement():
    pq = np.zeros((LANES, FOX_FD), np.float32)
    pk = np.zeros((LANES, FOX_FD), np.float32)
    oq = np.zeros((1, FOX_FD), np.float32)
    ok = np.zeros((1, FOX_FD), np.float32)
    for h in range(FOX_HEADS):
        base = (h // 2) * LANES + (h % 2) * DEC_LANES
        for part in range(3):
            pq[part * FOX_HEADS + h, base + part] = 1.0
            pk[part * FOX_HEADS + h, base + 3 + part] = -1.0
            oq[0, base + 3 + part] = 1.0
            ok[0, base + part] = 1.0
    return pq, pk, oq, ok


def _kv_kernel(x_ref, wk_ref, wv_ref, wf_ref, fb_ref, pq_ref, pk_ref, oq_ref, ok_ref,
               k_ref, v_ref, cq_ref, ck_ref, carry_ref, tril_ref, *, tiles_per_seq):
    i = pl.program_id(0)
    tm = x_ref.shape[0]

    @pl.when(i == 0)
    def _():
        r = lax.broadcasted_iota(I32, (tm, tm), 0)
        cc = lax.broadcasted_iota(I32, (tm, tm), 1)
        tril_ref[...] = (cc <= r).astype(BF16)

    @pl.when(i % tiles_per_seq == 0)
    def _():
        carry_ref[...] = jnp.zeros_like(carry_ref)

    xb = x_ref[...].astype(BF16)
    for w_ref, o_ref in ((wk_ref, k_ref), (wv_ref, v_ref)):
        n = o_ref.shape[1]
        for c in range(n // TN_PROJ):
            cs = slice(c * TN_PROJ, (c + 1) * TN_PROJ)
            o_ref[:, cs] = jnp.dot(xb, w_ref[:, cs], preferred_element_type=F32).astype(o_ref.dtype)
    fl = jnp.dot(xb, wf_ref[...], preferred_element_type=F32) + fb_ref[...]
    lane = lax.broadcasted_iota(I32, fl.shape, 1)
    log_f = jnp.where(lane < FOX_HEADS, _log_sigmoid(fl), 0.0)
    tril = tril_ref[...]
    cum = carry_ref[...]
    for part in _split3(log_f):
        cum = cum + jnp.dot(tril, part.astype(BF16), preferred_element_type=F32)
    carry_ref[...] = cum[tm - 1:tm, :]
    hi, mid, lo = _split3(cum * LOG2E)
    packed = hi + pltpu.roll(mid, FOX_HEADS, axis=1) + pltpu.roll(lo, 2 * FOX_HEADS, axis=1)
    packed = packed.astype(BF16)
    cq_ref[...] = (jnp.dot(packed, pq_ref[...], preferred_element_type=F32)
                   + oq_ref[...]).astype(cq_ref.dtype)
    ck_ref[...] = (jnp.dot(packed, pk_ref[...], preferred_element_type=F32)
                   + ok_ref[...]).astype(ck_ref.dtype)


def _fox_kv(h, wk, wv, wf_pad, fb_pad, seq):
    t, d = h.shape
    tm = TM_KV
    pq, pk, oq, ok = _decay_placement()
    full = lambda a: pl.BlockSpec(a.shape, lambda i: (0, 0))
    consts = [jnp.asarray(pq, BF16), jnp.asarray(pk, BF16), jnp.asarray(oq), jnp.asarray(ok)]
    row_out = pl.BlockSpec((tm, FOX_FD), lambda i: (i, 0))
    return pl.pallas_call(
        functools.partial(_kv_kernel, tiles_per_seq=seq // tm),
        grid=(t // tm,),
        in_specs=[pl.BlockSpec((tm, d), lambda i: (i, 0)), full(wk), full(wv), full(wf_pad),
                  full(fb_pad)] + [full(c) for c in consts],
        out_specs=[row_out, row_out, row_out, row_out],
        out_shape=[jax.ShapeDtypeStruct((t, FOX_FD), BF16)] * 4,
        scratch_shapes=[pltpu.VMEM((1, LANES), F32), pltpu.VMEM((tm, tm), BF16)],
        compiler_params=_cparams(1),
        name="fox_kv",
    )(h, wk, wv, wf_pad, fb_pad, *consts)


def _fox_kernel(qa_ref, ga_ref, cqa_ref, qb_ref, gb_ref, cqb_ref, k_ref, ck_ref, v_ref, o_ref,
                s_ref, mb_ref, m_ref, acc_ref):
    step = pl.program_id(2)
    n_steps = pl.num_programs(2)
    tq = qa_ref.shape[0]
    tk = tq
    lane = lax.broadcasted_iota(I32, (tq, LANES), 1)
    klane = lax.broadcasted_iota(I32, (tk, LANES), 1)

    def augmented(q_ref, cq_ref):
        q = q_ref[...]
        cq = cq_ref[...]
        out = []
        for hh in range(2):
            in_head = (lane >= hh * FOX_DH) & (lane < (hh + 1) * FOX_DH)
            in_dec = (lane >= hh * DEC_LANES) & (lane < (hh + 1) * DEC_LANES)
            out.append(jnp.concatenate([jnp.where(in_head, q, jnp.zeros_like(q)),
                                        jnp.where(in_dec, cq, jnp.zeros_like(cq))], axis=1))
        return out

    one_lane = (FOX_DH, 0)
    v_keep = [klane < FOX_DH, klane >= FOX_DH]

    def scores(q_augs, j):
        ks = pl.ds(pl.multiple_of(j * tk, tk), tk)
        k_aug = jnp.concatenate([k_ref[ks, :], ck_ref[ks, :]], axis=1)
        for hh in range(2):
            s = lax.dot_general(q_augs[hh], k_aug, (((1,), (1,)), ((), ())),
                                preferred_element_type=F32)
            s_ref[hh] = s
            mb_ref[hh] = jnp.broadcast_to(jnp.max(s, axis=1, keepdims=True), (tq, LANES))

    def absorb(j, masked):
        ks = pl.ds(pl.multiple_of(j * tk, tk), tk)
        v = v_ref[ks, :]
        for hh in range(2):
            ones = (klane == one_lane[hh]).astype(BF16)
            v_aug = jnp.where(v_keep[hh], v, ones)
            if masked:
                r = lax.broadcasted_iota(I32, (tq, tk), 0)
                cc = lax.broadcasted_iota(I32, (tq, tk), 1)
                s = jnp.where(cc <= r, s_ref[hh], -jnp.inf)
                mb = jnp.broadcast_to(jnp.max(s, axis=1, keepdims=True), (tq, LANES))
            else:
                s = s_ref[hh]
                mb = mb_ref[hh]
            m_old = m_ref[hh]
            m_new = jnp.maximum(m_old, mb)
            m_ref[hh] = m_new
            alpha = jnp.exp2(m_old - m_new)
            p = jnp.exp2((s - jnp.concatenate([m_new] * (tk // LANES), axis=1)).astype(BF16))
            acc_ref[hh] = alpha * acc_ref[hh] + jnp.dot(p, v_aug, preferred_element_type=F32)

    def reset():
        m_ref[...] = jnp.full(m_ref.shape, -jnp.inf, F32)
        acc_ref[...] = jnp.zeros(acc_ref.shape, F32)

    def sweep(q_augs, qi, group):
        def run(first, count):
            for j in range(count):
                absorb(first + j, False)
                scores(q_augs, first + j + 1)

        def body(i, carry):
            run(group * i, group)
            return carry

        lax.fori_loop(0, qi // group, body, 0)
        done = (qi // group) * group
        piece = group // 2
        while piece >= 1:
            take = ((qi - done) // piece) * piece

            @pl.when(take > 0)
            def _(done=done, piece=piece):
                run(done, piece)

            done = done + take
            piece //= 2

    def finish(g_ref, slot):
        acc0 = acc_ref[0]
        acc1 = acc_ref[1]
        o = jnp.where(lane < FOX_DH, acc0 / acc0[:, FOX_DH:FOX_DH + 1], acc1 / acc1[:, 0:1])
        o_ref[slot] = (o * _sigmoid(g_ref[...].astype(F32))).astype(o_ref.dtype)

    qi_a = step
    qi_b = 2 * n_steps - 1 - step
    qa = augmented(qa_ref, cqa_ref)
    qb = augmented(qb_ref, cqb_ref)
    reset()
    scores(qa, 0)
    sweep(qa, qi_a, 2)
    absorb(qi_a, True)
    scores(qb, 0)
    finish(ga_ref, 0)
    reset()
    sweep(qb, qi_b, 2)
    absorb(qi_b, True)
    finish(gb_ref, 1)


def _fox_attn(qg, k, v, cq, ck, batch, seq):
    tq = FOX_TQ
    nq = seq // tq
    half = nq // 2
    n_pairs = FOX_FD // LANES
    qa_blk = lambda b, p, j: (b * nq + j, p)
    qb_blk = lambda b, p, j: (b * nq + nq - 1 - j, p)
    ga_blk = lambda b, p, j: (b * nq + j, n_pairs + p)
    gb_blk = lambda b, p, j: (b * nq + nq - 1 - j, n_pairs + p)
    kv_blk = lambda b, p, j: (b, p)
    blk = lambda m: pl.BlockSpec((tq, LANES), m)
    return pl.pallas_call(
        _fox_kernel,
        grid=(batch, n_pairs, half),
        in_specs=[
            blk(qa_blk), blk(ga_blk), blk(qa_blk),
            blk(qb_blk), blk(gb_blk), blk(qb_blk),
            pl.BlockSpec((seq, LANES), kv_blk),
            pl.BlockSpec((seq, LANES), kv_blk),
            pl.BlockSpec((seq, LANES), kv_blk),
        ],
        out_specs=pl.BlockSpec((None, 2, tq, LANES), lambda b, p, j: (b * half + j, 0, 0, p)),
        out_shape=jax.ShapeDtypeStruct((batch * half, 2, tq, FOX_FD), BF16),
        scratch_shapes=[pltpu.VMEM((2, tq, tq), F32),
                        pltpu.VMEM((2, tq, LANES), F32),
                        pltpu.VMEM((2, tq, LANES), F32),
                        pltpu.VMEM((2, tq, LANES), F32)],
        compiler_params=_cparams(3),
        name="fox_attn",
    )(qg, qg, cq, qg, qg, cq, k, ck, v)


def _fox_tile_map(nq):
    assert TM_POST == FOX_TQ
    half = nq // 2

    def o_map(i):
        b = i // nq
        qi = i % nq
        late = (qi >= half).astype(I32)
        return b * half + jnp.minimum(qi, nq - 1 - qi), late

    return o_map


def kernel(x, gla_w_in, gla_w_gk, gla_b_gk, gla_norm_g, gla_w_out, kv_w, forget_bias, fox_w_qg,
           fox_w_out, router_w, router_bias, moe_w_gate, moe_w_up, moe_w_down, ln_g, ln_b):
    batch, seq, d = x.shape
    t = batch * seq
    h = x.reshape(t, d)
    w2 = _router_weight(router_w)
    rb_col = router_bias.reshape(N_EXPERTS, 1)
    n_main = 2 * GLA_HK + 2 * GLA_HV
    k_sh = v_sh = cq_sh = ck_sh = None
    xs_buf = jnp.zeros(((t // TM_MOE + N_CLASSES) * TM_MOE, d), F32)
    for layer in range(DEPTH):
        if layer < N_A_LAYERS:
            w_in = gla_w_in[layer]
            w_main = w_in[:, :n_main].astype(BF16)
            w_gr = jnp.pad(w_in[:, n_main:], ((0, 0), (0, LANES - GLA_RANK))).astype(BF16)
            qkvg, gr = _proj(h, [w_main, w_gr], [BF16, F32])
            wgk = jnp.pad(gla_w_gk[layer], ((0, LANES - GLA_RANK), (0, 0)))
            wgk_hi = wgk.astype(BF16)
            wgk_pad = jnp.stack([wgk_hi, (wgk - wgk_hi.astype(F32)).astype(BF16)])
            o = _gla_core(qkvg, gr, wgk_pad, gla_b_gk[layer].reshape(1, GLA_HK),
                          gla_norm_g[layer].reshape(1, GLA_DV), batch, seq)
            o4 = o.reshape(t // TM_POST, 1, TM_POST, GLA_HV)
            o_map = lambda i: (i, 0)
            w_out = gla_w_out[layer].astype(BF16)
        else:
            j = layer - N_A_LAYERS
            w_qg = fox_w_qg[j]
            w_qg = jnp.concatenate([w_qg[:, :FOX_FD] * (FOX_DH ** -0.5 * LOG2E), w_qg[:, FOX_FD:]], axis=1)
            (qg,) = _proj(h, [w_qg.astype(BF16)], [BF16])
            o4 = _fox_attn(qg, k_sh, v_sh, cq_sh, ck_sh, batch, seq)
            o_map = _fox_tile_map(seq // FOX_TQ)
            w_out = fox_w_out[j].astype(BF16)
        h1, cls = _post_mixer(o4, o_map, w_out, h, ln_g[layer, 0].reshape(1, d), ln_b[layer, 0].reshape(1, d),
                              w2, rb_col)
        h, xs_buf = _moe_block(h1, cls, w2, moe_w_gate, moe_w_up, moe_w_down, layer,
                               ln_g[layer, 1].reshape(1, d), ln_b[layer, 1].reshape(1, d), xs_buf)
        if layer == N_A_LAYERS - 1:
            wf_pad = jnp.pad(kv_w[:, 2 * FOX_FD:], ((0, 0), (0, LANES - FOX_HEADS))).astype(BF16)
            fb_pad = jnp.pad(forget_bias, (0, LANES - FOX_HEADS)).reshape(1, LANES)
            k_sh, v_sh, cq_sh, ck_sh = _fox_kv(h, kv_w[:, :FOX_FD].astype(BF16),
                                            kv_w[:, FOX_FD:2 * FOX_FD].astype(BF16), wf_pad, fb_pad, seq)
    return h.reshape(batch, seq, d)
```

```python
import functools
import math

import jax
import jax.numpy as jnp
import numpy as np
from jax import lax
from jax.experimental import pallas as pl
from jax.experimental.pallas import tpu as pltpu

F32 = jnp.float32
BF16 = jnp.bfloat16
I32 = jnp.int32

D_MODEL = 1024
DEPTH = 4
N_A_LAYERS = DEPTH // 2

GLA_HEADS = 4
GLA_DK = 128
GLA_DV = 256
GLA_RANK = 16
GLA_GATE_NORM = 16.0
GLA_CHUNK = 64
GLA_HK = GLA_HEADS * GLA_DK
GLA_HV = GLA_HEADS * GLA_DV

FOX_HEADS = 16
FOX_DH = 64
FOX_FD = FOX_HEADS * FOX_DH

N_EXPERTS = 16
N_GROUPS = 4
EPG = 4
D_EXPERT = 512
N_PAIRS = 6
N_CLASSES = N_GROUPS * N_PAIRS
CLASS_PAD = 32

ALPHA = float((2 * DEPTH) ** 0.25)
LN_EPS = 1e-5
RMS_EPS = 1e-6
LOG2E = 1.4426950408889634

LANES = 128
VMEM_LIMIT = 48 * 1024 * 1024

TM_PROJ = 1024
TM_KV = 512
TN_PROJ = 512
GLA_ROWS = 512
TM_POST = 512
TM_RANK = 512
TM_ROWS = 512
TM_MOE = 256
FOX_TQ = 1024


def _cparams(n_axes):
    return pltpu.CompilerParams(dimension_semantics=("arbitrary",) * n_axes,
                                vmem_limit_bytes=VMEM_LIMIT)


def _log_sigmoid(x):
    return jnp.minimum(x, 0.0) - jnp.log(1.0 + jnp.exp(-jnp.abs(x)))


def _sigmoid(x):
    return 1.0 / (1.0 + jnp.exp(-x))


def _layer_norm(z, g, b):
    mu = jnp.mean(z, axis=-1, keepdims=True)
    zc = z - mu
    var = jnp.mean(zc * zc, axis=-1, keepdims=True)
    return zc * lax.rsqrt(var + LN_EPS) * g + b


def _split3(x):
    hi = x.astype(BF16).astype(F32)
    r = x - hi
    mid = r.astype(BF16).astype(F32)
    lo = (r - mid).astype(BF16).astype(F32)
    return hi, mid, lo


def _proj_kernel(x_ref, *refs, n_w):
    xb = x_ref[...].astype(BF16)
    for w_ref, o_ref in zip(refs[:n_w], refs[n_w:]):
        n = o_ref.shape[1]
        tn = min(TN_PROJ, n)
        for c in range(n // tn):
            o_ref[:, c * tn:(c + 1) * tn] = jnp.dot(
                xb, w_ref[:, c * tn:(c + 1) * tn], preferred_element_type=F32).astype(o_ref.dtype)


def _proj(x, ws, out_dtypes):
    t, k = x.shape
    tm = TM_PROJ
    return pl.pallas_call(
        functools.partial(_proj_kernel, n_w=len(ws)),
        grid=(t // tm,),
        in_specs=[pl.BlockSpec((tm, k), lambda i: (i, 0))]
        + [pl.BlockSpec(w.shape, lambda i: (0, 0)) for w in ws],
        out_specs=[pl.BlockSpec((tm, w.shape[1]), lambda i: (i, 0)) for w in ws],
        out_shape=[jax.ShapeDtypeStruct((t, w.shape[1]), dt) for w, dt in zip(ws, out_dtypes)],
        compiler_params=_cparams(1),
        name="proj",
    )(x, *ws)


def _gla_kernel(q_ref, k_ref, v_ref, g_ref, gr_ref, wgk_ref, bgk_ref, ng_ref, o_ref, state_ref):
    c = GLA_CHUNK

    @pl.when(pl.program_id(1) == 0)
    def _():
        state_ref[...] = jnp.zeros_like(state_ref)

    gr = gr_ref[...]
    gr_hi = gr.astype(BF16)
    gr_lo = (gr - gr_hi.astype(F32)).astype(BF16)
    w_hi = wgk_ref[0]
    gkz = (jnp.dot(gr_hi, w_hi, preferred_element_type=F32)
           + jnp.dot(gr_lo, w_hi, preferred_element_type=F32)
           + jnp.dot(gr_hi, wgk_ref[1], preferred_element_type=F32)) + bgk_ref[...]
    gk = _log_sigmoid(gkz) * (1.0 / GLA_GATE_NORM)
    row = lax.broadcasted_iota(I32, (c, c), 0)
    col = lax.broadcasted_iota(I32, (c, c), 1)
    causal = col <= row
    tril = jnp.where(causal, 1.0, 0.0).astype(BF16)
    gk_parts = [part.astype(BF16) for part in _split3(gk)]
    scale = GLA_DK ** -0.5
    ng = ng_ref[...]

    for ci in range(GLA_ROWS // c):
        rs = slice(ci * c, (ci + 1) * c)
        bc = sum(jnp.dot(tril, part[rs], preferred_element_type=F32) for part in gk_parts)
        b_last = bc[c - 1:c, :]
        qf = q_ref[rs, :].astype(F32)
        kf = k_ref[rs, :].astype(F32)
        q_dec = (qf * scale * jnp.exp(bc)).astype(BF16)
        k_inv = (kf * jnp.exp(-bc)).astype(BF16)
        k_end = (kf * jnp.exp(b_last - bc)).astype(BF16)
        dec = jnp.exp(b_last)
        for h in range(GLA_HEADS):
            ks = slice(h * GLA_DK, (h + 1) * GLA_DK)
            vs = slice(h * GLA_DV, (h + 1) * GLA_DV)
            v_h = v_ref[rs, vs]
            attn = lax.dot_general(q_dec[:, ks], k_inv[:, ks], (((1,), (1,)), ((), ())),
                                   preferred_element_type=F32)
            attn = jnp.where(causal, attn, 0.0).astype(BF16)
            st = state_ref[h]
            o = jnp.dot(attn, v_h, preferred_element_type=F32)
            o = o + lax.dot_general(q_dec[:, ks], st.astype(BF16), (((1,), (1,)), ((), ())),
                                    preferred_element_type=F32)
            kv_t = lax.dot_general(v_h, k_end[:, ks], (((0,), (0,)), ((), ())),
                                   preferred_element_type=F32)
            state_ref[h] = st * dec[:, ks] + kv_t
            o = o * lax.rsqrt(jnp.mean(o * o, axis=-1, keepdims=True) + RMS_EPS) * ng
            gate = g_ref[rs, vs].astype(F32)
            o = o * (gate * _sigmoid(gate))
            o_ref[rs, vs] = o.astype(o_ref.dtype)


def _gla_core(qkvg, gr, wgk_pad, bgk, ng, batch, seq):
    t = batch * seq
    r = GLA_ROWS
    nblk = seq // r
    rowmap = lambda b, i: b * nblk + i
    return pl.pallas_call(
        _gla_kernel,
        grid=(batch, nblk),
        in_specs=[
            pl.BlockSpec((r, GLA_HK), lambda b, i: (rowmap(b, i), 0)),
            pl.BlockSpec((r, GLA_HK), lambda b, i: (rowmap(b, i), 1)),
            pl.BlockSpec((r, GLA_HV), lambda b, i: (rowmap(b, i), 1)),
            pl.BlockSpec((r, GLA_HV), lambda b, i: (rowmap(b, i), 2)),
            pl.BlockSpec((r, LANES), lambda b, i: (rowmap(b, i), 0)),
            pl.BlockSpec((2, LANES, GLA_HK), lambda b, i: (0, 0, 0)),
            pl.BlockSpec((1, GLA_HK), lambda b, i: (0, 0)),
            pl.BlockSpec((1, GLA_DV), lambda b, i: (0, 0)),
        ],
        out_specs=pl.BlockSpec((r, GLA_HV), lambda b, i: (rowmap(b, i), 0)),
        out_shape=jax.ShapeDtypeStruct((t, GLA_HV), BF16),
        scratch_shapes=[pltpu.VMEM((GLA_HEADS, GLA_DV, GLA_DK), F32)],
        compiler_params=_cparams(2),
        name="gla_core",
    )(qkvg, qkvg, qkvg, qkvg, gr, wgk_pad, bgk, ng)


def _route_class(logits_t, bias_col):
    mx = jnp.max(logits_t, axis=0, keepdims=True)
    e = jnp.exp(logits_t - mx)
    scores = e / jnp.sum(e, axis=0, keepdims=True)
    sel = scores + bias_col
    rows = [sel[j:j + 1, :] for j in range(N_EXPERTS)]
    best_g = None
    best_s = None
    for g in range(N_GROUPS):
        m = rows[g * EPG:(g + 1) * EPG]
        gs = None
        for a in range(EPG):
            for b in range(a + 1, EPG):
                s = m[a] + m[b]
                gs = s if gs is None else jnp.maximum(gs, s)
        if g == 0:
            best_g = jnp.zeros(gs.shape, I32)
            best_s = gs
        else:
            better = gs > best_s
            best_g = jnp.where(better, g, best_g)
            best_s = jnp.where(better, gs, best_s)
    mem = []
    for j in range(EPG):
        vj = rows[j]
        for g in range(1, N_GROUPS):
            vj = jnp.where(best_g == g, rows[g * EPG + j], vj)
        mem.append(vj)
    i1 = jnp.zeros(best_g.shape, I32)
    b1 = mem[0]
    for j in range(1, EPG):
        better = mem[j] > b1
        i1 = jnp.where(better, j, i1)
        b1 = jnp.where(better, mem[j], b1)
    i2 = jnp.where(i1 == 0, 1, 0).astype(I32)
    b2 = jnp.where(i1 == 0, mem[1], mem[0])
    for j in range(1, EPG):
        better = (mem[j] > b2) & (i1 != j) & (i2 != j)
        i2 = jnp.where(better, j, i2)
        b2 = jnp.where(better, mem[j], b2)
    lo = jnp.minimum(i1, i2)
    hi = jnp.maximum(i1, i2)
    pair = jnp.where(lo == 0, hi - 1, jnp.where(lo == 1, hi + 1, 5))
    return best_g * N_PAIRS + pair


def _router_weight(router_w):
    hi = router_w.astype(BF16)
    lo = (router_w - hi.astype(F32)).astype(BF16)
    pad = jnp.zeros((router_w.shape[0], LANES - 2 * N_EXPERTS), BF16)
    return jnp.concatenate([hi, lo, pad], axis=1)


def _router_logits(x, w2_ref):
    hi = x.astype(BF16)
    lo = (x - hi.astype(F32)).astype(BF16)
    w2 = w2_ref[...]
    a = jnp.dot(hi, w2, preferred_element_type=F32) + jnp.dot(lo, w2, preferred_element_type=F32)
    return a + pltpu.roll(a, LANES - N_EXPERTS, axis=1)


def _post_kernel(o_ref, w_ref, h_ref, g_ref, b_ref, w2_ref, rb_ref, h1_ref, cls_ref):
    half = h_ref.shape[0] // 2
    for part in range(2):
        rs = slice(part * half, (part + 1) * half)
        mix = jnp.dot(o_ref[rs, :], w_ref[...], preferred_element_type=F32)
        h1 = _layer_norm(ALPHA * h_ref[rs, :] + mix, g_ref[...], b_ref[...])
        h1_ref[rs, :] = h1
        logits_t = _router_logits(h1, w2_ref).T[:N_EXPERTS, :]
        cls_ref[:, rs] = _route_class(logits_t, rb_ref[...])


def _post_mixer(o, w_out, h, ln_g, ln_b, w2, rb_col):
    t, d = h.shape
    tm = TM_POST
    kdim = o.shape[1]
    return pl.pallas_call(
        _post_kernel,
        grid=(t // tm,),
        in_specs=[
            pl.BlockSpec((tm, kdim), lambda i: (i, 0)),
            pl.BlockSpec((kdim, d), lambda i: (0, 0)),
            pl.BlockSpec((tm, d), lambda i: (i, 0)),
            pl.BlockSpec((1, d), lambda i: (0, 0)),
            pl.BlockSpec((1, d), lambda i: (0, 0)),
            pl.BlockSpec((d, LANES), lambda i: (0, 0)),
            pl.BlockSpec((N_EXPERTS, 1), lambda i: (0, 0)),
        ],
        out_specs=[pl.BlockSpec((tm, d), lambda i: (i, 0)),
                   pl.BlockSpec((1, tm), lambda i: (0, i))],
        out_shape=[jax.ShapeDtypeStruct((t, d), F32), jax.ShapeDtypeStruct((1, t), I32)],
        compiler_params=_cparams(1),
        name="post_mixer",
    )(o, w_out, h, ln_g, ln_b, w2, rb_col)


def _rank_kernel(cls_ref, pos_ref, toff_ref, cnt_ref, carry_ref, off_ref, tri_ref):
    phase = pl.program_id(0)
    i = pl.program_id(1)
    tm = cls_ref.shape[1]
    onehot = (lax.broadcasted_iota(I32, (CLASS_PAD, tm), 0) == cls_ref[...]).astype(F32)
    tile_count = jnp.sum(onehot, axis=1, keepdims=True)

    @pl.when((phase == 0) & (i == 0))
    def _():
        cnt_ref[...] = jnp.zeros_like(cnt_ref)
        r = lax.broadcasted_iota(I32, (tm, tm), 0)
        cc = lax.broadcasted_iota(I32, (tm, tm), 1)
        tri_ref[...] = (r <= cc).astype(BF16)

    @pl.when(phase == 0)
    def _():
        cnt_ref[...] += jnp.broadcast_to(tile_count, cnt_ref.shape)

    @pl.when((phase == 1) & (i == 0))
    def _():
        ntile = jnp.floor((cnt_ref[...] + (TM_MOE - 1)) * (1.0 / TM_MOE))
        r = lax.broadcasted_iota(I32, (CLASS_PAD, CLASS_PAD), 0)
        cc = lax.broadcasted_iota(I32, (CLASS_PAD, CLASS_PAD), 1)
        strict = (cc < r).astype(BF16)
        first_tile = jnp.dot(strict, ntile.astype(BF16), preferred_element_type=F32)
        toff_ref[...] = first_tile.astype(I32)
        off_ref[...] = first_tile * float(TM_MOE)
        carry_ref[...] = jnp.zeros_like(carry_ref)

    @pl.when(phase == 1)
    def _():
        prefix = jnp.dot(onehot.astype(BF16), tri_ref[...], preferred_element_type=F32)
        base = off_ref[:, 0:1] + carry_ref[:, 0:1] - 1.0
        posf = jnp.sum(onehot * (prefix + base), axis=0, keepdims=True)
        pos_ref[...] = posf.astype(I32)
        carry_ref[...] += jnp.broadcast_to(tile_count, carry_ref.shape)


def _rank(cls):
    t = cls.shape[1]
    tm = TM_RANK
    return pl.pallas_call(
        _rank_kernel,
        grid=(2, t // tm),
        in_specs=[pl.BlockSpec((1, tm), lambda p, i: (0, i))],
        out_specs=[pl.BlockSpec((1, tm), lambda p, i: (0, i * p)),
                   pl.BlockSpec((CLASS_PAD, LANES), lambda p, i: (0, 0))],
        out_shape=[jax.ShapeDtypeStruct((1, t), I32),
                   jax.ShapeDtypeStruct((CLASS_PAD, LANES), I32)],
        scratch_shapes=[pltpu.VMEM((CLASS_PAD, LANES), F32),
                        pltpu.VMEM((CLASS_PAD, LANES), F32),
                        pltpu.VMEM((CLASS_PAD, LANES), F32),
                        pltpu.VMEM((tm, tm), BF16)],
        compiler_params=_cparams(2),
        name="rank",
    )(cls)


def _dispatch_kernel(pos_ref, h_ref, xs_in_ref, xs_ref, sem):
    del xs_in_ref
    tm = h_ref.shape[0]
    base = pl.program_id(0) * tm

    for r in range(tm):
        p = pos_ref[base + r]
        pltpu.make_async_copy(h_ref.at[pl.ds(r, 1), :], xs_ref.at[pl.ds(p, 1), :],
                              sem).start(priority=r % 2)
    pltpu.make_async_copy(h_ref, xs_ref.at[pl.ds(0, tm), :], sem).wait()


def _dispatch(pos, h1, xs0):
    t, d = h1.shape
    tm = TM_ROWS
    n_sorted = xs0.shape[0]
    return pl.pallas_call(
        _dispatch_kernel,
        grid_spec=pltpu.PrefetchScalarGridSpec(
            num_scalar_prefetch=1,
            grid=(t // tm,),
            in_specs=[pl.BlockSpec((tm, d), lambda i, pos: (i, 0)),
                      pl.BlockSpec(memory_space=pl.ANY)],
            out_specs=pl.BlockSpec(memory_space=pl.ANY),
            scratch_shapes=[pltpu.SemaphoreType.DMA(())],
        ),
        out_shape=jax.ShapeDtypeStruct((n_sorted, d), F32),
        input_output_aliases={2: 0},
        compiler_params=_cparams(1),
        name="dispatch",
    )(pos, h1, xs0)


def _combine_kernel(pos_ref, h_ref, ys_ref, g_ref, b_ref, o_ref, buf_ref, sem):
    tm = h_ref.shape[0]
    base = pl.program_id(0) * tm

    for r in range(tm):
        p = pos_ref[base + r]
        pltpu.make_async_copy(ys_ref.at[pl.ds(p, 1), :], buf_ref.at[pl.ds(r, 1), :],
                              sem).start(priority=r % 2)
    pltpu.make_async_copy(ys_ref.at[pl.ds(0, tm), :], buf_ref, sem).wait()
    o_ref[...] = _layer_norm(ALPHA * h_ref[...] + buf_ref[...], g_ref[...], b_ref[...])


def _combine(pos, h1, ys, ln_g, ln_b):
    t, d = h1.shape
    tm = TM_ROWS
    return pl.pallas_call(
        _combine_kernel,
        grid_spec=pltpu.PrefetchScalarGridSpec(
            num_scalar_prefetch=1,
            grid=(t // tm,),
            in_specs=[pl.BlockSpec((tm, d), lambda i, pos: (i, 0)),
                      pl.BlockSpec(memory_space=pl.ANY),
                      pl.BlockSpec((1, d), lambda i, pos: (0, 0)),
                      pl.BlockSpec((1, d), lambda i, pos: (0, 0))],
            out_specs=pl.BlockSpec((tm, d), lambda i, pos: (i, 0)),
            scratch_shapes=[pltpu.VMEM((tm, d), F32), pltpu.SemaphoreType.DMA(())],
        ),
        out_shape=jax.ShapeDtypeStruct((t, d), F32),
        compiler_params=_cparams(1),
        name="combine",
    )(pos, h1, ys, ln_g, ln_b)


def _moe_kernel(ea_ref, eb_ref, valid_ref, xs_ref, w2_ref, wga_ref, wua_ref, wda_ref,
                wgb_ref, wub_ref, wdb_ref, ys_ref, wbf_up_ref, wbf_dn_ref):
    i = pl.program_id(0)
    prev = jnp.maximum(i - 1, 0)
    fresh = (i == 0) | (ea_ref[i] != ea_ref[prev]) | (eb_ref[i] != eb_ref[prev])

    @pl.when(fresh)
    def _():
        for slot, w_ref in enumerate((wga_ref, wua_ref, wgb_ref, wub_ref)):
            wbf_up_ref[slot] = w_ref[...].astype(BF16)
        for slot, w_ref in enumerate((wda_ref, wdb_ref)):
            wbf_dn_ref[slot] = w_ref[...].astype(BF16)

    @pl.when(valid_ref[i] == 0)
    def _():
        ys_ref[...] = jnp.zeros_like(ys_ref)

    @pl.when(valid_ref[i] != 0)
    def _():
        x = xs_ref[...]
        logits = _router_logits(x, w2_ref)
        lane = lax.broadcasted_iota(I32, logits.shape, 1)
        logits = jnp.where(lane < N_EXPERTS, logits, -jnp.inf)
        e = jnp.exp(logits - jnp.max(logits, axis=1, keepdims=True))
        scores = e / jnp.sum(e, axis=1, keepdims=True)
        sa = jnp.sum(jnp.where(lane == ea_ref[i], scores, 0.0), axis=1, keepdims=True)
        sb = jnp.sum(jnp.where(lane == eb_ref[i], scores, 0.0), axis=1, keepdims=True)
        tot = sa + sb
        xb = x.astype(BF16)

        def expert(slot, gate):
            a = jnp.dot(xb, wbf_up_ref[2 * slot], preferred_element_type=F32)
            u = jnp.dot(xb, wbf_up_ref[2 * slot + 1], preferred_element_type=F32)
            hid = (a * _sigmoid(a)) * u * gate
            return jnp.dot(hid.astype(BF16), wbf_dn_ref[slot], preferred_element_type=F32)

        ys_ref[...] = expert(0, sa / tot) + expert(1, sb / tot)


def _moe(ea, eb, valid, xs, w2, wg, wu, wd, layer):
    n_sorted, d = xs.shape
    tm = TM_MOE
    f = wg.shape[3]
    idx_a = lambda i, ea, eb, va: (layer, ea[i], 0, 0)
    idx_b = lambda i, ea, eb, va: (layer, eb[i], 0, 0)
    up_a = pl.BlockSpec((None, None, d, f), idx_a)
    up_b = pl.BlockSpec((None, None, d, f), idx_b)
    dn_a = pl.BlockSpec((None, None, f, d), idx_a)
    dn_b = pl.BlockSpec((None, None, f, d), idx_b)
    return pl.pallas_call(
        _moe_kernel,
        grid_spec=pltpu.PrefetchScalarGridSpec(
            num_scalar_prefetch=3,
            grid=(n_sorted // tm,),
            in_specs=[pl.BlockSpec((tm, d), lambda i, ea, eb, va: (i, 0)),
                      pl.BlockSpec((d, LANES), lambda i, ea, eb, va: (0, 0)),
                      up_a, up_a, dn_a, up_b, up_b, dn_b],
            out_specs=pl.BlockSpec((tm, d), lambda i, ea, eb, va: (i, 0)),
            scratch_shapes=[pltpu.VMEM((4, d, f), BF16), pltpu.VMEM((2, f, d), BF16)],
        ),
        out_shape=jax.ShapeDtypeStruct((n_sorted, d), F32),
        compiler_params=_cparams(1),
        name="moe",
    )(ea, eb, valid, xs, w2, wg, wu, wd, wg, wu, wd)


_PAIR_LO = (0, 0, 0, 1, 1, 2)
_PAIR_HI = (1, 2, 3, 2, 3, 3)


def _tile_plan(toff, n_tiles):
    first = toff[:N_CLASSES, 0]
    tiles = jnp.arange(n_tiles, dtype=I32)
    cls = jnp.sum((first[None, :] <= tiles[:, None]).astype(I32), axis=1) - 1
    group = cls // N_PAIRS
    pair = cls % N_PAIRS
    lo = jnp.asarray(_PAIR_LO, I32)[pair]
    hi = jnp.asarray(_PAIR_HI, I32)[pair]
    total = toff[N_CLASSES, 0]
    valid = (tiles < total).astype(I32)
    return group * EPG + lo, group * EPG + hi, valid


def _moe_block(h1, cls, w2, wg, wu, wd, layer, ln_g, ln_b, xs_buf):
    t, _ = h1.shape
    n_tiles = xs_buf.shape[0] // TM_MOE
    pos2d, toff = _rank(cls)
    pos = pos2d.reshape(t)
    ea, eb, valid = _tile_plan(toff, n_tiles)
    xs = _dispatch(pos, h1, xs_buf)
    ys = _moe(ea, eb, valid, xs, w2, wg, wu, wd, layer)
    return _combine(pos, h1, ys, ln_g, ln_b), xs


DEC_LANES = 6


def _decay_placement():
    pq = np.zeros((LANES, FOX_FD), np.float32)
    pk = np.zeros((LANES, FOX_FD), np.float32)
    oq = np.zeros((1, FOX_FD), np.float32)
    ok = np.zeros((1, FOX_FD), np.float32)
    for h in range(FOX_HEADS):
        base = (h // 2) * LANES + (h % 2) * DEC_LANES
        for part in range(3):
            pq[part * FOX_HEADS + h, base + part] = 1.0
            pk[part * FOX_HEADS + h, base + 3 + part] = -1.0
            oq[0, base + 3 + part] = 1.0
            ok[0, base + part] = 1.0
    return pq, pk, oq, ok


def _kv_kernel(x_ref, wk_ref, wv_ref, wf_ref, fb_ref, pq_ref, pk_ref, oq_ref, ok_ref,
               k_ref, v_ref, cq_ref, ck_ref, carry_ref, tril_ref, *, tiles_per_seq):
    i = pl.program_id(0)
    tm = x_ref.shape[0]

    @pl.when(i == 0)
    def _():
        r = lax.broadcasted_iota(I32, (tm, tm), 0)
        cc = lax.broadcasted_iota(I32, (tm, tm), 1)
        tril_ref[...] = (cc <= r).astype(BF16)

    @pl.when(i % tiles_per_seq == 0)
    def _():
        carry_ref[...] = jnp.zeros_like(carry_ref)

    xb = x_ref[...].astype(BF16)
    for w_ref, o_ref in ((wk_ref, k_ref), (wv_ref, v_ref)):
        n = o_ref.shape[1]
        for c in range(n // TN_PROJ):
            cs = slice(c * TN_PROJ, (c + 1) * TN_PROJ)
            o_ref[:, cs] = jnp.dot(xb, w_ref[:, cs], preferred_element_type=F32).astype(o_ref.dtype)
    fl = jnp.dot(xb, wf_ref[...], preferred_element_type=F32) + fb_ref[...]
    lane = lax.broadcasted_iota(I32, fl.shape, 1)
    log_f = jnp.where(lane < FOX_HEADS, _log_sigmoid(fl), 0.0)
    tril = tril_ref[...]
    cum = carry_ref[...]
    for part in _split3(log_f):
        cum = cum + jnp.dot(tril, part.astype(BF16), preferred_element_type=F32)
    carry_ref[...] = cum[tm - 1:tm, :]
    hi, mid, lo = _split3(cum * LOG2E)
    packed = hi + pltpu.roll(mid, FOX_HEADS, axis=1) + pltpu.roll(lo, 2 * FOX_HEADS, axis=1)
    packed = packed.astype(BF16)
    cq_ref[...] = (jnp.dot(packed, pq_ref[...], preferred_element_type=F32)
                   + oq_ref[...]).astype(cq_ref.dtype)
    ck_ref[...] = (jnp.dot(packed, pk_ref[...], preferred_element_type=F32)
                   + ok_ref[...]).astype(ck_ref.dtype)


def _fox_kv(h, wk, wv, wf_pad, fb_pad, seq):
    t, d = h.shape
    tm = TM_KV
    pq, pk, oq, ok = _decay_placement()
    full = lambda a: pl.BlockSpec(a.shape, lambda i: (0, 0))
    consts = [jnp.asarray(pq, BF16), jnp.asarray(pk, BF16), jnp.asarray(oq), jnp.asarray(ok)]
    row_out = pl.BlockSpec((tm, FOX_FD), lambda i: (i, 0))
    return pl.pallas_call(
        functools.partial(_kv_kernel, tiles_per_seq=seq // tm),
        grid=(t // tm,),
        in_specs=[pl.BlockSpec((tm, d), lambda i: (i, 0)), full(wk), full(wv), full(wf_pad),
                  full(fb_pad)] + [full(c) for c in consts],
        out_specs=[row_out, row_out, row_out, row_out],
        out_shape=[jax.ShapeDtypeStruct((t, FOX_FD), BF16)] * 4,
        scratch_shapes=[pltpu.VMEM((1, LANES), F32), pltpu.VMEM((tm, tm), BF16)],
        compiler_params=_cparams(1),
        name="fox_kv",
    )(h, wk, wv, wf_pad, fb_pad, *consts)


def _fox_kernel(q_ref, g_ref, cq_ref, k_ref, ck_ref, v_ref, o_ref, s_ref, mb_ref, m_ref, acc_ref):
    seq = q_ref.shape[0]
    tq = FOX_TQ
    tk = tq
    nq = seq // tq
    lane = lax.broadcasted_iota(I32, (tq, LANES), 1)
    klane = lax.broadcasted_iota(I32, (tk, LANES), 1)

    def augmented(qi):
        rows = slice(qi * tq, (qi + 1) * tq)
        q = q_ref[rows, :]
        cq = cq_ref[rows, :]
        out = []
        for hh in range(2):
            in_head = (lane >= hh * FOX_DH) & (lane < (hh + 1) * FOX_DH)
            in_dec = (lane >= hh * DEC_LANES) & (lane < (hh + 1) * DEC_LANES)
            out.append(jnp.concatenate([jnp.where(in_head, q, jnp.zeros_like(q)),
                                        jnp.where(in_dec, cq, jnp.zeros_like(cq))], axis=1))
        return out

    one_lane = (FOX_DH, 0)
    v_keep = [klane < FOX_DH, klane >= FOX_DH]

    def scores(q_augs, j):
        ks = slice(j * tk, (j + 1) * tk)
        k_aug = jnp.concatenate([k_ref[ks, :], ck_ref[ks, :]], axis=1)
        for hh in range(2):
            s = lax.dot_general(q_augs[hh], k_aug, (((1,), (1,)), ((), ())),
                                preferred_element_type=F32)
            s_ref[hh] = s
            mb_ref[hh] = jnp.broadcast_to(jnp.max(s, axis=1, keepdims=True), (tq, LANES))

    def absorb(j, masked):
        ks = slice(j * tk, (j + 1) * tk)
        v = v_ref[ks, :]
        for hh in range(2):
            ones = (klane == one_lane[hh]).astype(BF16)
            v_aug = jnp.where(v_keep[hh], v, ones)
            if masked:
                r = lax.broadcasted_iota(I32, (tq, tk), 0)
                cc = lax.broadcasted_iota(I32, (tq, tk), 1)
                s = jnp.where(cc <= r, s_ref[hh], -jnp.inf)
                mb = jnp.broadcast_to(jnp.max(s, axis=1, keepdims=True), (tq, LANES))
            else:
                s = s_ref[hh]
                mb = mb_ref[hh]
            m_old = m_ref[hh]
            m_new = jnp.maximum(m_old, mb)
            m_ref[hh] = m_new
            alpha = jnp.exp2(m_old - m_new)
            p = jnp.exp2((s - jnp.concatenate([m_new] * (tk // LANES), axis=1)).astype(BF16))
            acc_ref[hh] = alpha * acc_ref[hh] + jnp.dot(p, v_aug, preferred_element_type=F32)

    def reset():
        m_ref[...] = jnp.full(m_ref.shape, -jnp.inf, F32)
        acc_ref[...] = jnp.zeros(acc_ref.shape, F32)

    def finish(qi):
        rows = slice(qi * tq, (qi + 1) * tq)
        acc0 = acc_ref[0]
        acc1 = acc_ref[1]
        o = jnp.where(lane < FOX_DH, acc0 / acc0[:, FOX_DH:FOX_DH + 1], acc1 / acc1[:, 0:1])
        o_ref[rows, :] = (o * _sigmoid(g_ref[rows, :].astype(F32))).astype(o_ref.dtype)

    pairs = [(qi, j) for qi in range(nq) for j in range(qi + 1)]
    q_augs = augmented(0)
    scores(q_augs, 0)
    for n, (qi, j) in enumerate(pairs):
        if j == 0:
            reset()
        absorb(j, j == qi)
        if n + 1 < len(pairs):
            qn, jn = pairs[n + 1]
            if qn != qi:
                q_augs = augmented(qn)
            scores(q_augs, jn)
        if j == qi:
            finish(qi)


def _fox_attn(qg, k, v, cq, ck, batch, seq):
    t = batch * seq
    n_pairs = FOX_FD // LANES
    blk = lambda m: pl.BlockSpec((seq, LANES), m)
    pair_blk = lambda b, p: (b, p)
    return pl.pallas_call(
        _fox_kernel,
        grid=(batch, n_pairs),
        in_specs=[blk(pair_blk),
                  blk(lambda b, p: (b, n_pairs + p)),
                  blk(pair_blk),
                  blk(pair_blk), blk(pair_blk), blk(pair_blk)],
        out_specs=blk(pair_blk),
        out_shape=jax.ShapeDtypeStruct((t, FOX_FD), BF16),
        scratch_shapes=[pltpu.VMEM((2, FOX_TQ, FOX_TQ), F32),
                        pltpu.VMEM((2, FOX_TQ, LANES), F32),
                        pltpu.VMEM((2, FOX_TQ, LANES), F32),
                        pltpu.VMEM((2, FOX_TQ, LANES), F32)],
        compiler_params=_cparams(2),
        name="fox_attn",
    )(qg, qg, cq, k, ck, v)


def kernel(x, gla_w_in, gla_w_gk, gla_b_gk, gla_norm_g, gla_w_out, kv_w, forget_bias, fox_w_qg,
           fox_w_out, router_w, router_bias, moe_w_gate, moe_w_up, moe_w_down, ln_g, ln_b):
    batch, seq, d = x.shape
    t = batch * seq
    h = x.reshape(t, d)
    w2 = _router_weight(router_w)
    rb_col = router_bias.reshape(N_EXPERTS, 1)
    n_main = 2 * GLA_HK + 2 * GLA_HV
    k_sh = v_sh = cq_sh = ck_sh = None
    xs_buf = jnp.zeros(((t // TM_MOE + N_CLASSES) * TM_MOE, d), F32)
    for layer in range(DEPTH):
        if layer < N_A_LAYERS:
            w_in = gla_w_in[layer]
            w_main = w_in[:, :n_main].astype(BF16)
            w_gr = jnp.pad(w_in[:, n_main:], ((0, 0), (0, LANES - GLA_RANK))).astype(BF16)
            qkvg, gr = _proj(h, [w_main, w_gr], [BF16, F32])
            wgk = jnp.pad(gla_w_gk[layer], ((0, LANES - GLA_RANK), (0, 0)))
            wgk_hi = wgk.astype(BF16)
            wgk_pad = jnp.stack([wgk_hi, (wgk - wgk_hi.astype(F32)).astype(BF16)])
            o = _gla_core(qkvg, gr, wgk_pad, gla_b_gk[layer].reshape(1, GLA_HK),
                          gla_norm_g[layer].reshape(1, GLA_DV), batch, seq)
            w_out = gla_w_out[layer].astype(BF16)
        else:
            j = layer - N_A_LAYERS
            w_qg = fox_w_qg[j]
            w_qg = jnp.concatenate([w_qg[:, :FOX_FD] * (FOX_DH ** -0.5 * LOG2E), w_qg[:, FOX_FD:]], axis=1)
            (qg,) = _proj(h, [w_qg.astype(BF16)], [BF16])
            o = _fox_attn(qg, k_sh, v_sh, cq_sh, ck_sh, batch, seq)
            w_out = fox_w_out[j].astype(BF16)
        h1, cls = _post_mixer(o, w_out, h, ln_g[layer, 0].reshape(1, d), ln_b[layer, 0].reshape(1, d),
                              w2, rb_col)
        h, xs_buf = _moe_block(h1, cls, w2, moe_w_gate, moe_w_up, moe_w_down, layer,
                               ln_g[layer, 1].reshape(1, d), ln_b[layer, 1].reshape(1, d), xs_buf)
        if layer == N_A_LAYERS - 1:
            wf_pad = jnp.pad(kv_w[:, 2 * FOX_FD:], ((0, 0), (0, LANES - FOX_HEADS))).astype(BF16)
            fb_pad = jnp.pad(forget_bias, (0, LANES - FOX_HEADS)).reshape(1, LANES)
            k_sh, v_sh, cq_sh, ck_sh = _fox_kv(h, kv_w[:, :FOX_FD].astype(BF16),
                                            kv_w[:, FOX_FD:2 * FOX_FD].astype(BF16), wf_pad, fb_pad, seq)
    return h.reshape(batch, seq, d)
```

```python
import functools
import math

import jax
import jax.numpy as jnp
import numpy as np
from jax import lax
from jax.experimental import pallas as pl
from jax.experimental.pallas import tpu as pltpu

F32 = jnp.float32
BF16 = jnp.bfloat16
I32 = jnp.int32

D_MODEL = 1024
DEPTH = 4
N_A_LAYERS = DEPTH // 2

GLA_HEADS = 4
GLA_DK = 128
GLA_DV = 256
GLA_RANK = 16
GLA_GATE_NORM = 16.0
GLA_CHUNK = 64
GLA_HK = GLA_HEADS * GLA_DK
GLA_HV = GLA_HEADS * GLA_DV

FOX_HEADS = 16
FOX_DH = 64
FOX_FD = FOX_HEADS * FOX_DH

N_EXPERTS = 16
N_GROUPS = 4
EPG = 4
D_EXPERT = 512
N_PAIRS = 6
N_CLASSES = N_GROUPS * N_PAIRS
CLASS_PAD = 32

ALPHA = float((2 * DEPTH) ** 0.25)
LN_EPS = 1e-5
RMS_EPS = 1e-6
LOG2E = 1.4426950408889634

LANES = 128
VMEM_LIMIT = 48 * 1024 * 1024

TM_PROJ = 1024
TM_KV = 512
TN_PROJ = 512
GLA_ROWS = 512
TM_POST = 512
TM_RANK = 512
TM_ROWS = 512
TM_MOE = 256
FOX_TQ = 1024


def _cparams(n_axes):
    return pltpu.CompilerParams(dimension_semantics=("arbitrary",) * n_axes,
                                vmem_limit_bytes=VMEM_LIMIT)


def _log_sigmoid(x):
    return jnp.minimum(x, 0.0) - jnp.log(1.0 + jnp.exp(-jnp.abs(x)))


def _sigmoid(x):
    return 1.0 / (1.0 + jnp.exp(-x))


def _layer_norm(z, g, b):
    mu = jnp.mean(z, axis=-1, keepdims=True)
    zc = z - mu
    var = jnp.mean(zc * zc, axis=-1, keepdims=True)
    return zc * lax.rsqrt(var + LN_EPS) * g + b


def _split3(x):
    hi = x.astype(BF16).astype(F32)
    r = x - hi
    mid = r.astype(BF16).astype(F32)
    lo = (r - mid).astype(BF16).astype(F32)
    return hi, mid, lo


def _proj_kernel(x_ref, *refs, n_w):
    xb = x_ref[...].astype(BF16)
    for w_ref, o_ref in zip(refs[:n_w], refs[n_w:]):
        n = o_ref.shape[1]
        tn = min(TN_PROJ, n)
        for c in range(n // tn):
            o_ref[:, c * tn:(c + 1) * tn] = jnp.dot(
                xb, w_ref[:, c * tn:(c + 1) * tn], preferred_element_type=F32).astype(o_ref.dtype)


def _proj(x, ws, out_dtypes):
    t, k = x.shape
    tm = TM_PROJ
    return pl.pallas_call(
        functools.partial(_proj_kernel, n_w=len(ws)),
        grid=(t // tm,),
        in_specs=[pl.BlockSpec((tm, k), lambda i: (i, 0))]
        + [pl.BlockSpec(w.shape, lambda i: (0, 0)) for w in ws],
        out_specs=[pl.BlockSpec((tm, w.shape[1]), lambda i: (i, 0)) for w in ws],
        out_shape=[jax.ShapeDtypeStruct((t, w.shape[1]), dt) for w, dt in zip(ws, out_dtypes)],
        compiler_params=_cparams(1),
        name="proj",
    )(x, *ws)


def _gla_kernel(q_ref, k_ref, v_ref, g_ref, gr_ref, wgk_ref, bgk_ref, ng_ref, o_ref, state_ref):
    c = GLA_CHUNK

    @pl.when(pl.program_id(1) == 0)
    def _():
        state_ref[...] = jnp.zeros_like(state_ref)

    gr = gr_ref[...]
    gr_hi = gr.astype(BF16)
    gr_lo = (gr - gr_hi.astype(F32)).astype(BF16)
    w_hi = wgk_ref[0]
    gkz = (jnp.dot(gr_hi, w_hi, preferred_element_type=F32)
           + jnp.dot(gr_lo, w_hi, preferred_element_type=F32)
           + jnp.dot(gr_hi, wgk_ref[1], preferred_element_type=F32)) + bgk_ref[...]
    gk = _log_sigmoid(gkz) * (1.0 / GLA_GATE_NORM)
    row = lax.broadcasted_iota(I32, (c, c), 0)
    col = lax.broadcasted_iota(I32, (c, c), 1)
    causal = col <= row
    tril = jnp.where(causal, 1.0, 0.0).astype(BF16)
    gk_parts = [part.astype(BF16) for part in _split3(gk)]
    scale = GLA_DK ** -0.5
    ng = ng_ref[...]

    for ci in range(GLA_ROWS // c):
        rs = slice(ci * c, (ci + 1) * c)
        bc = sum(jnp.dot(tril, part[rs], preferred_element_type=F32) for part in gk_parts)
        b_last = bc[c - 1:c, :]
        qf = q_ref[rs, :].astype(F32)
        kf = k_ref[rs, :].astype(F32)
        q_dec = (qf * scale * jnp.exp(bc)).astype(BF16)
        k_inv = (kf * jnp.exp(-bc)).astype(BF16)
        k_end = (kf * jnp.exp(b_last - bc)).astype(BF16)
        dec = jnp.exp(b_last)
        for h in range(GLA_HEADS):
            ks = slice(h * GLA_DK, (h + 1) * GLA_DK)
            vs = slice(h * GLA_DV, (h + 1) * GLA_DV)
            v_h = v_ref[rs, vs]
            attn = lax.dot_general(q_dec[:, ks], k_inv[:, ks], (((1,), (1,)), ((), ())),
                                   preferred_element_type=F32)
            attn = jnp.where(causal, attn, 0.0).astype(BF16)
            st = state_ref[h]
            o = jnp.dot(attn, v_h, preferred_element_type=F32)
            o = o + lax.dot_general(q_dec[:, ks], st.astype(BF16), (((1,), (1,)), ((), ())),
                                    preferred_element_type=F32)
            kv_t = lax.dot_general(v_h, k_end[:, ks], (((0,), (0,)), ((), ())),
                                   preferred_element_type=F32)
            state_ref[h] = st * dec[:, ks] + kv_t
            o = o * lax.rsqrt(jnp.mean(o * o, axis=-1, keepdims=True) + RMS_EPS) * ng
            gate = g_ref[rs, vs].astype(F32)
            o = o * (gate * _sigmoid(gate))
            o_ref[rs, vs] = o.astype(o_ref.dtype)


def _gla_core(qkvg, gr, wgk_pad, bgk, ng, batch, seq):
    t = batch * seq
    r = GLA_ROWS
    nblk = seq // r
    rowmap = lambda b, i: b * nblk + i
    return pl.pallas_call(
        _gla_kernel,
        grid=(batch, nblk),
        in_specs=[
            pl.BlockSpec((r, GLA_HK), lambda b, i: (rowmap(b, i), 0)),
            pl.BlockSpec((r, GLA_HK), lambda b, i: (rowmap(b, i), 1)),
            pl.BlockSpec((r, GLA_HV), lambda b, i: (rowmap(b, i), 1)),
            pl.BlockSpec((r, GLA_HV), lambda b, i: (rowmap(b, i), 2)),
            pl.BlockSpec((r, LANES), lambda b, i: (rowmap(b, i), 0)),
            pl.BlockSpec((2, LANES, GLA_HK), lambda b, i: (0, 0, 0)),
            pl.BlockSpec((1, GLA_HK), lambda b, i: (0, 0)),
            pl.BlockSpec((1, GLA_DV), lambda b, i: (0, 0)),
        ],
        out_specs=pl.BlockSpec((r, GLA_HV), lambda b, i: (rowmap(b, i), 0)),
        out_shape=jax.ShapeDtypeStruct((t, GLA_HV), BF16),
        scratch_shapes=[pltpu.VMEM((GLA_HEADS, GLA_DV, GLA_DK), F32)],
        compiler_params=_cparams(2),
        name="gla_core",
    )(qkvg, qkvg, qkvg, qkvg, gr, wgk_pad, bgk, ng)


def _route_class(logits_t, bias_col):
    mx = jnp.max(logits_t, axis=0, keepdims=True)
    e = jnp.exp(logits_t - mx)
    scores = e / jnp.sum(e, axis=0, keepdims=True)
    sel = scores + bias_col
    rows = [sel[j:j + 1, :] for j in range(N_EXPERTS)]
    best_g = None
    best_s = None
    for g in range(N_GROUPS):
        m = rows[g * EPG:(g + 1) * EPG]
        gs = None
        for a in range(EPG):
            for b in range(a + 1, EPG):
                s = m[a] + m[b]
                gs = s if gs is None else jnp.maximum(gs, s)
        if g == 0:
            best_g = jnp.zeros(gs.shape, I32)
            best_s = gs
        else:
            better = gs > best_s
            best_g = jnp.where(better, g, best_g)
            best_s = jnp.where(better, gs, best_s)
    mem = []
    for j in range(EPG):
        vj = rows[j]
        for g in range(1, N_GROUPS):
            vj = jnp.where(best_g == g, rows[g * EPG + j], vj)
        mem.append(vj)
    i1 = jnp.zeros(best_g.shape, I32)
    b1 = mem[0]
    for j in range(1, EPG):
        better = mem[j] > b1
        i1 = jnp.where(better, j, i1)
        b1 = jnp.where(better, mem[j], b1)
    i2 = jnp.where(i1 == 0, 1, 0).astype(I32)
    b2 = jnp.where(i1 == 0, mem[1], mem[0])
    for j in range(1, EPG):
        better = (mem[j] > b2) & (i1 != j) & (i2 != j)
        i2 = jnp.where(better, j, i2)
        b2 = jnp.where(better, mem[j], b2)
    lo = jnp.minimum(i1, i2)
    hi = jnp.maximum(i1, i2)
    pair = jnp.where(lo == 0, hi - 1, jnp.where(lo == 1, hi + 1, 5))
    return best_g * N_PAIRS + pair


def _router_weight(router_w):
    hi = router_w.astype(BF16)
    lo = (router_w - hi.astype(F32)).astype(BF16)
    pad = jnp.zeros((router_w.shape[0], LANES - 2 * N_EXPERTS), BF16)
    return jnp.concatenate([hi, lo, pad], axis=1)


def _router_logits(x, w2_ref):
    hi = x.astype(BF16)
    lo = (x - hi.astype(F32)).astype(BF16)
    w2 = w2_ref[...]
    a = jnp.dot(hi, w2, preferred_element_type=F32) + jnp.dot(lo, w2, preferred_element_type=F32)
    return a + pltpu.roll(a, LANES - N_EXPERTS, axis=1)


def _post_kernel(o_ref, w_ref, h_ref, g_ref, b_ref, w2_ref, rb_ref, h1_ref, cls_ref):
    half = h_ref.shape[0] // 2
    for part in range(2):
        rs = slice(part * half, (part + 1) * half)
        mix = jnp.dot(o_ref[rs, :], w_ref[...], preferred_element_type=F32)
        h1 = _layer_norm(ALPHA * h_ref[rs, :] + mix, g_ref[...], b_ref[...])
        h1_ref[rs, :] = h1
        logits_t = _router_logits(h1, w2_ref).T[:N_EXPERTS, :]
        cls_ref[:, rs] = _route_class(logits_t, rb_ref[...])


def _post_mixer(o, w_out, h, ln_g, ln_b, w2, rb_col):
    t, d = h.shape
    tm = TM_POST
    kdim = o.shape[1]
    return pl.pallas_call(
        _post_kernel,
        grid=(t // tm,),
        in_specs=[
            pl.BlockSpec((tm, kdim), lambda i: (i, 0)),
            pl.BlockSpec((kdim, d), lambda i: (0, 0)),
            pl.BlockSpec((tm, d), lambda i: (i, 0)),
            pl.BlockSpec((1, d), lambda i: (0, 0)),
            pl.BlockSpec((1, d), lambda i: (0, 0)),
            pl.BlockSpec((d, LANES), lambda i: (0, 0)),
            pl.BlockSpec((N_EXPERTS, 1), lambda i: (0, 0)),
        ],
        out_specs=[pl.BlockSpec((tm, d), lambda i: (i, 0)),
                   pl.BlockSpec((1, tm), lambda i: (0, i))],
        out_shape=[jax.ShapeDtypeStruct((t, d), F32), jax.ShapeDtypeStruct((1, t), I32)],
        compiler_params=_cparams(1),
        name="post_mixer",
    )(o, w_out, h, ln_g, ln_b, w2, rb_col)


def _rank_kernel(cls_ref, pos_ref, toff_ref, cnt_ref, carry_ref, off_ref, tri_ref):
    phase = pl.program_id(0)
    i = pl.program_id(1)
    tm = cls_ref.shape[1]
    onehot = (lax.broadcasted_iota(I32, (CLASS_PAD, tm), 0) == cls_ref[...]).astype(F32)
    tile_count = jnp.sum(onehot, axis=1, keepdims=True)

    @pl.when((phase == 0) & (i == 0))
    def _():
        cnt_ref[...] = jnp.zeros_like(cnt_ref)
        r = lax.broadcasted_iota(I32, (tm, tm), 0)
        cc = lax.broadcasted_iota(I32, (tm, tm), 1)
        tri_ref[...] = (r <= cc).astype(BF16)

    @pl.when(phase == 0)
    def _():
        cnt_ref[...] += jnp.broadcast_to(tile_count, cnt_ref.shape)

    @pl.when((phase == 1) & (i == 0))
    def _():
        ntile = jnp.floor((cnt_ref[...] + (TM_MOE - 1)) * (1.0 / TM_MOE))
        r = lax.broadcasted_iota(I32, (CLASS_PAD, CLASS_PAD), 0)
        cc = lax.broadcasted_iota(I32, (CLASS_PAD, CLASS_PAD), 1)
        strict = (cc < r).astype(BF16)
        first_tile = jnp.dot(strict, ntile.astype(BF16), preferred_element_type=F32)
        toff_ref[...] = first_tile.astype(I32)
        off_ref[...] = first_tile * float(TM_MOE)
        carry_ref[...] = jnp.zeros_like(carry_ref)

    @pl.when(phase == 1)
    def _():
        prefix = jnp.dot(onehot.astype(BF16), tri_ref[...], preferred_element_type=F32)
        base = off_ref[:, 0:1] + carry_ref[:, 0:1] - 1.0
        posf = jnp.sum(onehot * (prefix + base), axis=0, keepdims=True)
        pos_ref[...] = posf.astype(I32)
        carry_ref[...] += jnp.broadcast_to(tile_count, carry_ref.shape)


def _rank(cls):
    t = cls.shape[1]
    tm = TM_RANK
    return pl.pallas_call(
        _rank_kernel,
        grid=(2, t // tm),
        in_specs=[pl.BlockSpec((1, tm), lambda p, i: (0, i))],
        out_specs=[pl.BlockSpec((1, tm), lambda p, i: (0, i * p)),
                   pl.BlockSpec((CLASS_PAD, LANES), lambda p, i: (0, 0))],
        out_shape=[jax.ShapeDtypeStruct((1, t), I32),
                   jax.ShapeDtypeStruct((CLASS_PAD, LANES), I32)],
        scratch_shapes=[pltpu.VMEM((CLASS_PAD, LANES), F32),
                        pltpu.VMEM((CLASS_PAD, LANES), F32),
                        pltpu.VMEM((CLASS_PAD, LANES), F32),
                        pltpu.VMEM((tm, tm), BF16)],
        compiler_params=_cparams(2),
        name="rank",
    )(cls)


def _dispatch_kernel(pos_ref, h_ref, xs_in_ref, xs_ref, sem):
    del xs_in_ref
    tm = h_ref.shape[0]
    base = pl.program_id(0) * tm

    for r in range(tm):
        p = pos_ref[base + r]
        pltpu.make_async_copy(h_ref.at[pl.ds(r, 1), :], xs_ref.at[pl.ds(p, 1), :],
                              sem).start(priority=r % 2)
    pltpu.make_async_copy(h_ref, xs_ref.at[pl.ds(0, tm), :], sem).wait()


def _dispatch(pos, h1, xs0):
    t, d = h1.shape
    tm = TM_ROWS
    n_sorted = xs0.shape[0]
    return pl.pallas_call(
        _dispatch_kernel,
        grid_spec=pltpu.PrefetchScalarGridSpec(
            num_scalar_prefetch=1,
            grid=(t // tm,),
            in_specs=[pl.BlockSpec((tm, d), lambda i, pos: (i, 0)),
                      pl.BlockSpec(memory_space=pl.ANY)],
            out_specs=pl.BlockSpec(memory_space=pl.ANY),
            scratch_shapes=[pltpu.SemaphoreType.DMA(())],
        ),
        out_shape=jax.ShapeDtypeStruct((n_sorted, d), F32),
        input_output_aliases={2: 0},
        compiler_params=_cparams(1),
        name="dispatch",
    )(pos, h1, xs0)


def _combine_kernel(pos_ref, h_ref, ys_ref, g_ref, b_ref, o_ref, buf_ref, sem):
    tm = h_ref.shape[0]
    base = pl.program_id(0) * tm

    for r in range(tm):
        p = pos_ref[base + r]
        pltpu.make_async_copy(ys_ref.at[pl.ds(p, 1), :], buf_ref.at[pl.ds(r, 1), :],
                              sem).start(priority=r % 2)
    pltpu.make_async_copy(ys_ref.at[pl.ds(0, tm), :], buf_ref, sem).wait()
    o_ref[...] = _layer_norm(ALPHA * h_ref[...] + buf_ref[...], g_ref[...], b_ref[...])


def _combine(pos, h1, ys, ln_g, ln_b):
    t, d = h1.shape
    tm = TM_ROWS
    return pl.pallas_call(
        _combine_kernel,
        grid_spec=pltpu.PrefetchScalarGridSpec(
            num_scalar_prefetch=1,
            grid=(t // tm,),
            in_specs=[pl.BlockSpec((tm, d), lambda i, pos: (i, 0)),
                      pl.BlockSpec(memory_space=pl.ANY),
                      pl.BlockSpec((1, d), lambda i, pos: (0, 0)),
                      pl.BlockSpec((1, d), lambda i, pos: (0, 0))],
            out_specs=pl.BlockSpec((tm, d), lambda i, pos: (i, 0)),
            scratch_shapes=[pltpu.VMEM((tm, d), F32), pltpu.SemaphoreType.DMA(())],
        ),
        out_shape=jax.ShapeDtypeStruct((t, d), F32),
        compiler_params=_cparams(1),
        name="combine",
    )(pos, h1, ys, ln_g, ln_b)


def _moe_kernel(ea_ref, eb_ref, valid_ref, xs_ref, w2_ref, wga_ref, wua_ref, wda_ref,
                wgb_ref, wub_ref, wdb_ref, ys_ref, wbf_up_ref, wbf_dn_ref):
    i = pl.program_id(0)
    prev = jnp.maximum(i - 1, 0)
    fresh = (i == 0) | (ea_ref[i] != ea_ref[prev]) | (eb_ref[i] != eb_ref[prev])

    @pl.when(fresh)
    def _():
        for slot, w_ref in enumerate((wga_ref, wua_ref, wgb_ref, wub_ref)):
            wbf_up_ref[slot] = w_ref[...].astype(BF16)
        for slot, w_ref in enumerate((wda_ref, wdb_ref)):
            wbf_dn_ref[slot] = w_ref[...].astype(BF16)

    @pl.when(valid_ref[i] == 0)
    def _():
        ys_ref[...] = jnp.zeros_like(ys_ref)

    @pl.when(valid_ref[i] != 0)
    def _():
        x = xs_ref[...]
        logits = _router_logits(x, w2_ref)
        lane = lax.broadcasted_iota(I32, logits.shape, 1)
        logits = jnp.where(lane < N_EXPERTS, logits, -jnp.inf)
        e = jnp.exp(logits - jnp.max(logits, axis=1, keepdims=True))
        scores = e / jnp.sum(e, axis=1, keepdims=True)
        sa = jnp.sum(jnp.where(lane == ea_ref[i], scores, 0.0), axis=1, keepdims=True)
        sb = jnp.sum(jnp.where(lane == eb_ref[i], scores, 0.0), axis=1, keepdims=True)
        tot = sa + sb
        xb = x.astype(BF16)

        def expert(slot, gate):
            a = jnp.dot(xb, wbf_up_ref[2 * slot], preferred_element_type=F32)
            u = jnp.dot(xb, wbf_up_ref[2 * slot + 1], preferred_element_type=F32)
            hid = (a * _sigmoid(a)) * u * gate
            return jnp.dot(hid.astype(BF16), wbf_dn_ref[slot], preferred_element_type=F32)

        ys_ref[...] = expert(0, sa / tot) + expert(1, sb / tot)


def _moe(ea, eb, valid, xs, w2, wg, wu, wd, layer):
    n_sorted, d = xs.shape
    tm = TM_MOE
    f = wg.shape[3]
    idx_a = lambda i, ea, eb, va: (layer, ea[i], 0, 0)
    idx_b = lambda i, ea, eb, va: (layer, eb[i], 0, 0)
    up_a = pl.BlockSpec((None, None, d, f), idx_a)
    up_b = pl.BlockSpec((None, None, d, f), idx_b)
    dn_a = pl.BlockSpec((None, None, f, d), idx_a)
    dn_b = pl.BlockSpec((None, None, f, d), idx_b)
    return pl.pallas_call(
        _moe_kernel,
        grid_spec=pltpu.PrefetchScalarGridSpec(
            num_scalar_prefetch=3,
            grid=(n_sorted // tm,),
            in_specs=[pl.BlockSpec((tm, d), lambda i, ea, eb, va: (i, 0)),
                      pl.BlockSpec((d, LANES), lambda i, ea, eb, va: (0, 0)),
                      up_a, up_a, dn_a, up_b, up_b, dn_b],
            out_specs=pl.BlockSpec((tm, d), lambda i, ea, eb, va: (i, 0)),
            scratch_shapes=[pltpu.VMEM((4, d, f), BF16), pltpu.VMEM((2, f, d), BF16)],
        ),
        out_shape=jax.ShapeDtypeStruct((n_sorted, d), F32),
        compiler_params=_cparams(1),
        name="moe",
    )(ea, eb, valid, xs, w2, wg, wu, wd, wg, wu, wd)


_PAIR_LO = (0, 0, 0, 1, 1, 2)
_PAIR_HI = (1, 2, 3, 2, 3, 3)


def _tile_plan(toff, n_tiles):
    first = toff[:N_CLASSES, 0]
    tiles = jnp.arange(n_tiles, dtype=I32)
    cls = jnp.sum((first[None, :] <= tiles[:, None]).astype(I32), axis=1) - 1
    group = cls // N_PAIRS
    pair = cls % N_PAIRS
    lo = jnp.asarray(_PAIR_LO, I32)[pair]
    hi = jnp.asarray(_PAIR_HI, I32)[pair]
    total = toff[N_CLASSES, 0]
    valid = (tiles < total).astype(I32)
    return group * EPG + lo, group * EPG + hi, valid


def _moe_block(h1, cls, w2, wg, wu, wd, layer, ln_g, ln_b, xs_buf):
    t, _ = h1.shape
    n_tiles = xs_buf.shape[0] // TM_MOE
    pos2d, toff = _rank(cls)
    pos = pos2d.reshape(t)
    ea, eb, valid = _tile_plan(toff, n_tiles)
    xs = _dispatch(pos, h1, xs_buf)
    ys = _moe(ea, eb, valid, xs, w2, wg, wu, wd, layer)
    return _combine(pos, h1, ys, ln_g, ln_b), xs


DEC_LANES = 6


def _decay_placement():
    pq = np.zeros((LANES, FOX_FD), np.float32)
    pk = np.zeros((LANES, FOX_FD), np.float32)
    oq = np.zeros((1, FOX_FD), np.float32)
    ok = np.zeros((1, FOX_FD), np.float32)
    for h in range(FOX_HEADS):
        base = (h // 2) * LANES + (h % 2) * DEC_LANES
        for part in range(3):
            pq[part * FOX_HEADS + h, base + part] = 1.0
            pk[part * FOX_HEADS + h, base + 3 + part] = -1.0
            oq[0, base + 3 + part] = 1.0
            ok[0, base + part] = 1.0
    return pq, pk, oq, ok


def _kv_kernel(x_ref, wk_ref, wv_ref, wf_ref, fb_ref, pq_ref, pk_ref, oq_ref, ok_ref,
               k_ref, v_ref, cq_ref, ck_ref, carry_ref, tril_ref, *, tiles_per_seq):
    i = pl.program_id(0)
    tm = x_ref.shape[0]

    @pl.when(i == 0)
    def _():
        r = lax.broadcasted_iota(I32, (tm, tm), 0)
        cc = lax.broadcasted_iota(I32, (tm, tm), 1)
        tril_ref[...] = (cc <= r).astype(BF16)

    @pl.when(i % tiles_per_seq == 0)
    def _():
        carry_ref[...] = jnp.zeros_like(carry_ref)

    xb = x_ref[...].astype(BF16)
    for w_ref, o_ref in ((wk_ref, k_ref), (wv_ref, v_ref)):
        n = o_ref.shape[1]
        for c in range(n // TN_PROJ):
            cs = slice(c * TN_PROJ, (c + 1) * TN_PROJ)
            o_ref[:, cs] = jnp.dot(xb, w_ref[:, cs], preferred_element_type=F32).astype(o_ref.dtype)
    fl = jnp.dot(xb, wf_ref[...], preferred_element_type=F32) + fb_ref[...]
    lane = lax.broadcasted_iota(I32, fl.shape, 1)
    log_f = jnp.where(lane < FOX_HEADS, _log_sigmoid(fl), 0.0)
    tril = tril_ref[...]
    cum = carry_ref[...]
    for part in _split3(log_f):
        cum = cum + jnp.dot(tril, part.astype(BF16), preferred_element_type=F32)
    carry_ref[...] = cum[tm - 1:tm, :]
    hi, mid, lo = _split3(cum * LOG2E)
    packed = hi + pltpu.roll(mid, FOX_HEADS, axis=1) + pltpu.roll(lo, 2 * FOX_HEADS, axis=1)
    packed = packed.astype(BF16)
    cq_ref[...] = (jnp.dot(packed, pq_ref[...], preferred_element_type=F32)
                   + oq_ref[...]).astype(cq_ref.dtype)
    ck_ref[...] = (jnp.dot(packed, pk_ref[...], preferred_element_type=F32)
                   + ok_ref[...]).astype(ck_ref.dtype)


def _fox_kv(h, wk, wv, wf_pad, fb_pad, seq):
    t, d = h.shape
    tm = TM_KV
    pq, pk, oq, ok = _decay_placement()
    full = lambda a: pl.BlockSpec(a.shape, lambda i: (0, 0))
    consts = [jnp.asarray(pq, BF16), jnp.asarray(pk, BF16), jnp.asarray(oq), jnp.asarray(ok)]
    row_out = pl.BlockSpec((tm, FOX_FD), lambda i: (i, 0))
    return pl.pallas_call(
        functools.partial(_kv_kernel, tiles_per_seq=seq // tm),
        grid=(t // tm,),
        in_specs=[pl.BlockSpec((tm, d), lambda i: (i, 0)), full(wk), full(wv), full(wf_pad),
                  full(fb_pad)] + [full(c) for c in consts],
        out_specs=[row_out, row_out, row_out, row_out],
        out_shape=[jax.ShapeDtypeStruct((t, FOX_FD), BF16)] * 4,
        scratch_shapes=[pltpu.VMEM((1, LANES), F32), pltpu.VMEM((tm, tm), BF16)],
        compiler_params=_cparams(1),
        name="fox_kv",
    )(h, wk, wv, wf_pad, fb_pad, *consts)


def _fox_kernel(q_ref, g_ref, cq_ref, k_ref, ck_ref, v_ref, o_ref, s_ref, mb_ref, m_ref, acc_ref):
    seq = q_ref.shape[0]
    tq = FOX_TQ
    tk = tq
    nq = seq // tq
    lane = lax.broadcasted_iota(I32, (tq, LANES), 1)
    klane = lax.broadcasted_iota(I32, (tk, LANES), 1)

    def augmented(qi):
        rows = slice(qi * tq, (qi + 1) * tq)
        q = q_ref[rows, :]
        cq = cq_ref[rows, :]
        out = []
        for hh in range(2):
            in_head = (lane >= hh * FOX_DH) & (lane < (hh + 1) * FOX_DH)
            in_dec = (lane >= hh * DEC_LANES) & (lane < (hh + 1) * DEC_LANES)
            out.append(jnp.concatenate([jnp.where(in_head, q, jnp.zeros_like(q)),
                                        jnp.where(in_dec, cq, jnp.zeros_like(cq))], axis=1))
        return out

    one_lane = (FOX_DH, 0)
    v_keep = [klane < FOX_DH, klane >= FOX_DH]

    hq = tq // 2
    top, bot = slice(0, hq), slice(hq, tq)
    nt_dims = (((1,), (1,)), ((), ()))

    def scores(q_augs, j, diagonal):
        ks = slice(j * tk, (j + 1) * tk)
        k_aug = jnp.concatenate([k_ref[ks, :], ck_ref[ks, :]], axis=1)
        for hh in range(2):
            if diagonal:
                s_ref[hh, top, 0:hq] = lax.dot_general(q_augs[hh][top], k_aug[0:hq], nt_dims,
                                                       preferred_element_type=F32)
                s_ref[hh, bot, :] = lax.dot_general(q_augs[hh][bot], k_aug, nt_dims,
                                                    preferred_element_type=F32)
            else:
                s = lax.dot_general(q_augs[hh], k_aug, nt_dims, preferred_element_type=F32)
                s_ref[hh] = s
                mb_ref[hh] = jnp.broadcast_to(jnp.max(s, axis=1, keepdims=True), (tq, LANES))

    def absorb(j, diagonal):
        ks = slice(j * tk, (j + 1) * tk)
        v = v_ref[ks, :]
        for hh in range(2):
            ones = (klane == one_lane[hh]).astype(BF16)
            v_aug = jnp.where(v_keep[hh], v, ones)
            m_old = m_ref[hh]
            if diagonal:
                r = lax.broadcasted_iota(I32, (hq, hq), 0)
                cc = lax.broadcasted_iota(I32, (hq, hq), 1)
                tri = cc <= r
                s_tt = jnp.where(tri, s_ref[hh, top, 0:hq], -jnp.inf)
                s_bt = s_ref[hh, bot, 0:hq]
                s_bb = jnp.where(tri, s_ref[hh, bot, hq:tk], -jnp.inf)
                reps = hq // LANES

                m_top = jnp.maximum(m_old[top], jnp.broadcast_to(
                    jnp.max(s_tt, axis=1, keepdims=True), (hq, LANES)))
                m_ref[hh, top, :] = m_top
                p_tt = jnp.exp2((s_tt - jnp.concatenate([m_top] * reps, axis=1)).astype(BF16))
                acc_ref[hh, top, :] = (jnp.exp2(m_old[top] - m_top) * acc_ref[hh, top, :]
                                       + jnp.dot(p_tt, v_aug[0:hq], preferred_element_type=F32))

                m_bot = jnp.maximum(m_old[bot], jnp.broadcast_to(
                    jnp.maximum(jnp.max(s_bt, axis=1, keepdims=True),
                                jnp.max(s_bb, axis=1, keepdims=True)), (hq, LANES)))
                m_ref[hh, bot, :] = m_bot
                m_rep = jnp.concatenate([m_bot] * reps, axis=1)
                p_bt = jnp.exp2((s_bt - m_rep).astype(BF16))
                p_bb = jnp.exp2((s_bb - m_rep).astype(BF16))
                acc_ref[hh, bot, :] = (jnp.exp2(m_old[bot] - m_bot) * acc_ref[hh, bot, :]
                                       + jnp.dot(p_bt, v_aug[0:hq], preferred_element_type=F32)
                                       + jnp.dot(p_bb, v_aug[hq:tk], preferred_element_type=F32))
            else:
                m_new = jnp.maximum(m_old, mb_ref[hh])
                m_ref[hh] = m_new
                alpha = jnp.exp2(m_old - m_new)
                p = jnp.exp2((s_ref[hh] - jnp.concatenate([m_new] * (tk // LANES), axis=1)).astype(BF16))
                acc_ref[hh] = alpha * acc_ref[hh] + jnp.dot(p, v_aug, preferred_element_type=F32)

    def reset():
        m_ref[...] = jnp.full(m_ref.shape, -jnp.inf, F32)
        acc_ref[...] = jnp.zeros(acc_ref.shape, F32)

    def finish(qi):
        rows = slice(qi * tq, (qi + 1) * tq)
        acc0 = acc_ref[0]
        acc1 = acc_ref[1]
        o = jnp.where(lane < FOX_DH, acc0 / acc0[:, FOX_DH:FOX_DH + 1], acc1 / acc1[:, 0:1])
        o_ref[rows, :] = (o * _sigmoid(g_ref[rows, :].astype(F32))).astype(o_ref.dtype)

    pairs = [(qi, j) for qi in range(nq) for j in range(qi + 1)]
    q_augs = augmented(0)
    scores(q_augs, 0, True)
    for n, (qi, j) in enumerate(pairs):
        if j == 0:
            reset()
        absorb(j, j == qi)
        if n + 1 < len(pairs):
            qn, jn = pairs[n + 1]
            if qn != qi:
                q_augs = augmented(qn)
            scores(q_augs, jn, jn == qn)
        if j == qi:
            finish(qi)


def _fox_attn(qg, k, v, cq, ck, batch, seq):
    t = batch * seq
    n_pairs = FOX_FD // LANES
    blk = lambda m: pl.BlockSpec((seq, LANES), m)
    pair_blk = lambda b, p: (b, p)
    return pl.pallas_call(
        _fox_kernel,
        grid=(batch, n_pairs),
        in_specs=[blk(pair_blk),
                  blk(lambda b, p: (b, n_pairs + p)),
                  blk(pair_blk),
                  blk(pair_blk), blk(pair_blk), blk(pair_blk)],
        out_specs=blk(pair_blk),
        out_shape=jax.ShapeDtypeStruct((t, FOX_FD), BF16),
        scratch_shapes=[pltpu.VMEM((2, FOX_TQ, FOX_TQ), F32),
                        pltpu.VMEM((2, FOX_TQ, LANES), F32),
                        pltpu.VMEM((2, FOX_TQ, LANES), F32),
                        pltpu.VMEM((2, FOX_TQ, LANES), F32)],
        compiler_params=_cparams(2),
        name="fox_attn",
    )(qg, qg, cq, k, ck, v)


def kernel(x, gla_w_in, gla_w_gk, gla_b_gk, gla_norm_g, gla_w_out, kv_w, forget_bias, fox_w_qg,
           fox_w_out, router_w, router_bias, moe_w_gate, moe_w_up, moe_w_down, ln_g, ln_b):
    batch, seq, d = x.shape
    t = batch * seq
    h = x.reshape(t, d)
    w2 = _router_weight(router_w)
    rb_col = router_bias.reshape(N_EXPERTS, 1)
    n_main = 2 * GLA_HK + 2 * GLA_HV
    k_sh = v_sh = cq_sh = ck_sh = None
    xs_buf = jnp.zeros(((t // TM_MOE + N_CLASSES) * TM_MOE, d), F32)
    for layer in range(DEPTH):
        if layer < N_A_LAYERS:
            w_in = gla_w_in[layer]
            w_main = w_in[:, :n_main].astype(BF16)
            w_gr = jnp.pad(w_in[:, n_main:], ((0, 0), (0, LANES - GLA_RANK))).astype(BF16)
            qkvg, gr = _proj(h, [w_main, w_gr], [BF16, F32])
            wgk = jnp.pad(gla_w_gk[layer], ((0, LANES - GLA_RANK), (0, 0)))
            wgk_hi = wgk.astype(BF16)
            wgk_pad = jnp.stack([wgk_hi, (wgk - wgk_hi.astype(F32)).astype(BF16)])
            o = _gla_core(qkvg, gr, wgk_pad, gla_b_gk[layer].reshape(1, GLA_HK),
                          gla_norm_g[layer].reshape(1, GLA_DV), batch, seq)
            w_out = gla_w_out[layer].astype(BF16)
        else:
            j = layer - N_A_LAYERS
            w_qg = fox_w_qg[j]
            w_qg = jnp.concatenate([w_qg[:, :FOX_FD] * (FOX_DH ** -0.5 * LOG2E), w_qg[:, FOX_FD:]], axis=1)
            (qg,) = _proj(h, [w_qg.astype(BF16)], [BF16])
            o = _fox_attn(qg, k_sh, v_sh, cq_sh, ck_sh, batch, seq)
            w_out = fox_w_out[j].astype(BF16)
        h1, cls = _post_mixer(o, w_out, h, ln_g[layer, 0].reshape(1, d), ln_b[layer, 0].reshape(1, d),
                              w2, rb_col)
        h, xs_buf = _moe_block(h1, cls, w2, moe_w_gate, moe_w_up, moe_w_down, layer,
                               ln_g[layer, 1].reshape(1, d), ln_b[layer, 1].reshape(1, d), xs_buf)
        if layer == N_A_LAYERS - 1:
            wf_pad = jnp.pad(kv_w[:, 2 * FOX_FD:], ((0, 0), (0, LANES - FOX_HEADS))).astype(BF16)
            fb_pad = jnp.pad(forget_bias, (0, LANES - FOX_HEADS)).reshape(1, LANES)
            k_sh, v_sh, cq_sh, ck_sh = _fox_kv(h, kv_w[:, :FOX_FD].astype(BF16),
                                            kv_w[:, FOX_FD:2 * FOX_FD].astype(BF16), wf_pad, fb_pad, seq)
    return h.reshape(batch, seq, d)
```

```python
import functools
import math

import jax
import jax.numpy as jnp
import numpy as np
from jax import lax
from jax.experimental import pallas as pl
from jax.experimental.pallas import tpu as pltpu

F32 = jnp.float32
BF16 = jnp.bfloat16
I32 = jnp.int32

D_MODEL = 1024
DEPTH = 4
N_A_LAYERS = DEPTH // 2

GLA_HEADS = 4
GLA_DK = 128
GLA_DV = 256
GLA_RANK = 16
GLA_GATE_NORM = 16.0
GLA_CHUNK = 64
GLA_HK = GLA_HEADS * GLA_DK
GLA_HV = GLA_HEADS * GLA_DV

FOX_HEADS = 16
FOX_DH = 64
FOX_FD = FOX_HEADS * FOX_DH

N_EXPERTS = 16
N_GROUPS = 4
EPG = 4
D_EXPERT = 512
N_PAIRS = 6
N_CLASSES = N_GROUPS * N_PAIRS
CLASS_PAD = 32

ALPHA = float((2 * DEPTH) ** 0.25)
LN_EPS = 1e-5
RMS_EPS = 1e-6
LOG2E = 1.4426950408889634

LANES = 128
VMEM_LIMIT = 48 * 1024 * 1024

TM_PROJ = 1024
TM_KV = 512
TN_PROJ = 512
GLA_ROWS = 512
TM_POST = 512
TM_RANK = 512
TM_ROWS = 512
TM_MOE = 512
FOX_TQ = 1024


def _cparams(n_axes):
    return pltpu.CompilerParams(dimension_semantics=("arbitrary",) * n_axes,
                                vmem_limit_bytes=VMEM_LIMIT)


def _log_sigmoid(x):
    return jnp.minimum(x, 0.0) - jnp.log(1.0 + jnp.exp(-jnp.abs(x)))


def _sigmoid(x):
    return 1.0 / (1.0 + jnp.exp(-x))


def _layer_norm(z, g, b):
    mu = jnp.mean(z, axis=-1, keepdims=True)
    zc = z - mu
    var = jnp.mean(zc * zc, axis=-1, keepdims=True)
    return zc * lax.rsqrt(var + LN_EPS) * g + b


def _split3(x):
    hi = x.astype(BF16).astype(F32)
    r = x - hi
    mid = r.astype(BF16).astype(F32)
    lo = (r - mid).astype(BF16).astype(F32)
    return hi, mid, lo


def _proj_kernel(x_ref, *refs, n_w):
    xb = x_ref[...].astype(BF16)
    for w_ref, o_ref in zip(refs[:n_w], refs[n_w:]):
        n = o_ref.shape[1]
        tn = min(TN_PROJ, n)
        for c in range(n // tn):
            o_ref[:, c * tn:(c + 1) * tn] = jnp.dot(
                xb, w_ref[:, c * tn:(c + 1) * tn], preferred_element_type=F32).astype(o_ref.dtype)


def _proj(x, ws, out_dtypes):
    t, k = x.shape
    tm = TM_PROJ
    return pl.pallas_call(
        functools.partial(_proj_kernel, n_w=len(ws)),
        grid=(t // tm,),
        in_specs=[pl.BlockSpec((tm, k), lambda i: (i, 0))]
        + [pl.BlockSpec(w.shape, lambda i: (0, 0)) for w in ws],
        out_specs=[pl.BlockSpec((tm, w.shape[1]), lambda i: (i, 0)) for w in ws],
        out_shape=[jax.ShapeDtypeStruct((t, w.shape[1]), dt) for w, dt in zip(ws, out_dtypes)],
        compiler_params=_cparams(1),
        name="proj",
    )(x, *ws)


def _gla_kernel(q_ref, k_ref, v_ref, g_ref, gr_ref, wgk_ref, bgk_ref, ng_ref, o_ref, state_ref):
    c = GLA_CHUNK

    @pl.when(pl.program_id(1) == 0)
    def _():
        state_ref[...] = jnp.zeros_like(state_ref)

    gr = gr_ref[...]
    gr_hi = gr.astype(BF16)
    gr_lo = (gr - gr_hi.astype(F32)).astype(BF16)
    w_hi = wgk_ref[0]
    gkz = (jnp.dot(gr_hi, w_hi, preferred_element_type=F32)
           + jnp.dot(gr_lo, w_hi, preferred_element_type=F32)
           + jnp.dot(gr_hi, wgk_ref[1], preferred_element_type=F32)) + bgk_ref[...]
    gk = _log_sigmoid(gkz) * (1.0 / GLA_GATE_NORM)
    row = lax.broadcasted_iota(I32, (c, c), 0)
    col = lax.broadcasted_iota(I32, (c, c), 1)
    causal = col <= row
    tril = jnp.where(causal, 1.0, 0.0).astype(BF16)
    gk_parts = [part.astype(BF16) for part in _split3(gk)]
    scale = GLA_DK ** -0.5
    ng = ng_ref[...]

    for ci in range(GLA_ROWS // c):
        rs = slice(ci * c, (ci + 1) * c)
        bc = sum(jnp.dot(tril, part[rs], preferred_element_type=F32) for part in gk_parts)
        b_last = bc[c - 1:c, :]
        qf = q_ref[rs, :].astype(F32)
        kf = k_ref[rs, :].astype(F32)
        q_dec = (qf * scale * jnp.exp(bc)).astype(BF16)
        k_inv = (kf * jnp.exp(-bc)).astype(BF16)
        k_end = (kf * jnp.exp(b_last - bc)).astype(BF16)
        dec = jnp.exp(b_last)
        for h in range(GLA_HEADS):
            ks = slice(h * GLA_DK, (h + 1) * GLA_DK)
            vs = slice(h * GLA_DV, (h + 1) * GLA_DV)
            v_h = v_ref[rs, vs]
            attn = lax.dot_general(q_dec[:, ks], k_inv[:, ks], (((1,), (1,)), ((), ())),
                                   preferred_element_type=F32)
            attn = jnp.where(causal, attn, 0.0).astype(BF16)
            st = state_ref[h]
            o = jnp.dot(attn, v_h, preferred_element_type=F32)
            o = o + lax.dot_general(q_dec[:, ks], st.astype(BF16), (((1,), (1,)), ((), ())),
                                    preferred_element_type=F32)
            kv_t = lax.dot_general(v_h, k_end[:, ks], (((0,), (0,)), ((), ())),
                                   preferred_element_type=F32)
            state_ref[h] = st * dec[:, ks] + kv_t
            o = o * lax.rsqrt(jnp.mean(o * o, axis=-1, keepdims=True) + RMS_EPS) * ng
            gate = g_ref[rs, vs].astype(F32)
            o = o * (gate * _sigmoid(gate))
            o_ref[rs, vs] = o.astype(o_ref.dtype)


def _gla_core(qkvg, gr, wgk_pad, bgk, ng, batch, seq):
    t = batch * seq
    r = GLA_ROWS
    nblk = seq // r
    rowmap = lambda b, i: b * nblk + i
    return pl.pallas_call(
        _gla_kernel,
        grid=(batch, nblk),
        in_specs=[
            pl.BlockSpec((r, GLA_HK), lambda b, i: (rowmap(b, i), 0)),
            pl.BlockSpec((r, GLA_HK), lambda b, i: (rowmap(b, i), 1)),
            pl.BlockSpec((r, GLA_HV), lambda b, i: (rowmap(b, i), 1)),
            pl.BlockSpec((r, GLA_HV), lambda b, i: (rowmap(b, i), 2)),
            pl.BlockSpec((r, LANES), lambda b, i: (rowmap(b, i), 0)),
            pl.BlockSpec((2, LANES, GLA_HK), lambda b, i: (0, 0, 0)),
            pl.BlockSpec((1, GLA_HK), lambda b, i: (0, 0)),
            pl.BlockSpec((1, GLA_DV), lambda b, i: (0, 0)),
        ],
        out_specs=pl.BlockSpec((r, GLA_HV), lambda b, i: (rowmap(b, i), 0)),
        out_shape=jax.ShapeDtypeStruct((t, GLA_HV), BF16),
        scratch_shapes=[pltpu.VMEM((GLA_HEADS, GLA_DV, GLA_DK), F32)],
        compiler_params=_cparams(2),
        name="gla_core",
    )(qkvg, qkvg, qkvg, qkvg, gr, wgk_pad, bgk, ng)


def _route_class(logits_t, bias_col):
    mx = jnp.max(logits_t, axis=0, keepdims=True)
    e = jnp.exp(logits_t - mx)
    scores = e / jnp.sum(e, axis=0, keepdims=True)
    sel = scores + bias_col
    rows = [sel[j:j + 1, :] for j in range(N_EXPERTS)]
    best_g = None
    best_s = None
    for g in range(N_GROUPS):
        m = rows[g * EPG:(g + 1) * EPG]
        gs = None
        for a in range(EPG):
            for b in range(a + 1, EPG):
                s = m[a] + m[b]
                gs = s if gs is None else jnp.maximum(gs, s)
        if g == 0:
            best_g = jnp.zeros(gs.shape, I32)
            best_s = gs
        else:
            better = gs > best_s
            best_g = jnp.where(better, g, best_g)
            best_s = jnp.where(better, gs, best_s)
    mem = []
    for j in range(EPG):
        vj = rows[j]
        for g in range(1, N_GROUPS):
            vj = jnp.where(best_g == g, rows[g * EPG + j], vj)
        mem.append(vj)
    i1 = jnp.zeros(best_g.shape, I32)
    b1 = mem[0]
    for j in range(1, EPG):
        better = mem[j] > b1
        i1 = jnp.where(better, j, i1)
        b1 = jnp.where(better, mem[j], b1)
    i2 = jnp.where(i1 == 0, 1, 0).astype(I32)
    b2 = jnp.where(i1 == 0, mem[1], mem[0])
    for j in range(1, EPG):
        better = (mem[j] > b2) & (i1 != j) & (i2 != j)
        i2 = jnp.where(better, j, i2)
        b2 = jnp.where(better, mem[j], b2)
    lo = jnp.minimum(i1, i2)
    hi = jnp.maximum(i1, i2)
    pair = jnp.where(lo == 0, hi - 1, jnp.where(lo == 1, hi + 1, 5))
    return best_g * N_PAIRS + pair


def _router_weight(router_w):
    hi = router_w.astype(BF16)
    lo = (router_w - hi.astype(F32)).astype(BF16)
    pad = jnp.zeros((router_w.shape[0], LANES - 2 * N_EXPERTS), BF16)
    return jnp.concatenate([hi, lo, pad], axis=1)


def _router_logits(x, w2_ref):
    hi = x.astype(BF16)
    lo = (x - hi.astype(F32)).astype(BF16)
    w2 = w2_ref[...]
    a = jnp.dot(hi, w2, preferred_element_type=F32) + jnp.dot(lo, w2, preferred_element_type=F32)
    return a + pltpu.roll(a, LANES - N_EXPERTS, axis=1)


def _post_kernel(o_ref, w_ref, h_ref, g_ref, b_ref, w2_ref, rb_ref, h1_ref, cls_ref):
    half = h_ref.shape[0] // 2
    for part in range(2):
        rs = slice(part * half, (part + 1) * half)
        mix = jnp.dot(o_ref[rs, :], w_ref[...], preferred_element_type=F32)
        h1 = _layer_norm(ALPHA * h_ref[rs, :] + mix, g_ref[...], b_ref[...])
        h1_ref[rs, :] = h1
        logits_t = _router_logits(h1, w2_ref).T[:N_EXPERTS, :]
        cls_ref[:, rs] = _route_class(logits_t, rb_ref[...])


def _post_mixer(o, w_out, h, ln_g, ln_b, w2, rb_col):
    t, d = h.shape
    tm = TM_POST
    kdim = o.shape[1]
    return pl.pallas_call(
        _post_kernel,
        grid=(t // tm,),
        in_specs=[
            pl.BlockSpec((tm, kdim), lambda i: (i, 0)),
            pl.BlockSpec((kdim, d), lambda i: (0, 0)),
            pl.BlockSpec((tm, d), lambda i: (i, 0)),
            pl.BlockSpec((1, d), lambda i: (0, 0)),
            pl.BlockSpec((1, d), lambda i: (0, 0)),
            pl.BlockSpec((d, LANES), lambda i: (0, 0)),
            pl.BlockSpec((N_EXPERTS, 1), lambda i: (0, 0)),
        ],
        out_specs=[pl.BlockSpec((tm, d), lambda i: (i, 0)),
                   pl.BlockSpec((1, tm), lambda i: (0, i))],
        out_shape=[jax.ShapeDtypeStruct((t, d), F32), jax.ShapeDtypeStruct((1, t), I32)],
        compiler_params=_cparams(1),
        name="post_mixer",
    )(o, w_out, h, ln_g, ln_b, w2, rb_col)


def _rank_kernel(cls_ref, pos_ref, toff_ref, cnt_ref, carry_ref, off_ref, tri_ref):
    phase = pl.program_id(0)
    i = pl.program_id(1)
    tm = cls_ref.shape[1]
    onehot = (lax.broadcasted_iota(I32, (CLASS_PAD, tm), 0) == cls_ref[...]).astype(F32)
    tile_count = jnp.sum(onehot, axis=1, keepdims=True)

    @pl.when((phase == 0) & (i == 0))
    def _():
        cnt_ref[...] = jnp.zeros_like(cnt_ref)
        r = lax.broadcasted_iota(I32, (tm, tm), 0)
        cc = lax.broadcasted_iota(I32, (tm, tm), 1)
        tri_ref[...] = (r <= cc).astype(BF16)

    @pl.when(phase == 0)
    def _():
        cnt_ref[...] += jnp.broadcast_to(tile_count, cnt_ref.shape)

    @pl.when((phase == 1) & (i == 0))
    def _():
        ntile = jnp.floor((cnt_ref[...] + (TM_MOE - 1)) * (1.0 / TM_MOE))
        r = lax.broadcasted_iota(I32, (CLASS_PAD, CLASS_PAD), 0)
        cc = lax.broadcasted_iota(I32, (CLASS_PAD, CLASS_PAD), 1)
        strict = (cc < r).astype(BF16)
        first_tile = jnp.dot(strict, ntile.astype(BF16), preferred_element_type=F32)
        toff_ref[...] = first_tile.astype(I32)
        off_ref[...] = first_tile * float(TM_MOE)
        carry_ref[...] = jnp.zeros_like(carry_ref)

    @pl.when(phase == 1)
    def _():
        prefix = jnp.dot(onehot.astype(BF16), tri_ref[...], preferred_element_type=F32)
        base = off_ref[:, 0:1] + carry_ref[:, 0:1] - 1.0
        posf = jnp.sum(onehot * (prefix + base), axis=0, keepdims=True)
        pos_ref[...] = posf.astype(I32)
        carry_ref[...] += jnp.broadcast_to(tile_count, carry_ref.shape)


def _rank(cls):
    t = cls.shape[1]
    tm = TM_RANK
    return pl.pallas_call(
        _rank_kernel,
        grid=(2, t // tm),
        in_specs=[pl.BlockSpec((1, tm), lambda p, i: (0, i))],
        out_specs=[pl.BlockSpec((1, tm), lambda p, i: (0, i * p)),
                   pl.BlockSpec((CLASS_PAD, LANES), lambda p, i: (0, 0))],
        out_shape=[jax.ShapeDtypeStruct((1, t), I32),
                   jax.ShapeDtypeStruct((CLASS_PAD, LANES), I32)],
        scratch_shapes=[pltpu.VMEM((CLASS_PAD, LANES), F32),
                        pltpu.VMEM((CLASS_PAD, LANES), F32),
                        pltpu.VMEM((CLASS_PAD, LANES), F32),
                        pltpu.VMEM((tm, tm), BF16)],
        compiler_params=_cparams(2),
        name="rank",
    )(cls)


def _dispatch_kernel(pos_ref, h_ref, xs_in_ref, xs_ref, sem):
    del xs_in_ref
    tm = h_ref.shape[0]
    base = pl.program_id(0) * tm

    for r in range(tm):
        p = pos_ref[base + r]
        pltpu.make_async_copy(h_ref.at[pl.ds(r, 1), :], xs_ref.at[pl.ds(p, 1), :],
                              sem).start(priority=r % 2)
    pltpu.make_async_copy(h_ref, xs_ref.at[pl.ds(0, tm), :], sem).wait()


def _dispatch(pos, h1, xs0):
    t, d = h1.shape
    tm = TM_ROWS
    n_sorted = xs0.shape[0]
    return pl.pallas_call(
        _dispatch_kernel,
        grid_spec=pltpu.PrefetchScalarGridSpec(
            num_scalar_prefetch=1,
            grid=(t // tm,),
            in_specs=[pl.BlockSpec((tm, d), lambda i, pos: (i, 0)),
                      pl.BlockSpec(memory_space=pl.ANY)],
            out_specs=pl.BlockSpec(memory_space=pl.ANY),
            scratch_shapes=[pltpu.SemaphoreType.DMA(())],
        ),
        out_shape=jax.ShapeDtypeStruct((n_sorted, d), F32),
        input_output_aliases={2: 0},
        compiler_params=_cparams(1),
        name="dispatch",
    )(pos, h1, xs0)


def _combine_kernel(pos_ref, h_ref, ys_ref, g_ref, b_ref, o_ref, buf_ref, sem):
    tm = h_ref.shape[0]
    base = pl.program_id(0) * tm

    for r in range(tm):
        p = pos_ref[base + r]
        pltpu.make_async_copy(ys_ref.at[pl.ds(p, 1), :], buf_ref.at[pl.ds(r, 1), :],
                              sem).start(priority=r % 2)
    pltpu.make_async_copy(ys_ref.at[pl.ds(0, tm), :], buf_ref, sem).wait()
    o_ref[...] = _layer_norm(ALPHA * h_ref[...] + buf_ref[...], g_ref[...], b_ref[...])


def _combine(pos, h1, ys, ln_g, ln_b):
    t, d = h1.shape
    tm = TM_ROWS
    return pl.pallas_call(
        _combine_kernel,
        grid_spec=pltpu.PrefetchScalarGridSpec(
            num_scalar_prefetch=1,
            grid=(t // tm,),
            in_specs=[pl.BlockSpec((tm, d), lambda i, pos: (i, 0)),
                      pl.BlockSpec(memory_space=pl.ANY),
                      pl.BlockSpec((1, d), lambda i, pos: (0, 0)),
                      pl.BlockSpec((1, d), lambda i, pos: (0, 0))],
            out_specs=pl.BlockSpec((tm, d), lambda i, pos: (i, 0)),
            scratch_shapes=[pltpu.VMEM((tm, d), F32), pltpu.SemaphoreType.DMA(())],
        ),
        out_shape=jax.ShapeDtypeStruct((t, d), F32),
        compiler_params=_cparams(1),
        name="combine",
    )(pos, h1, ys, ln_g, ln_b)


def _moe_kernel(ea_ref, eb_ref, valid_ref, xs_ref, w2_ref, wga_ref, wua_ref, wda_ref,
                wgb_ref, wub_ref, wdb_ref, ys_ref, wbf_up_ref, wbf_dn_ref):
    i = pl.program_id(0)
    prev = jnp.maximum(i - 1, 0)
    fresh = (i == 0) | (ea_ref[i] != ea_ref[prev]) | (eb_ref[i] != eb_ref[prev])

    @pl.when(fresh)
    def _():
        for slot, w_ref in enumerate((wga_ref, wua_ref, wgb_ref, wub_ref)):
            wbf_up_ref[slot] = w_ref[...].astype(BF16)
        for slot, w_ref in enumerate((wda_ref, wdb_ref)):
            wbf_dn_ref[slot] = w_ref[...].astype(BF16)

    @pl.when(valid_ref[i] == 0)
    def _():
        ys_ref[...] = jnp.zeros_like(ys_ref)

    @pl.when(valid_ref[i] != 0)
    def _():
        x = xs_ref[...]
        logits = _router_logits(x, w2_ref)
        lane = lax.broadcasted_iota(I32, logits.shape, 1)
        logits = jnp.where(lane < N_EXPERTS, logits, -jnp.inf)
        e = jnp.exp(logits - jnp.max(logits, axis=1, keepdims=True))
        scores = e / jnp.sum(e, axis=1, keepdims=True)
        sa = jnp.sum(jnp.where(lane == ea_ref[i], scores, 0.0), axis=1, keepdims=True)
        sb = jnp.sum(jnp.where(lane == eb_ref[i], scores, 0.0), axis=1, keepdims=True)
        tot = sa + sb
        xb = x.astype(BF16)

        def expert(slot, gate):
            a = jnp.dot(xb, wbf_up_ref[2 * slot], preferred_element_type=F32)
            u = jnp.dot(xb, wbf_up_ref[2 * slot + 1], preferred_element_type=F32)
            hid = (a * _sigmoid(a)) * u * gate
            return jnp.dot(hid.astype(BF16), wbf_dn_ref[slot], preferred_element_type=F32)

        ys_ref[...] = expert(0, sa / tot) + expert(1, sb / tot)


def _moe(ea, eb, valid, xs, w2, wg, wu, wd, layer):
    n_sorted, d = xs.shape
    tm = TM_MOE
    f = wg.shape[3]
    idx_a = lambda i, ea, eb, va: (layer, ea[i], 0, 0)
    idx_b = lambda i, ea, eb, va: (layer, eb[i], 0, 0)
    up_a = pl.BlockSpec((None, None, d, f), idx_a)
    up_b = pl.BlockSpec((None, None, d, f), idx_b)
    dn_a = pl.BlockSpec((None, None, f, d), idx_a)
    dn_b = pl.BlockSpec((None, None, f, d), idx_b)
    return pl.pallas_call(
        _moe_kernel,
        grid_spec=pltpu.PrefetchScalarGridSpec(
            num_scalar_prefetch=3,
            grid=(n_sorted // tm,),
            in_specs=[pl.BlockSpec((tm, d), lambda i, ea, eb, va: (i, 0)),
                      pl.BlockSpec((d, LANES), lambda i, ea, eb, va: (0, 0)),
                      up_a, up_a, dn_a, up_b, up_b, dn_b],
            out_specs=pl.BlockSpec((tm, d), lambda i, ea, eb, va: (i, 0)),
            scratch_shapes=[pltpu.VMEM((4, d, f), BF16), pltpu.VMEM((2, f, d), BF16)],
        ),
        out_shape=jax.ShapeDtypeStruct((n_sorted, d), F32),
        compiler_params=_cparams(1),
        name="moe",
    )(ea, eb, valid, xs, w2, wg, wu, wd, wg, wu, wd)


_PAIR_LO = (0, 0, 0, 1, 1, 2)
_PAIR_HI = (1, 2, 3, 2, 3, 3)


def _tile_plan(toff, n_tiles):
    first = toff[:N_CLASSES, 0]
    tiles = jnp.arange(n_tiles, dtype=I32)
    cls = jnp.sum((first[None, :] <= tiles[:, None]).astype(I32), axis=1) - 1
    group = cls // N_PAIRS
    pair = cls % N_PAIRS
    lo = jnp.asarray(_PAIR_LO, I32)[pair]
    hi = jnp.asarray(_PAIR_HI, I32)[pair]
    total = toff[N_CLASSES, 0]
    valid = (tiles < total).astype(I32)
    return group * EPG + lo, group * EPG + hi, valid


def _moe_block(h1, cls, w2, wg, wu, wd, layer, ln_g, ln_b, xs_buf):
    t, _ = h1.shape
    n_tiles = xs_buf.shape[0] // TM_MOE
    pos2d, toff = _rank(cls)
    pos = pos2d.reshape(t)
    ea, eb, valid = _tile_plan(toff, n_tiles)
    xs = _dispatch(pos, h1, xs_buf)
    ys = _moe(ea, eb, valid, xs, w2, wg, wu, wd, layer)
    return _combine(pos, h1, ys, ln_g, ln_b), xs


DEC_LANES = 6


def _decay_placement():
    pq = np.zeros((LANES, FOX_FD), np.float32)
    pk = np.zeros((LANES, FOX_FD), np.float32)
    oq = np.zeros((1, FOX_FD), np.float32)
    ok = np.zeros((1, FOX_FD), np.float32)
    for h in range(FOX_HEADS):
        base = (h // 2) * LANES + (h % 2) * DEC_LANES
        for part in range(3):
            pq[part * FOX_HEADS + h, base + part] = 1.0
            pk[part * FOX_HEADS + h, base + 3 + part] = -1.0
            oq[0, base + 3 + part] = 1.0
            ok[0, base + part] = 1.0
    return pq, pk, oq, ok


def _kv_kernel(x_ref, wk_ref, wv_ref, wf_ref, fb_ref, pq_ref, pk_ref, oq_ref, ok_ref,
               k_ref, v_ref, cq_ref, ck_ref, carry_ref, tril_ref, *, tiles_per_seq):
    i = pl.program_id(0)
    tm = x_ref.shape[0]

    @pl.when(i == 0)
    def _():
        r = lax.broadcasted_iota(I32, (tm, tm), 0)
        cc = lax.broadcasted_iota(I32, (tm, tm), 1)
        tril_ref[...] = (cc <= r).astype(BF16)

    @pl.when(i % tiles_per_seq == 0)
    def _():
        carry_ref[...] = jnp.zeros_like(carry_ref)

    xb = x_ref[...].astype(BF16)
    for w_ref, o_ref in ((wk_ref, k_ref), (wv_ref, v_ref)):
        n = o_ref.shape[1]
        for c in range(n // TN_PROJ):
            cs = slice(c * TN_PROJ, (c + 1) * TN_PROJ)
            o_ref[:, cs] = jnp.dot(xb, w_ref[:, cs], preferred_element_type=F32).astype(o_ref.dtype)
    fl = jnp.dot(xb, wf_ref[...], preferred_element_type=F32) + fb_ref[...]
    lane = lax.broadcasted_iota(I32, fl.shape, 1)
    log_f = jnp.where(lane < FOX_HEADS, _log_sigmoid(fl), 0.0)
    tril = tril_ref[...]
    cum = carry_ref[...]
    for part in _split3(log_f):
        cum = cum + jnp.dot(tril, part.astype(BF16), preferred_element_type=F32)
    carry_ref[...] = cum[tm - 1:tm, :]
    hi, mid, lo = _split3(cum * LOG2E)
    packed = hi + pltpu.roll(mid, FOX_HEADS, axis=1) + pltpu.roll(lo, 2 * FOX_HEADS, axis=1)
    packed = packed.astype(BF16)
    cq_ref[...] = (jnp.dot(packed, pq_ref[...], preferred_element_type=F32)
                   + oq_ref[...]).astype(cq_ref.dtype)
    ck_ref[...] = (jnp.dot(packed, pk_ref[...], preferred_element_type=F32)
                   + ok_ref[...]).astype(ck_ref.dtype)


def _fox_kv(h, wk, wv, wf_pad, fb_pad, seq):
    t, d = h.shape
    tm = TM_KV
    pq, pk, oq, ok = _decay_placement()
    full = lambda a: pl.BlockSpec(a.shape, lambda i: (0, 0))
    consts = [jnp.asarray(pq, BF16), jnp.asarray(pk, BF16), jnp.asarray(oq), jnp.asarray(ok)]
    row_out = pl.BlockSpec((tm, FOX_FD), lambda i: (i, 0))
    return pl.pallas_call(
        functools.partial(_kv_kernel, tiles_per_seq=seq // tm),
        grid=(t // tm,),
        in_specs=[pl.BlockSpec((tm, d), lambda i: (i, 0)), full(wk), full(wv), full(wf_pad),
                  full(fb_pad)] + [full(c) for c in consts],
        out_specs=[row_out, row_out, row_out, row_out],
        out_shape=[jax.ShapeDtypeStruct((t, FOX_FD), BF16)] * 4,
        scratch_shapes=[pltpu.VMEM((1, LANES), F32), pltpu.VMEM((tm, tm), BF16)],
        compiler_params=_cparams(1),
        name="fox_kv",
    )(h, wk, wv, wf_pad, fb_pad, *consts)


def _fox_kernel(q_ref, g_ref, cq_ref, k_ref, ck_ref, v_ref, o_ref, s_ref, mb_ref, m_ref, acc_ref):
    seq = q_ref.shape[0]
    tq = FOX_TQ
    tk = tq
    nq = seq // tq
    lane = lax.broadcasted_iota(I32, (tq, LANES), 1)
    klane = lax.broadcasted_iota(I32, (tk, LANES), 1)

    def augmented(qi):
        rows = slice(qi * tq, (qi + 1) * tq)
        q = q_ref[rows, :]
        cq = cq_ref[rows, :]
        out = []
        for hh in range(2):
            in_head = (lane >= hh * FOX_DH) & (lane < (hh + 1) * FOX_DH)
            in_dec = (lane >= hh * DEC_LANES) & (lane < (hh + 1) * DEC_LANES)
            out.append(jnp.concatenate([jnp.where(in_head, q, jnp.zeros_like(q)),
                                        jnp.where(in_dec, cq, jnp.zeros_like(cq))], axis=1))
        return out

    one_lane = (FOX_DH, 0)
    v_keep = [klane < FOX_DH, klane >= FOX_DH]

    hq = tq // 2
    top, bot = slice(0, hq), slice(hq, tq)
    nt_dims = (((1,), (1,)), ((), ()))

    def scores(q_augs, j, diagonal):
        ks = slice(j * tk, (j + 1) * tk)
        k_aug = jnp.concatenate([k_ref[ks, :], ck_ref[ks, :]], axis=1)
        for hh in range(2):
            if diagonal:
                s_ref[hh, top, 0:hq] = lax.dot_general(q_augs[hh][top], k_aug[0:hq], nt_dims,
                                                       preferred_element_type=F32)
                s_ref[hh, bot, :] = lax.dot_general(q_augs[hh][bot], k_aug, nt_dims,
                                                    preferred_element_type=F32)
            else:
                s = lax.dot_general(q_augs[hh], k_aug, nt_dims, preferred_element_type=F32)
                s_ref[hh] = s
                mb_ref[hh] = jnp.broadcast_to(jnp.max(s, axis=1, keepdims=True), (tq, LANES))

    def absorb(j, diagonal):
        ks = slice(j * tk, (j + 1) * tk)
        v = v_ref[ks, :]
        for hh in range(2):
            ones = (klane == one_lane[hh]).astype(BF16)
            v_aug = jnp.where(v_keep[hh], v, ones)
            m_old = m_ref[hh]
            if diagonal:
                r = lax.broadcasted_iota(I32, (hq, hq), 0)
                cc = lax.broadcasted_iota(I32, (hq, hq), 1)
                tri = cc <= r
                s_tt = jnp.where(tri, s_ref[hh, top, 0:hq], -jnp.inf)
                s_bt = s_ref[hh, bot, 0:hq]
                s_bb = jnp.where(tri, s_ref[hh, bot, hq:tk], -jnp.inf)
                reps = hq // LANES

                m_top = jnp.maximum(m_old[top], jnp.broadcast_to(
                    jnp.max(s_tt, axis=1, keepdims=True), (hq, LANES)))
                m_ref[hh, top, :] = m_top
                p_tt = jnp.exp2((s_tt - jnp.concatenate([m_top] * reps, axis=1)).astype(BF16))
                acc_ref[hh, top, :] = (jnp.exp2(m_old[top] - m_top) * acc_ref[hh, top, :]
                                       + jnp.dot(p_tt, v_aug[0:hq], preferred_element_type=F32))

                m_bot = jnp.maximum(m_old[bot], jnp.broadcast_to(
                    jnp.maximum(jnp.max(s_bt, axis=1, keepdims=True),
                                jnp.max(s_bb, axis=1, keepdims=True)), (hq, LANES)))
                m_ref[hh, bot, :] = m_bot
                m_rep = jnp.concatenate([m_bot] * reps, axis=1)
                p_bt = jnp.exp2((s_bt - m_rep).astype(BF16))
                p_bb = jnp.exp2((s_bb - m_rep).astype(BF16))
                acc_ref[hh, bot, :] = (jnp.exp2(m_old[bot] - m_bot) * acc_ref[hh, bot, :]
                                       + jnp.dot(p_bt, v_aug[0:hq], preferred_element_type=F32)
                                       + jnp.dot(p_bb, v_aug[hq:tk], preferred_element_type=F32))
            else:
                m_new = jnp.maximum(m_old, mb_ref[hh])
                m_ref[hh] = m_new
                alpha = jnp.exp2(m_old - m_new)
                p = jnp.exp2((s_ref[hh] - jnp.concatenate([m_new] * (tk // LANES), axis=1)).astype(BF16))
                acc_ref[hh] = alpha * acc_ref[hh] + jnp.dot(p, v_aug, preferred_element_type=F32)

    def reset():
        m_ref[...] = jnp.full(m_ref.shape, -jnp.inf, F32)
        acc_ref[...] = jnp.zeros(acc_ref.shape, F32)

    def finish(qi):
        rows = slice(qi * tq, (qi + 1) * tq)
        acc0 = acc_ref[0]
        acc1 = acc_ref[1]
        o = jnp.where(lane < FOX_DH, acc0 / acc0[:, FOX_DH:FOX_DH + 1], acc1 / acc1[:, 0:1])
        o_ref[rows, :] = (o * _sigmoid(g_ref[rows, :].astype(F32))).astype(o_ref.dtype)

    pairs = [(qi, j) for qi in range(nq) for j in range(qi + 1)]
    q_augs = augmented(0)
    scores(q_augs, 0, True)
    for n, (qi, j) in enumerate(pairs):
        if j == 0:
            reset()
        absorb(j, j == qi)
        if n + 1 < len(pairs):
            qn, jn = pairs[n + 1]
            if qn != qi:
                q_augs = augmented(qn)
            scores(q_augs, jn, jn == qn)
        if j == qi:
            finish(qi)


def _fox_attn(qg, k, v, cq, ck, batch, seq):
    t = batch * seq
    n_pairs = FOX_FD // LANES
    blk = lambda m: pl.BlockSpec((seq, LANES), m)
    pair_blk = lambda b, p: (b, p)
    return pl.pallas_call(
        _fox_kernel,
        grid=(batch, n_pairs),
        in_specs=[blk(pair_blk),
                  blk(lambda b, p: (b, n_pairs + p)),
                  blk(pair_blk),
                  blk(pair_blk), blk(pair_blk), blk(pair_blk)],
        out_specs=blk(pair_blk),
        out_shape=jax.ShapeDtypeStruct((t, FOX_FD), BF16),
        scratch_shapes=[pltpu.VMEM((2, FOX_TQ, FOX_TQ), F32),
                        pltpu.VMEM((2, FOX_TQ, LANES), F32),
                        pltpu.VMEM((2, FOX_TQ, LANES), F32),
                        pltpu.VMEM((2, FOX_TQ, LANES), F32)],
        compiler_params=_cparams(2),
        name="fox_attn",
    )(qg, qg, cq, k, ck, v)


def kernel(x, gla_w_in, gla_w_gk, gla_b_gk, gla_norm_g, gla_w_out, kv_w, forget_bias, fox_w_qg,
           fox_w_out, router_w, router_bias, moe_w_gate, moe_w_up, moe_w_down, ln_g, ln_b):
    batch, seq, d = x.shape
    t = batch * seq
    h = x.reshape(t, d)
    w2 = _router_weight(router_w)
    rb_col = router_bias.reshape(N_EXPERTS, 1)
    n_main = 2 * GLA_HK + 2 * GLA_HV
    k_sh = v_sh = cq_sh = ck_sh = None
    xs_buf = jnp.zeros(((t // TM_MOE + N_CLASSES) * TM_MOE, d), F32)
    for layer in range(DEPTH):
        if layer < N_A_LAYERS:
            w_in = gla_w_in[layer]
            w_main = w_in[:, :n_main].astype(BF16)
            w_gr = jnp.pad(w_in[:, n_main:], ((0, 0), (0, LANES - GLA_RANK))).astype(BF16)
            qkvg, gr = _proj(h, [w_main, w_gr], [BF16, F32])
            wgk = jnp.pad(gla_w_gk[layer], ((0, LANES - GLA_RANK), (0, 0)))
            wgk_hi = wgk.astype(BF16)
            wgk_pad = jnp.stack([wgk_hi, (wgk - wgk_hi.astype(F32)).astype(BF16)])
            o = _gla_core(qkvg, gr, wgk_pad, gla_b_gk[layer].reshape(1, GLA_HK),
                          gla_norm_g[layer].reshape(1, GLA_DV), batch, seq)
            w_out = gla_w_out[layer].astype(BF16)
        else:
            j = layer - N_A_LAYERS
            w_qg = fox_w_qg[j]
            w_qg = jnp.concatenate([w_qg[:, :FOX_FD] * (FOX_DH ** -0.5 * LOG2E), w_qg[:, FOX_FD:]], axis=1)
            (qg,) = _proj(h, [w_qg.astype(BF16)], [BF16])
            o = _fox_attn(qg, k_sh, v_sh, cq_sh, ck_sh, batch, seq)
            w_out = fox_w_out[j].astype(BF16)
        h1, cls = _post_mixer(o, w_out, h, ln_g[layer, 0].reshape(1, d), ln_b[layer, 0].reshape(1, d),
                              w2, rb_col)
        h, xs_buf = _moe_block(h1, cls, w2, moe_w_gate, moe_w_up, moe_w_down, layer,
                               ln_g[layer, 1].reshape(1, d), ln_b[layer, 1].reshape(1, d), xs_buf)
        if layer == N_A_LAYERS - 1:
            wf_pad = jnp.pad(kv_w[:, 2 * FOX_FD:], ((0, 0), (0, LANES - FOX_HEADS))).astype(BF16)
            fb_pad = jnp.pad(forget_bias, (0, LANES - FOX_HEADS)).reshape(1, LANES)
            k_sh, v_sh, cq_sh, ck_sh = _fox_kv(h, kv_w[:, :FOX_FD].astype(BF16),
                                            kv_w[:, FOX_FD:2 * FOX_FD].astype(BF16), wf_pad, fb_pad, seq)
    return h.reshape(batch, seq, d)
```

```python
import functools
import math

import jax
import jax.numpy as jnp
import numpy as np
from jax import lax
from jax.experimental import pallas as pl
from jax.experimental.pallas import tpu as pltpu

F32 = jnp.float32
BF16 = jnp.bfloat16
I32 = jnp.int32

D_MODEL = 1024
DEPTH = 4
N_A_LAYERS = DEPTH // 2

GLA_HEADS = 4
GLA_DK = 128
GLA_DV = 256
GLA_RANK = 16
GLA_GATE_NORM = 16.0
GLA_CHUNK = 64
GLA_HK = GLA_HEADS * GLA_DK
GLA_HV = GLA_HEADS * GLA_DV

FOX_HEADS = 16
FOX_DH = 64
FOX_FD = FOX_HEADS * FOX_DH

N_EXPERTS = 16
N_GROUPS = 4
EPG = 4
D_EXPERT = 512
N_PAIRS = 6
N_CLASSES = N_GROUPS * N_PAIRS
CLASS_PAD = 32

ALPHA = float((2 * DEPTH) ** 0.25)
LN_EPS = 1e-5
RMS_EPS = 1e-6
LOG2E = 1.4426950408889634

LANES = 128
VMEM_LIMIT = 48 * 1024 * 1024

TM_PROJ = 1024
TM_KV = 512
TN_PROJ = 512
GLA_ROWS = 512
TM_POST = 512
TM_RANK = 512
TM_ROWS = 512
TM_MOE = 512
FOX_TQ = 1024


def _cparams(n_axes):
    return pltpu.CompilerParams(dimension_semantics=("arbitrary",) * n_axes,
                                vmem_limit_bytes=VMEM_LIMIT)


def _log_sigmoid(x):
    return jnp.minimum(x, 0.0) - jnp.log(1.0 + jnp.exp(-jnp.abs(x)))


def _sigmoid(x):
    return 1.0 / (1.0 + jnp.exp(-x))


def _layer_norm(z, g, b):
    mu = jnp.mean(z, axis=-1, keepdims=True)
    zc = z - mu
    var = jnp.mean(zc * zc, axis=-1, keepdims=True)
    return zc * lax.rsqrt(var + LN_EPS) * g + b


def _split3(x):
    hi = x.astype(BF16).astype(F32)
    r = x - hi
    mid = r.astype(BF16).astype(F32)
    lo = (r - mid).astype(BF16).astype(F32)
    return hi, mid, lo


def _proj_kernel(x_ref, *refs, n_w):
    xb = x_ref[...].astype(BF16)
    for w_ref, o_ref in zip(refs[:n_w], refs[n_w:]):
        n = o_ref.shape[1]
        tn = min(TN_PROJ, n)
        for c in range(n // tn):
            o_ref[:, c * tn:(c + 1) * tn] = jnp.dot(
                xb, w_ref[:, c * tn:(c + 1) * tn], preferred_element_type=F32).astype(o_ref.dtype)


def _proj(x, ws, out_dtypes):
    t, k = x.shape
    tm = TM_PROJ
    return pl.pallas_call(
        functools.partial(_proj_kernel, n_w=len(ws)),
        grid=(t // tm,),
        in_specs=[pl.BlockSpec((tm, k), lambda i: (i, 0))]
        + [pl.BlockSpec(w.shape, lambda i: (0, 0)) for w in ws],
        out_specs=[pl.BlockSpec((tm, w.shape[1]), lambda i: (i, 0)) for w in ws],
        out_shape=[jax.ShapeDtypeStruct((t, w.shape[1]), dt) for w, dt in zip(ws, out_dtypes)],
        compiler_params=_cparams(1),
        name="proj",
    )(x, *ws)


def _gla_kernel(q_ref, k_ref, v_ref, g_ref, gr_ref, wgk_ref, bgk_ref, ng_ref, o_ref, state_ref):
    c = GLA_CHUNK

    @pl.when(pl.program_id(1) == 0)
    def _():
        state_ref[...] = jnp.zeros_like(state_ref)

    gr = gr_ref[...]
    gr_hi = gr.astype(BF16)
    gr_lo = (gr - gr_hi.astype(F32)).astype(BF16)
    w_hi = wgk_ref[0]
    gkz = (jnp.dot(gr_hi, w_hi, preferred_element_type=F32)
           + jnp.dot(gr_lo, w_hi, preferred_element_type=F32)
           + jnp.dot(gr_hi, wgk_ref[1], preferred_element_type=F32)) + bgk_ref[...]
    gk = _log_sigmoid(gkz) * (1.0 / GLA_GATE_NORM)
    row = lax.broadcasted_iota(I32, (c, c), 0)
    col = lax.broadcasted_iota(I32, (c, c), 1)
    causal = col <= row
    tril = jnp.where(causal, 1.0, 0.0).astype(BF16)
    gk_parts = [part.astype(BF16) for part in _split3(gk)]
    scale = GLA_DK ** -0.5
    ng = ng_ref[...]

    for ci in range(GLA_ROWS // c):
        rs = slice(ci * c, (ci + 1) * c)
        bc = sum(jnp.dot(tril, part[rs], preferred_element_type=F32) for part in gk_parts)
        b_last = bc[c - 1:c, :]
        qf = q_ref[rs, :].astype(F32)
        kf = k_ref[rs, :].astype(F32)
        q_dec = (qf * scale * jnp.exp(bc)).astype(BF16)
        k_inv = (kf * jnp.exp(-bc)).astype(BF16)
        k_end = (kf * jnp.exp(b_last - bc)).astype(BF16)
        dec = jnp.exp(b_last)
        for h in range(GLA_HEADS):
            ks = slice(h * GLA_DK, (h + 1) * GLA_DK)
            vs = slice(h * GLA_DV, (h + 1) * GLA_DV)
            v_h = v_ref[rs, vs]
            attn = lax.dot_general(q_dec[:, ks], k_inv[:, ks], (((1,), (1,)), ((), ())),
                                   preferred_element_type=F32)
            attn = jnp.where(causal, attn, 0.0).astype(BF16)
            st = state_ref[h]
            o = jnp.dot(attn, v_h, preferred_element_type=F32)
            o = o + lax.dot_general(q_dec[:, ks], st.astype(BF16), (((1,), (1,)), ((), ())),
                                    preferred_element_type=F32)
            kv_t = lax.dot_general(v_h, k_end[:, ks], (((0,), (0,)), ((), ())),
                                   preferred_element_type=F32)
            state_ref[h] = st * dec[:, ks] + kv_t
            o = o * lax.rsqrt(jnp.mean(o * o, axis=-1, keepdims=True) + RMS_EPS) * ng
            gate = g_ref[rs, vs].astype(F32)
            o = o * (gate * _sigmoid(gate))
            o_ref[rs, vs] = o.astype(o_ref.dtype)


def _gla_core(qkvg, gr, wgk_pad, bgk, ng, batch, seq):
    t = batch * seq
    r = GLA_ROWS
    nblk = seq // r
    rowmap = lambda b, i: b * nblk + i
    return pl.pallas_call(
        _gla_kernel,
        grid=(batch, nblk),
        in_specs=[
            pl.BlockSpec((r, GLA_HK), lambda b, i: (rowmap(b, i), 0)),
            pl.BlockSpec((r, GLA_HK), lambda b, i: (rowmap(b, i), 1)),
            pl.BlockSpec((r, GLA_HV), lambda b, i: (rowmap(b, i), 1)),
            pl.BlockSpec((r, GLA_HV), lambda b, i: (rowmap(b, i), 2)),
            pl.BlockSpec((r, LANES), lambda b, i: (rowmap(b, i), 0)),
            pl.BlockSpec((2, LANES, GLA_HK), lambda b, i: (0, 0, 0)),
            pl.BlockSpec((1, GLA_HK), lambda b, i: (0, 0)),
            pl.BlockSpec((1, GLA_DV), lambda b, i: (0, 0)),
        ],
        out_specs=pl.BlockSpec((r, GLA_HV), lambda b, i: (rowmap(b, i), 0)),
        out_shape=jax.ShapeDtypeStruct((t, GLA_HV), BF16),
        scratch_shapes=[pltpu.VMEM((GLA_HEADS, GLA_DV, GLA_DK), F32)],
        compiler_params=_cparams(2),
        name="gla_core",
    )(qkvg, qkvg, qkvg, qkvg, gr, wgk_pad, bgk, ng)


def _route_class(logits_t, bias_col):
    mx = jnp.max(logits_t, axis=0, keepdims=True)
    e = jnp.exp(logits_t - mx)
    scores = e / jnp.sum(e, axis=0, keepdims=True)
    sel = scores + bias_col
    rows = [sel[j:j + 1, :] for j in range(N_EXPERTS)]
    best_g = None
    best_s = None
    for g in range(N_GROUPS):
        m = rows[g * EPG:(g + 1) * EPG]
        gs = None
        for a in range(EPG):
            for b in range(a + 1, EPG):
                s = m[a] + m[b]
                gs = s if gs is None else jnp.maximum(gs, s)
        if g == 0:
            best_g = jnp.zeros(gs.shape, I32)
            best_s = gs
        else:
            better = gs > best_s
            best_g = jnp.where(better, g, best_g)
            best_s = jnp.where(better, gs, best_s)
    mem = []
    for j in range(EPG):
        vj = rows[j]
        for g in range(1, N_GROUPS):
            vj = jnp.where(best_g == g, rows[g * EPG + j], vj)
        mem.append(vj)
    i1 = jnp.zeros(best_g.shape, I32)
    b1 = mem[0]
    for j in range(1, EPG):
        better = mem[j] > b1
        i1 = jnp.where(better, j, i1)
        b1 = jnp.where(better, mem[j], b1)
    i2 = jnp.where(i1 == 0, 1, 0).astype(I32)
    b2 = jnp.where(i1 == 0, mem[1], mem[0])
    for j in range(1, EPG):
        better = (mem[j] > b2) & (i1 != j) & (i2 != j)
        i2 = jnp.where(better, j, i2)
        b2 = jnp.where(better, mem[j], b2)
    lo = jnp.minimum(i1, i2)
    hi = jnp.maximum(i1, i2)
    pair = jnp.where(lo == 0, hi - 1, jnp.where(lo == 1, hi + 1, 5))
    return best_g * N_PAIRS + pair


def _router_weight(router_w):
    hi = router_w.astype(BF16)
    lo = (router_w - hi.astype(F32)).astype(BF16)
    pad = jnp.zeros((router_w.shape[0], LANES - 2 * N_EXPERTS), BF16)
    return jnp.concatenate([hi, lo, pad], axis=1)


def _router_logits(x, w2_ref):
    hi = x.astype(BF16)
    lo = (x - hi.astype(F32)).astype(BF16)
    w2 = w2_ref[...]
    a = jnp.dot(hi, w2, preferred_element_type=F32) + jnp.dot(lo, w2, preferred_element_type=F32)
    return a + pltpu.roll(a, LANES - N_EXPERTS, axis=1)


def _post_kernel(o_ref, w_ref, h_ref, g_ref, b_ref, w2_ref, rb_ref, h1_ref, cls_ref):
    half = h_ref.shape[0] // 2
    for part in range(2):
        rs = slice(part * half, (part + 1) * half)
        mix = jnp.dot(o_ref[rs, :], w_ref[...], preferred_element_type=F32)
        h1 = _layer_norm(ALPHA * h_ref[rs, :] + mix, g_ref[...], b_ref[...])
        h1_ref[rs, :] = h1
        logits_t = _router_logits(h1, w2_ref).T[:N_EXPERTS, :]
        cls_ref[:, rs] = _route_class(logits_t, rb_ref[...])


def _post_mixer(o, w_out, h, ln_g, ln_b, w2, rb_col):
    t, d = h.shape
    tm = TM_POST
    kdim = o.shape[1]
    return pl.pallas_call(
        _post_kernel,
        grid=(t // tm,),
        in_specs=[
            pl.BlockSpec((tm, kdim), lambda i: (i, 0)),
            pl.BlockSpec((kdim, d), lambda i: (0, 0)),
            pl.BlockSpec((tm, d), lambda i: (i, 0)),
            pl.BlockSpec((1, d), lambda i: (0, 0)),
            pl.BlockSpec((1, d), lambda i: (0, 0)),
            pl.BlockSpec((d, LANES), lambda i: (0, 0)),
            pl.BlockSpec((N_EXPERTS, 1), lambda i: (0, 0)),
        ],
        out_specs=[pl.BlockSpec((tm, d), lambda i: (i, 0)),
                   pl.BlockSpec((1, tm), lambda i: (0, i))],
        out_shape=[jax.ShapeDtypeStruct((t, d), F32), jax.ShapeDtypeStruct((1, t), I32)],
        compiler_params=_cparams(1),
        name="post_mixer",
    )(o, w_out, h, ln_g, ln_b, w2, rb_col)


def _rank_kernel(cls_ref, pos_ref, toff_ref, cnt_ref, carry_ref, off_ref, tri_ref):
    phase = pl.program_id(0)
    i = pl.program_id(1)
    tm = cls_ref.shape[1]
    onehot = (lax.broadcasted_iota(I32, (CLASS_PAD, tm), 0) == cls_ref[...]).astype(F32)
    tile_count = jnp.sum(onehot, axis=1, keepdims=True)

    @pl.when((phase == 0) & (i == 0))
    def _():
        cnt_ref[...] = jnp.zeros_like(cnt_ref)
        r = lax.broadcasted_iota(I32, (tm, tm), 0)
        cc = lax.broadcasted_iota(I32, (tm, tm), 1)
        tri_ref[...] = (r <= cc).astype(BF16)

    @pl.when(phase == 0)
    def _():
        cnt_ref[...] += jnp.broadcast_to(tile_count, cnt_ref.shape)

    @pl.when((phase == 1) & (i == 0))
    def _():
        ntile = jnp.floor((cnt_ref[...] + (TM_MOE - 1)) * (1.0 / TM_MOE))
        r = lax.broadcasted_iota(I32, (CLASS_PAD, CLASS_PAD), 0)
        cc = lax.broadcasted_iota(I32, (CLASS_PAD, CLASS_PAD), 1)
        strict = (cc < r).astype(BF16)
        first_tile = jnp.dot(strict, ntile.astype(BF16), preferred_element_type=F32)
        toff_ref[...] = first_tile.astype(I32)
        off_ref[...] = first_tile * float(TM_MOE)
        carry_ref[...] = jnp.zeros_like(carry_ref)

    @pl.when(phase == 1)
    def _():
        prefix = jnp.dot(onehot.astype(BF16), tri_ref[...], preferred_element_type=F32)
        base = off_ref[:, 0:1] + carry_ref[:, 0:1] - 1.0
        posf = jnp.sum(onehot * (prefix + base), axis=0, keepdims=True)
        pos_ref[...] = posf.astype(I32)
        carry_ref[...] += jnp.broadcast_to(tile_count, carry_ref.shape)


def _rank(cls):
    t = cls.shape[1]
    tm = TM_RANK
    return pl.pallas_call(
        _rank_kernel,
        grid=(2, t // tm),
        in_specs=[pl.BlockSpec((1, tm), lambda p, i: (0, i))],
        out_specs=[pl.BlockSpec((1, tm), lambda p, i: (0, i * p)),
                   pl.BlockSpec((CLASS_PAD, LANES), lambda p, i: (0, 0))],
        out_shape=[jax.ShapeDtypeStruct((1, t), I32),
                   jax.ShapeDtypeStruct((CLASS_PAD, LANES), I32)],
        scratch_shapes=[pltpu.VMEM((CLASS_PAD, LANES), F32),
                        pltpu.VMEM((CLASS_PAD, LANES), F32),
                        pltpu.VMEM((CLASS_PAD, LANES), F32),
                        pltpu.VMEM((tm, tm), BF16)],
        compiler_params=_cparams(2),
        name="rank",
    )(cls)


def _dispatch_kernel(pos_ref, h_ref, xs_in_ref, xs_ref, sem):
    del xs_in_ref
    tm = h_ref.shape[0]
    base = pl.program_id(0) * tm

    for r in range(tm):
        p = pos_ref[base + r]
        pltpu.make_async_copy(h_ref.at[pl.ds(r, 1), :], xs_ref.at[pl.ds(p, 1), :],
                              sem).start(priority=r % 2)
    pltpu.make_async_copy(h_ref, xs_ref.at[pl.ds(0, tm), :], sem).wait()


def _dispatch(pos, h1, xs0):
    t, d = h1.shape
    tm = TM_ROWS
    n_sorted = xs0.shape[0]
    return pl.pallas_call(
        _dispatch_kernel,
        grid_spec=pltpu.PrefetchScalarGridSpec(
            num_scalar_prefetch=1,
            grid=(t // tm,),
            in_specs=[pl.BlockSpec((tm, d), lambda i, pos: (i, 0)),
                      pl.BlockSpec(memory_space=pl.ANY)],
            out_specs=pl.BlockSpec(memory_space=pl.ANY),
            scratch_shapes=[pltpu.SemaphoreType.DMA(())],
        ),
        out_shape=jax.ShapeDtypeStruct((n_sorted, d), F32),
        input_output_aliases={2: 0},
        compiler_params=_cparams(1),
        name="dispatch",
    )(pos, h1, xs0)


def _combine_kernel(pos_ref, h_ref, ys_ref, g_ref, b_ref, o_ref, buf_ref, sem):
    tm = h_ref.shape[0]
    i = pl.program_id(0)
    slot = i % 2

    def gather(tile, s):
        for r in range(tm):
            p = pos_ref[tile * tm + r]
            pltpu.make_async_copy(ys_ref.at[pl.ds(p, 1), :], buf_ref.at[s, pl.ds(r, 1), :],
                                  sem.at[s]).start(priority=r % 2)

    @pl.when(i == 0)
    def _():
        gather(0, 0)

    @pl.when(i + 1 < pl.num_programs(0))
    def _():
        gather(i + 1, 1 - slot)

    pltpu.make_async_copy(ys_ref.at[pl.ds(0, tm), :], buf_ref.at[slot], sem.at[slot]).wait()
    o_ref[...] = _layer_norm(ALPHA * h_ref[...] + buf_ref[slot], g_ref[...], b_ref[...])


def _combine(pos, h1, ys, ln_g, ln_b):
    t, d = h1.shape
    tm = TM_ROWS
    return pl.pallas_call(
        _combine_kernel,
        grid_spec=pltpu.PrefetchScalarGridSpec(
            num_scalar_prefetch=1,
            grid=(t // tm,),
            in_specs=[pl.BlockSpec((tm, d), lambda i, pos: (i, 0)),
                      pl.BlockSpec(memory_space=pl.ANY),
                      pl.BlockSpec((1, d), lambda i, pos: (0, 0)),
                      pl.BlockSpec((1, d), lambda i, pos: (0, 0))],
            out_specs=pl.BlockSpec((tm, d), lambda i, pos: (i, 0)),
            scratch_shapes=[pltpu.VMEM((2, tm, d), F32), pltpu.SemaphoreType.DMA((2,))],
        ),
        out_shape=jax.ShapeDtypeStruct((t, d), F32),
        compiler_params=_cparams(1),
        name="combine",
    )(pos, h1, ys, ln_g, ln_b)


def _moe_kernel(ea_ref, eb_ref, valid_ref, xs_ref, w2_ref, wga_ref, wua_ref, wda_ref,
                wgb_ref, wub_ref, wdb_ref, ys_ref, wbf_up_ref, wbf_dn_ref):
    i = pl.program_id(0)
    prev = jnp.maximum(i - 1, 0)
    fresh = (i == 0) | (ea_ref[i] != ea_ref[prev]) | (eb_ref[i] != eb_ref[prev])

    @pl.when(fresh)
    def _():
        for slot, w_ref in enumerate((wga_ref, wua_ref, wgb_ref, wub_ref)):
            wbf_up_ref[slot] = w_ref[...].astype(BF16)
        for slot, w_ref in enumerate((wda_ref, wdb_ref)):
            wbf_dn_ref[slot] = w_ref[...].astype(BF16)

    @pl.when(valid_ref[i] == 0)
    def _():
        ys_ref[...] = jnp.zeros_like(ys_ref)

    @pl.when(valid_ref[i] != 0)
    def _():
        x = xs_ref[...]
        logits = _router_logits(x, w2_ref)
        lane = lax.broadcasted_iota(I32, logits.shape, 1)
        logits = jnp.where(lane < N_EXPERTS, logits, -jnp.inf)
        e = jnp.exp(logits - jnp.max(logits, axis=1, keepdims=True))
        scores = e / jnp.sum(e, axis=1, keepdims=True)
        sa = jnp.sum(jnp.where(lane == ea_ref[i], scores, 0.0), axis=1, keepdims=True)
        sb = jnp.sum(jnp.where(lane == eb_ref[i], scores, 0.0), axis=1, keepdims=True)
        tot = sa + sb
        xb = x.astype(BF16)

        def expert(slot, gate):
            a = jnp.dot(xb, wbf_up_ref[2 * slot], preferred_element_type=F32)
            u = jnp.dot(xb, wbf_up_ref[2 * slot + 1], preferred_element_type=F32)
            hid = (a * _sigmoid(a)) * u * gate
            return jnp.dot(hid.astype(BF16), wbf_dn_ref[slot], preferred_element_type=F32)

        ys_ref[...] = expert(0, sa / tot) + expert(1, sb / tot)


def _moe(ea, eb, valid, xs, w2, wg, wu, wd, layer):
    n_sorted, d = xs.shape
    tm = TM_MOE
    f = wg.shape[3]
    idx_a = lambda i, ea, eb, va: (layer, ea[i], 0, 0)
    idx_b = lambda i, ea, eb, va: (layer, eb[i], 0, 0)
    up_a = pl.BlockSpec((None, None, d, f), idx_a)
    up_b = pl.BlockSpec((None, None, d, f), idx_b)
    dn_a = pl.BlockSpec((None, None, f, d), idx_a)
    dn_b = pl.BlockSpec((None, None, f, d), idx_b)
    return pl.pallas_call(
        _moe_kernel,
        grid_spec=pltpu.PrefetchScalarGridSpec(
            num_scalar_prefetch=3,
            grid=(n_sorted // tm,),
            in_specs=[pl.BlockSpec((tm, d), lambda i, ea, eb, va: (i, 0)),
                      pl.BlockSpec((d, LANES), lambda i, ea, eb, va: (0, 0)),
                      up_a, up_a, dn_a, up_b, up_b, dn_b],
            out_specs=pl.BlockSpec((tm, d), lambda i, ea, eb, va: (i, 0)),
            scratch_shapes=[pltpu.VMEM((4, d, f), BF16), pltpu.VMEM((2, f, d), BF16)],
        ),
        out_shape=jax.ShapeDtypeStruct((n_sorted, d), F32),
        compiler_params=_cparams(1),
        name="moe",
    )(ea, eb, valid, xs, w2, wg, wu, wd, wg, wu, wd)


_PAIR_LO = (0, 0, 0, 1, 1, 2)
_PAIR_HI = (1, 2, 3, 2, 3, 3)


def _tile_plan(toff, n_tiles):
    first = toff[:N_CLASSES, 0]
    tiles = jnp.arange(n_tiles, dtype=I32)
    cls = jnp.sum((first[None, :] <= tiles[:, None]).astype(I32), axis=1) - 1
    group = cls // N_PAIRS
    pair = cls % N_PAIRS
    lo = jnp.asarray(_PAIR_LO, I32)[pair]
    hi = jnp.asarray(_PAIR_HI, I32)[pair]
    total = toff[N_CLASSES, 0]
    valid = (tiles < total).astype(I32)
    return group * EPG + lo, group * EPG + hi, valid


def _moe_block(h1, cls, w2, wg, wu, wd, layer, ln_g, ln_b, xs_buf):
    t, _ = h1.shape
    n_tiles = xs_buf.shape[0] // TM_MOE
    pos2d, toff = _rank(cls)
    pos = pos2d.reshape(t)
    ea, eb, valid = _tile_plan(toff, n_tiles)
    xs = _dispatch(pos, h1, xs_buf)
    ys = _moe(ea, eb, valid, xs, w2, wg, wu, wd, layer)
    return _combine(pos, h1, ys, ln_g, ln_b), xs


DEC_LANES = 6


def _decay_placement():
    pq = np.zeros((LANES, FOX_FD), np.float32)
    pk = np.zeros((LANES, FOX_FD), np.float32)
    oq = np.zeros((1, FOX_FD), np.float32)
    ok = np.zeros((1, FOX_FD), np.float32)
    for h in range(FOX_HEADS):
        base = (h // 2) * LANES + (h % 2) * DEC_LANES
        for part in range(3):
            pq[part * FOX_HEADS + h, base + part] = 1.0
            pk[part * FOX_HEADS + h, base + 3 + part] = -1.0
            oq[0, base + 3 + part] = 1.0
            ok[0, base + part] = 1.0
    return pq, pk, oq, ok


def _kv_kernel(x_ref, wk_ref, wv_ref, wf_ref, fb_ref, pq_ref, pk_ref, oq_ref, ok_ref,
               k_ref, v_ref, cq_ref, ck_ref, carry_ref, tril_ref, *, tiles_per_seq):
    i = pl.program_id(0)
    tm = x_ref.shape[0]

    @pl.when(i == 0)
    def _():
        r = lax.broadcasted_iota(I32, (tm, tm), 0)
        cc = lax.broadcasted_iota(I32, (tm, tm), 1)
        tril_ref[...] = (cc <= r).astype(BF16)

    @pl.when(i % tiles_per_seq == 0)
    def _():
        carry_ref[...] = jnp.zeros_like(carry_ref)

    xb = x_ref[...].astype(BF16)
    for w_ref, o_ref in ((wk_ref, k_ref), (wv_ref, v_ref)):
        n = o_ref.shape[1]
        for c in range(n // TN_PROJ):
            cs = slice(c * TN_PROJ, (c + 1) * TN_PROJ)
            o_ref[:, cs] = jnp.dot(xb, w_ref[:, cs], preferred_element_type=F32).astype(o_ref.dtype)
    fl = jnp.dot(xb, wf_ref[...], preferred_element_type=F32) + fb_ref[...]
    lane = lax.broadcasted_iota(I32, fl.shape, 1)
    log_f = jnp.where(lane < FOX_HEADS, _log_sigmoid(fl), 0.0)
    tril = tril_ref[...]
    cum = carry_ref[...]
    for part in _split3(log_f):
        cum = cum + jnp.dot(tril, part.astype(BF16), preferred_element_type=F32)
    carry_ref[...] = cum[tm - 1:tm, :]
    hi, mid, lo = _split3(cum * LOG2E)
    packed = hi + pltpu.roll(mid, FOX_HEADS, axis=1) + pltpu.roll(lo, 2 * FOX_HEADS, axis=1)
    packed = packed.astype(BF16)
    cq_ref[...] = (jnp.dot(packed, pq_ref[...], preferred_element_type=F32)
                   + oq_ref[...]).astype(cq_ref.dtype)
    ck_ref[...] = (jnp.dot(packed, pk_ref[...], preferred_element_type=F32)
                   + ok_ref[...]).astype(ck_ref.dtype)


def _fox_kv(h, wk, wv, wf_pad, fb_pad, seq):
    t, d = h.shape
    tm = TM_KV
    pq, pk, oq, ok = _decay_placement()
    full = lambda a: pl.BlockSpec(a.shape, lambda i: (0, 0))
    consts = [jnp.asarray(pq, BF16), jnp.asarray(pk, BF16), jnp.asarray(oq), jnp.asarray(ok)]
    row_out = pl.BlockSpec((tm, FOX_FD), lambda i: (i, 0))
    return pl.pallas_call(
        functools.partial(_kv_kernel, tiles_per_seq=seq // tm),
        grid=(t // tm,),
        in_specs=[pl.BlockSpec((tm, d), lambda i: (i, 0)), full(wk), full(wv), full(wf_pad),
                  full(fb_pad)] + [full(c) for c in consts],
        out_specs=[row_out, row_out, row_out, row_out],
        out_shape=[jax.ShapeDtypeStruct((t, FOX_FD), BF16)] * 4,
        scratch_shapes=[pltpu.VMEM((1, LANES), F32), pltpu.VMEM((tm, tm), BF16)],
        compiler_params=_cparams(1),
        name="fox_kv",
    )(h, wk, wv, wf_pad, fb_pad, *consts)


def _fox_kernel(q_ref, g_ref, cq_ref, k_ref, ck_ref, v_ref, o_ref, s_ref, mb_ref, m_ref, acc_ref):
    seq = q_ref.shape[0]
    tq = FOX_TQ
    tk = tq
    nq = seq // tq
    lane = lax.broadcasted_iota(I32, (tq, LANES), 1)
    klane = lax.broadcasted_iota(I32, (tk, LANES), 1)

    def augmented(qi):
        rows = slice(qi * tq, (qi + 1) * tq)
        q = q_ref[rows, :]
        cq = cq_ref[rows, :]
        out = []
        for hh in range(2):
            in_head = (lane >= hh * FOX_DH) & (lane < (hh + 1) * FOX_DH)
            in_dec = (lane >= hh * DEC_LANES) & (lane < (hh + 1) * DEC_LANES)
            out.append(jnp.concatenate([jnp.where(in_head, q, jnp.zeros_like(q)),
                                        jnp.where(in_dec, cq, jnp.zeros_like(cq))], axis=1))
        return out

    one_lane = (FOX_DH, 0)
    v_keep = [klane < FOX_DH, klane >= FOX_DH]

    hq = tq // 2
    top, bot = slice(0, hq), slice(hq, tq)
    nt_dims = (((1,), (1,)), ((), ()))

    def scores(q_augs, j, diagonal):
        ks = slice(j * tk, (j + 1) * tk)
        k_aug = jnp.concatenate([k_ref[ks, :], ck_ref[ks, :]], axis=1)
        for hh in range(2):
            if diagonal:
                s_ref[hh, top, 0:hq] = lax.dot_general(q_augs[hh][top], k_aug[0:hq], nt_dims,
                                                       preferred_element_type=F32)
                s_ref[hh, bot, :] = lax.dot_general(q_augs[hh][bot], k_aug, nt_dims,
                                                    preferred_element_type=F32)
            else:
                s = lax.dot_general(q_augs[hh], k_aug, nt_dims, preferred_element_type=F32)
                s_ref[hh] = s
                mb_ref[hh] = jnp.broadcast_to(jnp.max(s, axis=1, keepdims=True), (tq, LANES))

    def absorb(j, diagonal):
        ks = slice(j * tk, (j + 1) * tk)
        v = v_ref[ks, :]
        for hh in range(2):
            ones = (klane == one_lane[hh]).astype(BF16)
            v_aug = jnp.where(v_keep[hh], v, ones)
            m_old = m_ref[hh]
            if diagonal:
                r = lax.broadcasted_iota(I32, (hq, hq), 0)
                cc = lax.broadcasted_iota(I32, (hq, hq), 1)
                tri = cc <= r
                s_tt = jnp.where(tri, s_ref[hh, top, 0:hq], -jnp.inf)
                s_bt = s_ref[hh, bot, 0:hq]
                s_bb = jnp.where(tri, s_ref[hh, bot, hq:tk], -jnp.inf)
                reps = hq // LANES

                m_top = jnp.maximum(m_old[top], jnp.broadcast_to(
                    jnp.max(s_tt, axis=1, keepdims=True), (hq, LANES)))
                m_ref[hh, top, :] = m_top
                p_tt = jnp.exp2((s_tt - jnp.concatenate([m_top] * reps, axis=1)).astype(BF16))
                acc_ref[hh, top, :] = (jnp.exp2(m_old[top] - m_top) * acc_ref[hh, top, :]
                                       + jnp.dot(p_tt, v_aug[0:hq], preferred_element_type=F32))

                m_bot = jnp.maximum(m_old[bot], jnp.broadcast_to(
                    jnp.maximum(jnp.max(s_bt, axis=1, keepdims=True),
                                jnp.max(s_bb, axis=1, keepdims=True)), (hq, LANES)))
                m_ref[hh, bot, :] = m_bot
                m_rep = jnp.concatenate([m_bot] * reps, axis=1)
                p_bt = jnp.exp2((s_bt - m_rep).astype(BF16))
                p_bb = jnp.exp2((s_bb - m_rep).astype(BF16))
                acc_ref[hh, bot, :] = (jnp.exp2(m_old[bot] - m_bot) * acc_ref[hh, bot, :]
                                       + jnp.dot(p_bt, v_aug[0:hq], preferred_element_type=F32)
                                       + jnp.dot(p_bb, v_aug[hq:tk], preferred_element_type=F32))
            else:
                m_new = jnp.maximum(m_old, mb_ref[hh])
                m_ref[hh] = m_new
                alpha = jnp.exp2(m_old - m_new)
                p = jnp.exp2((s_ref[hh] - jnp.concatenate([m_new] * (tk // LANES), axis=1)).astype(BF16))
                acc_ref[hh] = alpha * acc_ref[hh] + jnp.dot(p, v_aug, preferred_element_type=F32)

    def reset():
        m_ref[...] = jnp.full(m_ref.shape, -jnp.inf, F32)
        acc_ref[...] = jnp.zeros(acc_ref.shape, F32)

    def finish(qi):
        rows = slice(qi * tq, (qi + 1) * tq)
        acc0 = acc_ref[0]
        acc1 = acc_ref[1]
        o = jnp.where(lane < FOX_DH, acc0 / acc0[:, FOX_DH:FOX_DH + 1], acc1 / acc1[:, 0:1])
        o_ref[rows, :] = (o * _sigmoid(g_ref[rows, :].astype(F32))).astype(o_ref.dtype)

    pairs = [(qi, j) for qi in range(nq) for j in range(qi + 1)]
    q_augs = augmented(0)
    scores(q_augs, 0, True)
    for n, (qi, j) in enumerate(pairs):
        if j == 0:
            reset()
        absorb(j, j == qi)
        if n + 1 < len(pairs):
            qn, jn = pairs[n + 1]
            if qn != qi:
                q_augs = augmented(qn)
            scores(q_augs, jn, jn == qn)
        if j == qi:
            finish(qi)


def _fox_attn(qg, k, v, cq, ck, batch, seq):
    t = batch * seq
    n_pairs = FOX_FD // LANES
    blk = lambda m: pl.BlockSpec((seq, LANES), m)
    pair_blk = lambda b, p: (b, p)
    return pl.pallas_call(
        _fox_kernel,
        grid=(batch, n_pairs),
        in_specs=[blk(pair_blk),
                  blk(lambda b, p: (b, n_pairs + p)),
                  blk(pair_blk),
                  blk(pair_blk), blk(pair_blk), blk(pair_blk)],
        out_specs=blk(pair_blk),
        out_shape=jax.ShapeDtypeStruct((t, FOX_FD), BF16),
        scratch_shapes=[pltpu.VMEM((2, FOX_TQ, FOX_TQ), F32),
                        pltpu.VMEM((2, FOX_TQ, LANES), F32),
                        pltpu.VMEM((2, FOX_TQ, LANES), F32),
                        pltpu.VMEM((2, FOX_TQ, LANES), F32)],
        compiler_params=_cparams(2),
        name="fox_attn",
    )(qg, qg, cq, k, ck, v)


def kernel(x, gla_w_in, gla_w_gk, gla_b_gk, gla_norm_g, gla_w_out, kv_w, forget_bias, fox_w_qg,
           fox_w_out, router_w, router_bias, moe_w_gate, moe_w_up, moe_w_down, ln_g, ln_b):
    batch, seq, d = x.shape
    t = batch * seq
    h = x.reshape(t, d)
    w2 = _router_weight(router_w)
    rb_col = router_bias.reshape(N_EXPERTS, 1)
    n_main = 2 * GLA_HK + 2 * GLA_HV
    k_sh = v_sh = cq_sh = ck_sh = None
    xs_buf = jnp.zeros(((t // TM_MOE + N_CLASSES) * TM_MOE, d), F32)
    for layer in range(DEPTH):
        if layer < N_A_LAYERS:
            w_in = gla_w_in[layer]
            w_main = w_in[:, :n_main].astype(BF16)
            w_gr = jnp.pad(w_in[:, n_main:], ((0, 0), (0, LANES - GLA_RANK))).astype(BF16)
            qkvg, gr = _proj(h, [w_main, w_gr], [BF16, F32])
            wgk = jnp.pad(gla_w_gk[layer], ((0, LANES - GLA_RANK), (0, 0)))
            wgk_hi = wgk.astype(BF16)
            wgk_pad = jnp.stack([wgk_hi, (wgk - wgk_hi.astype(F32)).astype(BF16)])
            o = _gla_core(qkvg, gr, wgk_pad, gla_b_gk[layer].reshape(1, GLA_HK),
                          gla_norm_g[layer].reshape(1, GLA_DV), batch, seq)
            w_out = gla_w_out[layer].astype(BF16)
        else:
            j = layer - N_A_LAYERS
            w_qg = fox_w_qg[j]
            w_qg = jnp.concatenate([w_qg[:, :FOX_FD] * (FOX_DH ** -0.5 * LOG2E), w_qg[:, FOX_FD:]], axis=1)
            (qg,) = _proj(h, [w_qg.astype(BF16)], [BF16])
            o = _fox_attn(qg, k_sh, v_sh, cq_sh, ck_sh, batch, seq)
            w_out = fox_w_out[j].astype(BF16)
        h1, cls = _post_mixer(o, w_out, h, ln_g[layer, 0].reshape(1, d), ln_b[layer, 0].reshape(1, d),
                              w2, rb_col)
        h, xs_buf = _moe_block(h1, cls, w2, moe_w_gate, moe_w_up, moe_w_down, layer,
                               ln_g[layer, 1].reshape(1, d), ln_b[layer, 1].reshape(1, d), xs_buf)
        if layer == N_A_LAYERS - 1:
            wf_pad = jnp.pad(kv_w[:, 2 * FOX_FD:], ((0, 0), (0, LANES - FOX_HEADS))).astype(BF16)
            fb_pad = jnp.pad(forget_bias, (0, LANES - FOX_HEADS)).reshape(1, LANES)
            k_sh, v_sh, cq_sh, ck_sh = _fox_kv(h, kv_w[:, :FOX_FD].astype(BF16),
                                            kv_w[:, FOX_FD:2 * FOX_FD].astype(BF16), wf_pad, fb_pad, seq)
    return h.reshape(batch, seq, d)
```
